```python
import math
import jax, jax.numpy as jnp
from jax import lax
import numpy as np

D_MODEL = 1024
BATCH = 8
SEQ = 2048
DEPTH = 2

N_MIXERS = 2
N_CONV_LAYERS = (DEPTH + 1) // 2
N_ATTN_LAYERS = DEPTH // 2
CONV_WIDTH = 31
N_HEADS = 8
HEAD_DIM = D_MODEL // N_HEADS
MOBA_BLOCK = 256
MOBA_TOPK = 3
Q_CHUNK = 64
N_GROUPS = 4
EXPERTS_PER_GROUP = 4
N_EXPERTS = N_GROUPS * EXPERTS_PER_GROUP
EXPERT_HIDDEN = 512
EXPERT_TOPK = 2
PLE_DIM = 256
EPS = 1e-6
NEG_INF = -1e30

kernel_name = 'hybrid_conformer_moba_hmoe'


def rms_norm(x, g):
    x32 = x.astype(jnp.float32)
    y = x32 * lax.rsqrt(jnp.mean(x32 * x32, axis=-1, keepdims=True) + EPS)
    return (y * g.astype(jnp.float32)).astype(x.dtype)


def layer_norm(x, g, b):
    x32 = x.astype(jnp.float32)
    mu = jnp.mean(x32, axis=-1, keepdims=True)
    xc = x32 - mu
    y = xc * lax.rsqrt(jnp.mean(xc * xc, axis=-1, keepdims=True) + EPS)
    return (y * g.astype(jnp.float32) + b.astype(jnp.float32)).astype(x.dtype)


def conformer_conv(xn, w_pw1, b_pw1, w_dw, b_dw, ln_g, ln_b, w_pw2, b_pw2):
    u = xn @ w_pw1 + b_pw1
    a, g = jnp.split(u, 2, axis=-1)
    u = a * jax.nn.sigmoid(g)
    u = lax.conv_general_dilated(
        u, w_dw[:, None, :], window_strides=(1,),
        padding=((CONV_WIDTH - 1, 0),),
        dimension_numbers=('NWC', 'WIO', 'NWC'),
        feature_group_count=D_MODEL) + b_dw
    u = jax.nn.silu(layer_norm(u, ln_g, ln_b))
    return u @ w_pw2 + b_pw2


def alibi_slopes():
    return jnp.exp2(-8.0 * jnp.arange(1, N_HEADS + 1, dtype=jnp.float32) / N_HEADS)


def moba_attention(xn, w_qkv, q_gain, k_gain, w_o):
    B, S, _ = xn.shape
    n_blk = -(-S // MOBA_BLOCK)
    s_pad = n_blk * MOBA_BLOCK
    n_qc = s_pad // Q_CHUNK
    k_sel_n = max(1, min(MOBA_TOPK, n_blk - 1))

    qkv = (xn @ w_qkv).reshape(B, S, 3, N_HEADS, HEAD_DIM)
    qkv = jnp.pad(qkv, ((0, 0), (0, s_pad - S), (0, 0), (0, 0), (0, 0)))
    q = rms_norm(qkv[:, :, 0], q_gain).transpose(0, 2, 1, 3)
    k = rms_norm(qkv[:, :, 1], k_gain).transpose(0, 2, 1, 3)
    v = qkv[:, :, 2].transpose(0, 2, 1, 3)
    k_blocks = k.reshape(B, N_HEADS, n_blk, MOBA_BLOCK, HEAD_DIM)
    v_blocks = v.reshape(B, N_HEADS, n_blk, MOBA_BLOCK, HEAD_DIM)

    q_blk = jnp.arange(s_pad) // MOBA_BLOCK
    k_mean = jnp.mean(k_blocks.astype(jnp.float32), axis=3)
    gate = jnp.einsum('bhsd,bhnd->bhsn', q.astype(jnp.float32), k_mean)
    fully_past = jnp.arange(n_blk)[None, :] < q_blk[:, None]
    gate = jnp.where(fully_past, gate, NEG_INF)
    _, sel_idx = lax.top_k(gate, k_sel_n)

    slopes = alibi_slopes()
    scale = HEAD_DIM ** -0.5
    h_ar = jnp.arange(N_HEADS)[:, None, None]

    def attend(step):
        b = step // n_qc
        start = (step % n_qc) * Q_CHUNK
        q_c = lax.dynamic_slice_in_dim(lax.dynamic_index_in_dim(q, b, 0, keepdims=False), start, Q_CHUNK, axis=1)
        kb = lax.dynamic_index_in_dim(k_blocks, b, 0, keepdims=False)
        vb = lax.dynamic_index_in_dim(v_blocks, b, 0, keepdims=False)
        idx = lax.dynamic_slice_in_dim(lax.dynamic_index_in_dim(sel_idx, b, 0, keepdims=False), start, Q_CHUNK, axis=1)
        t = start + jnp.arange(Q_CHUNK)
        j = start // MOBA_BLOCK
        k_own = lax.dynamic_index_in_dim(kb, j, 1, keepdims=False)
        v_own = lax.dynamic_index_in_dim(vb, j, 1, keepdims=False)
        dist_own = t[:, None] - (j * MOBA_BLOCK + jnp.arange(MOBA_BLOCK))[None, :]
        s_own = (jnp.einsum('hqd,hkd->hqk', q_c, k_own).astype(jnp.float32) * scale
                 - slopes[:, None, None] * dist_own.astype(jnp.float32))
        s_own = jnp.where(dist_own >= 0, s_own, NEG_INF)
        k_sel = kb[h_ar, idx]
        v_sel = vb[h_ar, idx]
        key_sel = idx[..., None] * MOBA_BLOCK + jnp.arange(MOBA_BLOCK)
        dist_sel = t[None, :, None, None] - key_sel
        s_sel = (jnp.einsum('hqd,hqnkd->hqnk', q_c, k_sel).astype(jnp.float32) * scale
                 - slopes[:, None, None, None] * dist_sel.astype(jnp.float32))
        valid = (jnp.arange(k_sel_n) < j)[None, None, :, None]
        s_sel = jnp.where(valid, s_sel, NEG_INF)
        logits = jnp.concatenate([s_sel.reshape(N_HEADS, Q_CHUNK, k_sel_n * MOBA_BLOCK), s_own], axis=-1)
        probs = jax.nn.softmax(logits, axis=-1).astype(v.dtype)
        p_sel = probs[..., :k_sel_n * MOBA_BLOCK].reshape(N_HEADS, Q_CHUNK, k_sel_n, MOBA_BLOCK)
        p_own = probs[..., k_sel_n * MOBA_BLOCK:]
        return (jnp.einsum('hqnk,hqnkd->hqd', p_sel, v_sel)
                + jnp.einsum('hqk,hkd->hqd', p_own, v_own))

    outs = lax.map(attend, jnp.arange(B * n_qc))
    o = outs.reshape(B, n_qc, N_HEADS, Q_CHUNK, HEAD_DIM).transpose(0, 1, 3, 2, 4)
    o = o.reshape(B, s_pad, D_MODEL)[:, :S]
    return o @ w_o


def hier_moe(xn, w_group, b_group, w_router, b_router, w_gate, w_up, w_down):
    B, S, D = xn.shape
    xf = xn.reshape(-1, D)
    g_prob = jax.nn.softmax((xf @ w_group + b_group).astype(jnp.float32), axis=-1)
    g_w, g_idx = lax.top_k(g_prob, 1)
    e_logits = jnp.einsum('td,gde->tge', xf, w_router) + b_router
    e_logits = jnp.take_along_axis(e_logits, g_idx[:, :, None], axis=1)[:, 0].astype(jnp.float32)
    e_prob = jax.nn.softmax(e_logits, axis=-1)
    e_w, e_idx = lax.top_k(e_prob, EXPERT_TOPK)
    e_w = e_w / jnp.sum(e_w, axis=-1, keepdims=True)
    weights = g_w * e_w
    expert_id = g_idx * EXPERTS_PER_GROUP + e_idx
    combine = jnp.sum(jax.nn.one_hot(expert_id, N_EXPERTS, dtype=jnp.float32) * weights[..., None],
                      axis=1).astype(xn.dtype)
    y = jnp.zeros_like(xf)
    for e in range(N_EXPERTS):
        hdn = jax.nn.silu(xf @ w_gate[e]) * (xf @ w_up[e])
        y = y + combine[:, e:e + 1] * (hdn @ w_down[e])
    return y.reshape(B, S, D)


def setup_inputs(seed: int = 0) -> dict:
    key = jax.random.key(seed)
    ks = jax.random.split(key, 26)

    def nrm(k, shape, scale):
        return jax.random.normal(k, shape, jnp.float32) * scale

    def gain(k, shape):
        return 1.0 + 0.05 * jax.random.normal(k, shape, jnp.float32)

    D = D_MODEL
    return {
        'x': nrm(ks[0], (BATCH, SEQ, D), 1.0),
        'p': nrm(ks[1], (DEPTH, BATCH, SEQ, PLE_DIM), 1.0),
        'g_mix': gain(ks[2], (DEPTH, D)),
        'g_ffn': gain(ks[3], (DEPTH, D)),
        'g_ple': gain(ks[4], (DEPTH, D)),
        'conv_w_pw1': nrm(ks[5], (N_CONV_LAYERS, D, 2 * D), D ** -0.5),
        'conv_b_pw1': nrm(ks[6], (N_CONV_LAYERS, 2 * D), 0.02),
        'conv_w_dw': nrm(ks[7], (N_CONV_LAYERS, CONV_WIDTH, D), CONV_WIDTH ** -0.5),
        'conv_b_dw': nrm(ks[8], (N_CONV_LAYERS, D), 0.02),
        'conv_ln_g': gain(ks[9], (N_CONV_LAYERS, D)),
        'conv_ln_b': nrm(ks[10], (N_CONV_LAYERS, D), 0.02),
        'conv_w_pw2': nrm(ks[11], (N_CONV_LAYERS, D, D), D ** -0.5),
        'conv_b_pw2': nrm(ks[12], (N_CONV_LAYERS, D), 0.02),
        'attn_w_qkv': nrm(ks[13], (N_ATTN_LAYERS, D, 3 * D), D ** -0.5),
        'attn_q_gain': gain(ks[14], (N_ATTN_LAYERS, HEAD_DIM)),
        'attn_k_gain': gain(ks[15], (N_ATTN_LAYERS, HEAD_DIM)),
        'attn_w_o': nrm(ks[16], (N_ATTN_LAYERS, D, D), D ** -0.5),
        'moe_w_group': nrm(ks[17], (DEPTH, D, N_GROUPS), D ** -0.5),
        'moe_b_group': nrm(ks[18], (DEPTH, N_GROUPS), 0.01),
        'moe_w_router': nrm(ks[19], (DEPTH, N_GROUPS, D, EXPERTS_PER_GROUP), D ** -0.5),
        'moe_b_router': nrm(ks[20], (DEPTH, N_GROUPS, EXPERTS_PER_GROUP), 0.01),
        'moe_w_gate': nrm(ks[21], (DEPTH, N_EXPERTS, D, EXPERT_HIDDEN), D ** -0.5),
        'moe_w_up': nrm(ks[22], (DEPTH, N_EXPERTS, D, EXPERT_HIDDEN), D ** -0.5),
        'moe_w_down': nrm(ks[23], (DEPTH, N_EXPERTS, EXPERT_HIDDEN, D), EXPERT_HIDDEN ** -0.5),
        'ple_w_gate': nrm(ks[24], (DEPTH, D, D), D ** -0.5),
        'ple_w_proj': nrm(ks[25], (DEPTH, PLE_DIM, D), PLE_DIM ** -0.5),
    }


def reference(x, p, g_mix, g_ffn, g_ple,
              conv_w_pw1, conv_b_pw1, conv_w_dw, conv_b_dw, conv_ln_g, conv_ln_b, conv_w_pw2, conv_b_pw2,
              attn_w_qkv, attn_q_gain, attn_k_gain, attn_w_o,
              moe_w_group, moe_b_group, moe_w_router, moe_b_router, moe_w_gate, moe_w_up, moe_w_down,
              ple_w_gate, ple_w_proj):
    h = x
    for i in range(DEPTH):
        xn = rms_norm(h, g_mix[i])
        if i % N_MIXERS == 0:
            c = i // N_MIXERS
            mix = conformer_conv(xn, conv_w_pw1[c], conv_b_pw1[c], conv_w_dw[c], conv_b_dw[c],
                                 conv_ln_g[c], conv_ln_b[c], conv_w_pw2[c], conv_b_pw2[c])
        else:
            a = i // N_MIXERS
            mix = moba_attention(xn, attn_w_qkv[a], attn_q_gain[a], attn_k_gain[a], attn_w_o[a])
        h = h + mix
        h = h + hier_moe(rms_norm(h, g_ffn[i]), moe_w_group[i], moe_b_group[i], moe_w_router[i],
                         moe_b_router[i], moe_w_gate[i], moe_w_up[i], moe_w_down[i])
        ple_gate = jax.nn.sigmoid(rms_norm(h, g_ple[i]) @ ple_w_gate[i])
        h = h + ple_gate * (p[i] @ ple_w_proj[i])
    return h
```

```python
import functools

import jax
import jax.numpy as jnp
from jax import lax
from jax.experimental import pallas as pl
from jax.experimental.pallas import tpu as pltpu

F32 = jnp.float32
BF16 = jnp.bfloat16

EPS = 1e-6
NEG_INF = -1e30

N_HEADS = 8
MOBA_BLOCK = 256
MOBA_TOPK = 3
N_GROUPS = 4
EXPERTS_PER_GROUP = 4
N_EXPERTS = N_GROUPS * EXPERTS_PER_GROUP

V7X_VMEM_BYTES = 64 * 1024 * 1024
LANES = 128
SUBLANES = 8
VMEM_LIMIT = V7X_VMEM_BYTES * 7 // 8

ROUTER_LANES = LANES
EXPERT_LANE0 = N_GROUPS

CONV_HALO = 32
CONV_ROWS = 32
CONV_COLS = 256


def _params(*semantics):
    return pltpu.CompilerParams(dimension_semantics=semantics, vmem_limit_bytes=VMEM_LIMIT)


def _resident(shape):
    nd = len(shape)
    return pl.BlockSpec(shape, lambda *_: (0,) * nd, pipeline_mode=pl.Buffered(1))


def _rms(x, g):
    return x * lax.rsqrt(jnp.mean(x * x, axis=-1, keepdims=True) + EPS) * g


def _sigmoid(x):
    return 1.0 / (1.0 + jnp.exp(-x))


def _conv_mixer_kernel(x_ref, g_ref, w1_ref, b1_ref, wdw_ref, bdw_ref, lng_ref, lnb_ref,
                       w2_ref, b2_ref, o_ref, ubuf, cbuf, *, ts, d, kw):
    s = pl.program_id(1)
    x = x_ref[0]
    xn = _rms(x, g_ref[...])
    u = jnp.dot(xn.astype(BF16), w1_ref[...], preferred_element_type=F32) + b1_ref[...]
    glu = u[:, :d] * _sigmoid(u[:, d:])

    @pl.when(s == 0)
    def _():
        ubuf[0:CONV_HALO, :] = jnp.zeros((CONV_HALO, d), F32)

    ubuf[CONV_HALO:CONV_HALO + ts, :] = glu

    first = CONV_HALO - (kw - 1)

    def chunk(i, carry):
        cols = pl.ds(pl.multiple_of(i * CONV_COLS, CONV_COLS), CONV_COLS)
        for r0 in range(0, ts, CONV_ROWS):
            acc = jnp.zeros((CONV_ROWS, CONV_COLS), F32)
            for k in range(kw):
                acc = acc + ubuf[r0 + first + k:r0 + first + k + CONV_ROWS, cols] * wdw_ref[k:k + 1, cols]
            cbuf[r0:r0 + CONV_ROWS, cols] = acc
        return carry

    lax.fori_loop(0, d // CONV_COLS, chunk, 0)
    ubuf[0:CONV_HALO, :] = ubuf[ts:ts + CONV_HALO, :]

    c = cbuf[...] + bdw_ref[...]
    mu = jnp.mean(c, axis=-1, keepdims=True)
    xc = c - mu
    y = xc * lax.rsqrt(jnp.mean(xc * xc, axis=-1, keepdims=True) + EPS) * lng_ref[...] + lnb_ref[...]
    y = y * _sigmoid(y)
    o_ref[0] = x + jnp.dot(y.astype(BF16), w2_ref[...], preferred_element_type=F32) + b2_ref[...]


def _conv_mixer(h, g, w1, b1, wdw, bdw, lng, lnb, w2, b2, *, ts=256):
    b, s, d = h.shape
    kw = wdw.shape[0]
    assert s % ts == 0 and ts >= CONV_HALO >= kw - 1 and d % CONV_COLS == 0
    row = lambda a: a.reshape(1, -1)
    kern = functools.partial(_conv_mixer_kernel, ts=ts, d=d, kw=kw)
    return pl.pallas_call(
        kern,
        out_shape=jax.ShapeDtypeStruct((b, s, d), F32),
        grid=(b, s // ts),
        in_specs=[
            pl.BlockSpec((1, ts, d), lambda i, j: (i, j, 0)),
            _resident((1, d)), _resident((d, 2 * d)), _resident((1, 2 * d)),
            _resident((kw, d)), _resident((1, d)), _resident((1, d)), _resident((1, d)),
            _resident((d, d)), _resident((1, d)),
        ],
        out_specs=pl.BlockSpec((1, ts, d), lambda i, j: (i, j, 0)),
        scratch_shapes=[pltpu.VMEM((ts + CONV_HALO, d), F32), pltpu.VMEM((ts, d), F32)],
        compiler_params=_params("arbitrary", "arbitrary"),
        name="conv_mixer",
    )(h, row(g), w1.astype(BF16), row(b1), wdw, row(bdw), row(lng), row(lnb), w2.astype(BF16), row(b2))


def _qkv_kernel(x_ref, g_ref, w_ref, qg_ref, kg_ref, q_ref, k_ref, v_ref, *, d, hd):
    xn = _rms(x_ref[...], g_ref[...]).astype(BF16)
    qkv = jnp.dot(xn, w_ref[...], preferred_element_type=F32)
    for h in range(d // hd):
        cols = slice(h * hd, (h + 1) * hd)
        q_ref[:, cols] = _rms(qkv[:, h * hd:(h + 1) * hd], qg_ref[...]).astype(BF16)
        k_ref[:, cols] = _rms(qkv[:, d + h * hd:d + (h + 1) * hd], kg_ref[...]).astype(BF16)
    v_ref[...] = qkv[:, 2 * d:].astype(BF16)


def _qkv(h2d, g, w, qg, kg, *, tm=512):
    t, d = h2d.shape
    hd = qg.shape[0]
    kern = functools.partial(_qkv_kernel, d=d, hd=hd)
    tile = pl.BlockSpec((tm, d), lambda i: (i, 0))
    return pl.pallas_call(
        kern,
        out_shape=[jax.ShapeDtypeStruct((t, d), BF16)] * 3,
        grid=(t // tm,),
        in_specs=[tile, _resident((1, d)), _resident((d, 3 * d)), _resident((1, hd)), _resident((1, hd))],
        out_specs=[tile, tile, tile],
        compiler_params=_params("arbitrary"),
        name="attn_qkv",
    )(h2d, g.reshape(1, d), w.astype(BF16), qg.reshape(1, hd), kg.reshape(1, hd))


def _moba_kernel(slope_ref, q_ref, k_ref, v_ref, o_ref, kmean, m_ref, l_ref, acc_ref, *, nb, blk, hd, topk):
    j = pl.program_id(2)
    scale = hd ** -0.5
    slope = slope_ref[0][:, :1]

    @pl.when(j == 0)
    def _():
        if kmean.shape[0] > nb:
            kmean[nb:, :] = jnp.zeros((kmean.shape[0] - nb, hd), F32)
        for n in range(nb):
            kmean[n:n + 1, :] = jnp.mean(k_ref[0, n * blk:(n + 1) * blk, :].astype(F32), axis=0, keepdims=True)

    q = q_ref[0]
    gate = lax.dot_general(kmean[...], q.astype(F32), (((1,), (1,)), ((), ())),
                           precision=lax.Precision.HIGHEST, preferred_element_type=F32)
    g = [jnp.where(m < j, gate[m:m + 1, :], -jnp.inf) for m in range(nb - 1)]
    rows = []
    for n in range(nb - 1):
        beaten = jnp.zeros((1, blk), F32)
        for m in range(nb - 1):
            if m == n:
                continue
            ahead = (g[m] > g[n]) | (g[m] == g[n]) if m < n else (g[m] > g[n])
            beaten = beaten + jnp.where(ahead, 1.0, 0.0)
        rows.append(jnp.where(n < j, jnp.where(beaten < topk, 1.0, 0.0), 0.0))
    sel_t = jnp.concatenate(rows + [jnp.zeros((LANES - (nb - 1), blk), F32)], axis=0)
    sel = sel_t.T

    r = lax.broadcasted_iota(jnp.int32, (blk, blk), 0)
    c = lax.broadcasted_iota(jnp.int32, (blk, blk), 1)
    rel = (r - c).astype(F32)

    def scores(k_blk, blocks_back):
        s = lax.dot_general(q, k_blk, (((1,), (1,)), ((), ())), preferred_element_type=F32)
        return s * scale - slope * (rel + blocks_back * float(blk))

    j0 = pl.multiple_of(j * blk, blk)
    s = jnp.where(rel >= 0, scores(k_ref[0, pl.ds(j0, blk), :], 0.0), NEG_INF)
    m = jnp.max(s, axis=-1, keepdims=True)
    p = jnp.exp(s - m)
    m_ref[...] = m
    l_ref[...] = jnp.sum(p, axis=-1, keepdims=True)
    acc_ref[...] = jnp.dot(p.astype(BF16), v_ref[0, pl.ds(j0, blk), :], preferred_element_type=F32)

    for n in range(nb - 1):
        @pl.when(n < j)
        def _(n=n):
            back = (j - n).astype(F32)
            s = jnp.where(sel[:, n:n + 1] > 0.5, scores(k_ref[0, n * blk:(n + 1) * blk, :], back), NEG_INF)
            m_old = m_ref[...]
            m_new = jnp.maximum(m_old, jnp.max(s, axis=-1, keepdims=True))
            alpha = jnp.exp(m_old - m_new)
            p = jnp.exp(s - m_new)
            m_ref[...] = m_new
            l_ref[...] = alpha * l_ref[...] + jnp.sum(p, axis=-1, keepdims=True)
            acc_ref[...] = alpha * acc_ref[...] + jnp.dot(
                p.astype(BF16), v_ref[0, n * blk:(n + 1) * blk, :], preferred_element_type=F32)

    o_ref[0] = (acc_ref[...] / l_ref[...]).astype(BF16)


def _moba(q, k, v, *, n_heads):
    b, s, d = q.shape
    hd = d // n_heads
    blk = MOBA_BLOCK
    assert s % blk == 0 and hd == LANES
    nb = s // blk
    topk = max(1, min(MOBA_TOPK, nb - 1))
    slopes = jnp.exp2(-8.0 * jnp.arange(1, n_heads + 1, dtype=F32) / n_heads)
    slopes = jnp.broadcast_to(slopes[:, None, None], (n_heads, 1, LANES))
    kern = functools.partial(_moba_kernel, nb=nb, blk=blk, hd=hd, topk=topk)
    return pl.pallas_call(
        kern,
        out_shape=jax.ShapeDtypeStruct((b, s, d), BF16),
        grid=(b, n_heads, nb),
        in_specs=[
            pl.BlockSpec((1, 1, LANES), lambda i, h, j: (h, 0, 0)),
            pl.BlockSpec((1, blk, hd), lambda i, h, j: (i, j, h)),
            pl.BlockSpec((1, s, hd), lambda i, h, j: (i, 0, h)),
            pl.BlockSpec((1, s, hd), lambda i, h, j: (i, 0, h)),
        ],
        out_specs=pl.BlockSpec((1, blk, hd), lambda i, h, j: (i, j, h)),
        scratch_shapes=[pltpu.VMEM((pl.cdiv(nb, SUBLANES) * SUBLANES, hd), F32), pltpu.VMEM((blk, 1), F32),
                        pltpu.VMEM((blk, 1), F32), pltpu.VMEM((blk, hd), F32)],
        compiler_params=_params("arbitrary", "arbitrary", "arbitrary"),
        name="moba_attn",
    )(slopes, q, k, v)


def _matmul_residual_kernel(h_ref, a_ref, w_ref, o_ref):
    o_ref[...] = h_ref[...] + jnp.dot(a_ref[...], w_ref[...], preferred_element_type=F32)


def _matmul_residual(h2d, a, w, *, tm=512):
    t, d = h2d.shape
    tile = pl.BlockSpec((tm, d), lambda i: (i, 0))
    return pl.pallas_call(
        _matmul_residual_kernel,
        out_shape=jax.ShapeDtypeStruct((t, d), F32),
        grid=(t // tm,),
        in_specs=[tile, pl.BlockSpec((tm, a.shape[1]), lambda i: (i, 0)), _resident(w.shape)],
        out_specs=tile,
        compiler_params=_params("arbitrary"),
        name="attn_out_proj",
    )(h2d, a, w.astype(BF16))


def _router_kernel(h_ref, g_ref, wr_ref, br_ref, xn_ref, comb_ref):
    xn = _rms(h_ref[...], g_ref[...])
    xn_ref[...] = xn.astype(BF16)
    logits = jnp.dot(xn, wr_ref[...], precision=lax.Precision.HIGHEST, preferred_element_type=F32) + br_ref[...]
    lane = lax.broadcasted_iota(jnp.int32, logits.shape, 1).astype(F32)

    def first_max(vals):
        mx = jnp.max(vals, axis=-1, keepdims=True)
        return mx, jnp.min(jnp.where(vals == mx, lane, float(ROUTER_LANES)), axis=-1, keepdims=True)

    is_g = lane < N_GROUPS
    gmax, gidx = first_max(jnp.where(is_g, logits, -jnp.inf))
    zg = jnp.sum(jnp.where(is_g, jnp.exp(logits - gmax), 0.0), axis=-1, keepdims=True)
    g_w = 1.0 / zg
    lo = EXPERT_LANE0 + gidx * EXPERTS_PER_GROUP
    in_grp = (lane >= lo) & (lane < lo + EXPERTS_PER_GROUP)
    el = jnp.where(in_grp, logits, -jnp.inf)
    m1, i1 = first_max(el)
    m2, i2 = first_max(jnp.where(lane == i1, -jnp.inf, el))
    p2 = jnp.exp(m2 - m1)
    w1 = g_w / (1.0 + p2)
    w2 = g_w * p2 / (1.0 + p2)
    comb_ref[...] = jnp.where(lane == i1, w1, 0.0) + jnp.where(lane == i2, w2, 0.0)


def _router(h2d, g, w_group, b_group, w_router, b_router, *, tm=512):
    t, d = h2d.shape
    ng, _, ne = w_router.shape
    wr = jnp.concatenate([w_group, jnp.transpose(w_router, (1, 0, 2)).reshape(d, ng * ne)], axis=1)
    br = jnp.concatenate([b_group, b_router.reshape(ng * ne)])
    pad = ROUTER_LANES - wr.shape[1]
    wr = jnp.pad(wr, ((0, 0), (0, pad)))
    br = jnp.pad(br, (0, pad)).reshape(1, ROUTER_LANES)
    tile = pl.BlockSpec((tm, d), lambda i: (i, 0))
    ctile = pl.BlockSpec((tm, ROUTER_LANES), lambda i: (i, 0))
    return pl.pallas_call(
        _router_kernel,
        out_shape=[jax.ShapeDtypeStruct((t, d), BF16), jax.ShapeDtypeStruct((t, ROUTER_LANES), F32)],
        grid=(t // tm,),
        in_specs=[tile, _resident((1, d)), _resident((d, ROUTER_LANES)), _resident((1, ROUTER_LANES))],
        out_specs=[tile, ctile],
        compiler_params=_params("arbitrary"),
        name="moe_router",
    )(h2d, g.reshape(1, d), wr, br)


def _moe_dense_kernel(h_ref, x_ref, comb_ref, wg_ref, wu_ref, wd_ref, o_ref, acc_ref):
    e = pl.program_id(1)

    @pl.when(e == 0)
    def _():
        acc_ref[...] = h_ref[...]

    x = x_ref[...]
    comb = comb_ref[...]
    lane = lax.broadcasted_iota(jnp.int32, comb.shape, 1)
    cw = jnp.sum(jnp.where(lane == e + EXPERT_LANE0, comb, 0.0), axis=-1, keepdims=True)
    gate = jnp.dot(x, wg_ref[0], preferred_element_type=F32)
    up = jnp.dot(x, wu_ref[0], preferred_element_type=F32)
    hdn = gate * _sigmoid(gate) * up * cw
    acc_ref[...] += jnp.dot(hdn.astype(BF16), wd_ref[0], preferred_element_type=F32)

    @pl.when(e == pl.num_programs(1) - 1)
    def _():
        o_ref[...] = acc_ref[...]


def _moe_dense(h2d, xn, comb, w_gate, w_up, w_down, *, tm=1024):
    t, d = h2d.shape
    ne, _, hid = w_gate.shape
    tile = pl.BlockSpec((tm, d), lambda i, e: (i, 0))
    return pl.pallas_call(
        _moe_dense_kernel,
        out_shape=jax.ShapeDtypeStruct((t, d), F32),
        grid=(t // tm, ne),
        in_specs=[
            tile, tile, pl.BlockSpec((tm, ROUTER_LANES), lambda i, e: (i, 0)),
            pl.BlockSpec((1, d, hid), lambda i, e: (e, 0, 0)),
            pl.BlockSpec((1, d, hid), lambda i, e: (e, 0, 0)),
            pl.BlockSpec((1, hid, d), lambda i, e: (e, 0, 0)),
        ],
        out_specs=tile,
        scratch_shapes=[pltpu.VMEM((tm, d), F32)],
        compiler_params=_params("arbitrary", "arbitrary"),
        name="moe_experts",
    )(h2d, xn, comb, w_gate.astype(BF16), w_up.astype(BF16), w_down.astype(BF16))


def _ple_kernel(h_ref, p_ref, g_ref, wg_ref, wp_ref, o_ref):
    h = h_ref[...]
    gate = _sigmoid(jnp.dot(_rms(h, g_ref[...]).astype(BF16), wg_ref[...], preferred_element_type=F32))
    proj = jnp.dot(p_ref[...].astype(BF16), wp_ref[...], preferred_element_type=F32)
    o_ref[...] = h + gate * proj


def _ple(h2d, p2d, g, w_gate, w_proj, *, tm=512):
    t, d = h2d.shape
    pd = p2d.shape[1]
    tile = pl.BlockSpec((tm, d), lambda i: (i, 0))
    return pl.pallas_call(
        _ple_kernel,
        out_shape=jax.ShapeDtypeStruct((t, d), F32),
        grid=(t // tm,),
        in_specs=[tile, pl.BlockSpec((tm, pd), lambda i: (i, 0)), _resident((1, d)),
                  _resident((d, d)), _resident((pd, d))],
        out_specs=tile,
        compiler_params=_params("arbitrary"),
        name="ple",
    )(h2d, p2d, g.reshape(1, d), w_gate.astype(BF16), w_proj.astype(BF16))


def kernel(x, p, g_mix, g_ffn, g_ple, conv_w_pw1, conv_b_pw1, conv_w_dw, conv_b_dw, conv_ln_g, conv_ln_b, conv_w_pw2, conv_b_pw2, attn_w_qkv, attn_q_gain, attn_k_gain, attn_w_o, moe_w_group, moe_b_group, moe_w_router, moe_b_router, moe_w_gate, moe_w_up, moe_w_down, ple_w_gate, ple_w_proj):
    b, s, d = x.shape
    depth = g_mix.shape[0]
    t = b * s
    h = x
    for i in range(depth):
        if i % 2 == 0:
            c = i // 2
            h = _conv_mixer(h.reshape(b, s, d), g_mix[i], conv_w_pw1[c], conv_b_pw1[c], conv_w_dw[c],
                            conv_b_dw[c], conv_ln_g[c], conv_ln_b[c], conv_w_pw2[c], conv_b_pw2[c])
            h = h.reshape(t, d)
        else:
            a = i // 2
            h = h.reshape(t, d)
            q, k, v = _qkv(h, g_mix[i], attn_w_qkv[a], attn_q_gain[a], attn_k_gain[a])
            o = _moba(q.reshape(b, s, d), k.reshape(b, s, d), v.reshape(b, s, d), n_heads=N_HEADS)
            h = _matmul_residual(h, o.reshape(t, d), attn_w_o[a])
        xn, comb = _router(h, g_ffn[i], moe_w_group[i], moe_b_group[i], moe_w_router[i], moe_b_router[i])
        h = _moe_dense(h, xn, comb, moe_w_gate[i], moe_w_up[i], moe_w_down[i])
        h = _ple(h, p[i].reshape(t, -1), g_ple[i], ple_w_gate[i], ple_w_proj[i])
    return h.reshape(b, s, d)
```

```python
import functools

import jax
import jax.numpy as jnp
import numpy as np
from jax import lax
from jax.experimental import pallas as pl
from jax.experimental.pallas import tpu as pltpu

F32 = jnp.float32
BF16 = jnp.bfloat16

EPS = 1e-6
NEG_INF = -1e30

N_HEADS = 8
MOBA_BLOCK = 256
MOBA_TOPK = 3
N_GROUPS = 4
EXPERTS_PER_GROUP = 4
N_EXPERTS = N_GROUPS * EXPERTS_PER_GROUP

V7X_VMEM_BYTES = 64 * 1024 * 1024
LANES = 128
SUBLANES = 8
VMEM_LIMIT = V7X_VMEM_BYTES * 7 // 8

ROUTER_LANES = LANES
EXPERT_LANE0 = N_GROUPS

CONV_HALO = 32
CONV_ROWS = 32
CONV_COLS = 256


def _params(*semantics):
    return pltpu.CompilerParams(dimension_semantics=semantics, vmem_limit_bytes=VMEM_LIMIT)


def _resident(shape):
    nd = len(shape)
    return pl.BlockSpec(shape, lambda *_: (0,) * nd, pipeline_mode=pl.Buffered(1))


def _rms(x, g):
    return x * lax.rsqrt(jnp.mean(x * x, axis=-1, keepdims=True) + EPS) * g


def _sigmoid(x):
    return 1.0 / (1.0 + jnp.exp(-x))


def _conv_mixer_kernel(x_ref, g_ref, w1_ref, b1_ref, wdw_ref, bdw_ref, lng_ref, lnb_ref,
                       w2_ref, b2_ref, o_ref, ubuf, cbuf, *, ts, d, kw):
    s = pl.program_id(1)
    x = x_ref[0]
    xn = _rms(x, g_ref[...])
    u = jnp.dot(xn.astype(BF16), w1_ref[...], preferred_element_type=F32) + b1_ref[...]
    glu = u[:, :d] * _sigmoid(u[:, d:])

    @pl.when(s == 0)
    def _():
        ubuf[0:CONV_HALO, :] = jnp.zeros((CONV_HALO, d), F32)

    ubuf[CONV_HALO:CONV_HALO + ts, :] = glu

    first = CONV_HALO - (kw - 1)

    def chunk(i, carry):
        cols = pl.ds(pl.multiple_of(i * CONV_COLS, CONV_COLS), CONV_COLS)
        for r0 in range(0, ts, CONV_ROWS):
            acc = jnp.zeros((CONV_ROWS, CONV_COLS), F32)
            for k in range(kw):
                acc = acc + ubuf[r0 + first + k:r0 + first + k + CONV_ROWS, cols] * wdw_ref[k:k + 1, cols]
            cbuf[r0:r0 + CONV_ROWS, cols] = acc
        return carry

    lax.fori_loop(0, d // CONV_COLS, chunk, 0)
    ubuf[0:CONV_HALO, :] = ubuf[ts:ts + CONV_HALO, :]

    c = cbuf[...] + bdw_ref[...]
    mu = jnp.mean(c, axis=-1, keepdims=True)
    xc = c - mu
    y = xc * lax.rsqrt(jnp.mean(xc * xc, axis=-1, keepdims=True) + EPS) * lng_ref[...] + lnb_ref[...]
    y = y * _sigmoid(y)
    o_ref[0] = x + jnp.dot(y.astype(BF16), w2_ref[...], preferred_element_type=F32) + b2_ref[...]


def _conv_mixer(h, g, w1, b1, wdw, bdw, lng, lnb, w2, b2, *, ts=256):
    b, s, d = h.shape
    kw = wdw.shape[0]
    assert s % ts == 0 and ts >= CONV_HALO >= kw - 1 and d % CONV_COLS == 0
    row = lambda a: a.reshape(1, -1)
    kern = functools.partial(_conv_mixer_kernel, ts=ts, d=d, kw=kw)
    return pl.pallas_call(
        kern,
        out_shape=jax.ShapeDtypeStruct((b, s, d), F32),
        grid=(b, s // ts),
        in_specs=[
            pl.BlockSpec((1, ts, d), lambda i, j: (i, j, 0)),
            _resident((1, d)), _resident((d, 2 * d)), _resident((1, 2 * d)),
            _resident((kw, d)), _resident((1, d)), _resident((1, d)), _resident((1, d)),
            _resident((d, d)), _resident((1, d)),
        ],
        out_specs=pl.BlockSpec((1, ts, d), lambda i, j: (i, j, 0)),
        scratch_shapes=[pltpu.VMEM((ts + CONV_HALO, d), F32), pltpu.VMEM((ts, d), F32)],
        compiler_params=_params("arbitrary", "arbitrary"),
        name="conv_mixer",
    )(h, row(g), w1.astype(BF16), row(b1), wdw, row(bdw), row(lng), row(lnb), w2.astype(BF16), row(b2))


def _qkv_kernel(x_ref, g_ref, w_ref, qg_ref, kg_ref, q_ref, k_ref, v_ref, *, d, hd):
    xn = _rms(x_ref[...], g_ref[...]).astype(BF16)
    qkv = jnp.dot(xn, w_ref[...], preferred_element_type=F32)
    for h in range(d // hd):
        cols = slice(h * hd, (h + 1) * hd)
        q_ref[:, cols] = _rms(qkv[:, h * hd:(h + 1) * hd], qg_ref[...]).astype(BF16)
        k_ref[:, cols] = _rms(qkv[:, d + h * hd:d + (h + 1) * hd], kg_ref[...]).astype(BF16)
    v_ref[...] = qkv[:, 2 * d:].astype(BF16)


def _qkv(h2d, g, w, qg, kg, *, tm=512):
    t, d = h2d.shape
    hd = qg.shape[0]
    kern = functools.partial(_qkv_kernel, d=d, hd=hd)
    tile = pl.BlockSpec((tm, d), lambda i: (i, 0))
    return pl.pallas_call(
        kern,
        out_shape=[jax.ShapeDtypeStruct((t, d), BF16)] * 3,
        grid=(t // tm,),
        in_specs=[tile, _resident((1, d)), _resident((d, 3 * d)), _resident((1, hd)), _resident((1, hd))],
        out_specs=[tile, tile, tile],
        compiler_params=_params("arbitrary"),
        name="attn_qkv",
    )(h2d, g.reshape(1, d), w.astype(BF16), qg.reshape(1, hd), kg.reshape(1, hd))


def _moba_kernel(slope_ref, q_ref, k_ref, v_ref, kaux_ref, vaux_ref, o_ref, kaug, vaug, zbuf, pbuf,
                 *, nb, blk, hd, topk):
    scale = hd ** -0.5
    negz = NEG_INF / scale
    slope = slope_ref[0][:, :1]
    kaug[:, :hd] = k_ref[0]
    kaug[:, hd:] = kaux_ref[...]
    vaug[:, :hd] = v_ref[0]
    vaug[:, hd:] = vaux_ref[...]

    km = jnp.concatenate([jnp.mean(k_ref[0, n * blk:(n + 1) * blk, :].astype(F32), axis=0, keepdims=True)
                          for n in range(nb)], axis=0)
    hi = km.astype(BF16).astype(F32)
    mid = (km - hi).astype(BF16).astype(F32)
    lo = (km - hi - mid).astype(BF16).astype(F32)
    pad = [jnp.zeros((-3 * nb % (2 * SUBLANES), hd), F32)] if 3 * nb % (2 * SUBLANES) else []
    km3 = jnp.concatenate([hi, mid, lo] + pad, axis=0).astype(BF16)

    r = lax.broadcasted_iota(jnp.int32, (blk, blk), 0)
    c = lax.broadcasted_iota(jnp.int32, (blk, blk), 1)
    causal = r >= c
    keypos = lax.broadcasted_iota(jnp.int32, (1, blk), 1).astype(F32)
    nt = (((1,), (1,)), ((), ()))

    for j in range(nb):
        q = q_ref[0, j * blk:(j + 1) * blk, :]
        select = j > topk
        if select:
            g3 = lax.dot_general(km3, q, nt, preferred_element_type=F32)
            gate = g3[0:nb] + g3[nb:2 * nb] + g3[2 * nb:3 * nb]
            g = [gate[m:m + 1, :] for m in range(j)]
            rows = []
            for n in range(j):
                beaten = jnp.zeros((1, blk), F32)
                for m in range(j):
                    if m != n:
                        ahead = (g[m] > g[n]) | (g[m] == g[n]) if m < n else (g[m] > g[n])
                        beaten = beaten + jnp.where(ahead, 1.0, 0.0)
                rows.append(jnp.where(beaten < topk, 0.0, negz))
            bias_t = jnp.concatenate(rows + [jnp.zeros((LANES - j, blk), F32)], axis=0)
            q = jnp.concatenate([q, bias_t.T.astype(BF16)], axis=1)

        mrun = None
        for n in range(j + 1):
            keys = kaug[n * blk:(n + 1) * blk, :] if select else k_ref[0, n * blk:(n + 1) * blk, :]
            z = lax.dot_general(q, keys, nt, preferred_element_type=F32)
            u = z * scale + slope * (keypos + float(n * blk))
            if n == j:
                u = jnp.where(causal, u, NEG_INF)
            zbuf[:, n * blk:(n + 1) * blk] = u
            for half in range(blk // LANES):
                f = u[:, half * LANES:(half + 1) * LANES]
                mrun = f if mrun is None else jnp.maximum(mrun, f)
        mb = jnp.broadcast_to(jnp.max(mrun, axis=-1, keepdims=True), (blk, LANES))
        for n in range(j + 1):
            for half in range(blk // LANES):
                cols = slice(n * blk + half * LANES, n * blk + (half + 1) * LANES)
                pbuf[:, cols] = jnp.exp(zbuf[:, cols] - mb).astype(BF16)
        acc = jnp.dot(pbuf[:, :(j + 1) * blk], vaug[:(j + 1) * blk, :], preferred_element_type=F32)
        o_ref[0, j * blk:(j + 1) * blk, :] = (acc[:, :hd] / acc[:, hd:hd + 1]).astype(BF16)


def _moba(q, k, v, *, n_heads):
    b, s, d = q.shape
    hd = d // n_heads
    blk = MOBA_BLOCK
    assert s % blk == 0 and hd == LANES and blk % LANES == 0
    nb = s // blk
    assert nb <= LANES
    topk = max(1, min(MOBA_TOPK, nb - 1))
    slopes = jnp.exp2(-8.0 * jnp.arange(1, n_heads + 1, dtype=F32) / n_heads)
    slopes = jnp.broadcast_to(slopes[:, None, None], (n_heads, 1, LANES))
    kaux = np.zeros((s, LANES), np.float32)
    kaux[np.arange(s), np.arange(s) // blk] = 1.0
    vaux = np.zeros((s, LANES), np.float32)
    vaux[:, 0] = 1.0
    kern = functools.partial(_moba_kernel, nb=nb, blk=blk, hd=hd, topk=topk)
    head = pl.BlockSpec((1, s, hd), lambda i, h: (i, 0, h))
    return pl.pallas_call(
        kern,
        out_shape=jax.ShapeDtypeStruct((b, s, d), BF16),
        grid=(b, n_heads),
        in_specs=[pl.BlockSpec((1, 1, LANES), lambda i, h: (h, 0, 0)), head, head, head,
                  _resident((s, LANES)), _resident((s, LANES))],
        out_specs=head,
        scratch_shapes=[pltpu.VMEM((s, hd + LANES), BF16), pltpu.VMEM((s, hd + LANES), BF16),
                        pltpu.VMEM((blk, s), F32), pltpu.VMEM((blk, s), BF16)],
        compiler_params=_params("arbitrary", "arbitrary"),
        name="moba_attn",
    )(slopes, q, k, v, jnp.asarray(kaux, BF16), jnp.asarray(vaux, BF16))


def _matmul_residual_kernel(h_ref, a_ref, w_ref, o_ref):
    o_ref[...] = h_ref[...] + jnp.dot(a_ref[...], w_ref[...], preferred_element_type=F32)


def _matmul_residual(h2d, a, w, *, tm=512):
    t, d = h2d.shape
    tile = pl.BlockSpec((tm, d), lambda i: (i, 0))
    return pl.pallas_call(
        _matmul_residual_kernel,
        out_shape=jax.ShapeDtypeStruct((t, d), F32),
        grid=(t // tm,),
        in_specs=[tile, pl.BlockSpec((tm, a.shape[1]), lambda i: (i, 0)), _resident(w.shape)],
        out_specs=tile,
        compiler_params=_params("arbitrary"),
        name="attn_out_proj",
    )(h2d, a, w.astype(BF16))


PAIRS = [(a, b) for a in range(EXPERTS_PER_GROUP) for b in range(a + 1, EXPERTS_PER_GROUP)]
N_CLASSES = N_GROUPS * len(PAIRS)
META_CLASS, META_RANK, META_WA, META_WB = 0, 1, 2, 3
EXPERT_TILE = 256


def _router_kernel(h_ref, g_ref, wr_ref, br_ref, tri_ref, hx_ref, cnt_ref, running, *, d):
    @pl.when(pl.program_id(0) == 0)
    def _():
        running[...] = jnp.zeros_like(running)

    xn = _rms(h_ref[...], g_ref[...])
    logits = jnp.dot(xn, wr_ref[...], precision=lax.Precision.HIGHEST, preferred_element_type=F32) + br_ref[...]
    lane = lax.broadcasted_iota(jnp.int32, logits.shape, 1).astype(F32)

    def first_max(vals):
        mx = jnp.max(vals, axis=-1, keepdims=True)
        return mx, jnp.min(jnp.where(vals == mx, lane, float(ROUTER_LANES)), axis=-1, keepdims=True)

    is_g = lane < N_GROUPS
    gmax, gidx = first_max(jnp.where(is_g, logits, -jnp.inf))
    zg = jnp.sum(jnp.where(is_g, jnp.exp(logits - gmax), 0.0), axis=-1, keepdims=True)
    g_w = 1.0 / zg
    lo = EXPERT_LANE0 + gidx * EXPERTS_PER_GROUP
    in_grp = (lane >= lo) & (lane < lo + EXPERTS_PER_GROUP)
    el = jnp.where(in_grp, logits, -jnp.inf)
    m1, i1 = first_max(el)
    m2, i2 = first_max(jnp.where(lane == i1, -jnp.inf, el))
    p2 = jnp.exp(m2 - m1)
    w1 = g_w / (1.0 + p2)
    w2 = g_w * p2 / (1.0 + p2)
    e1, e2 = i1 - lo, i2 - lo
    ea, eb = jnp.minimum(e1, e2), jnp.maximum(e1, e2)
    wa, wb = jnp.where(e1 < e2, w1, w2), jnp.where(e1 < e2, w2, w1)
    pair = ea * (2 * EXPERTS_PER_GROUP - 1 - ea) * 0.5 + (eb - ea - 1.0)
    cls = gidx * float(len(PAIRS)) + pair
    onehot = jnp.where(lane == cls, 1.0, 0.0)
    before = jnp.dot(tri_ref[...], onehot.astype(BF16), preferred_element_type=F32) + running[...]
    rank = jnp.sum(onehot * before, axis=-1, keepdims=True)
    running[...] += jnp.sum(onehot, axis=0, keepdims=True)
    cnt_ref[...] = running[...]

    meta = (jnp.where(lane == META_CLASS, cls, 0.0) + jnp.where(lane == META_RANK, rank, 0.0)
            + jnp.where(lane == META_WA, wa, 0.0) + jnp.where(lane == META_WB, wb, 0.0))
    hx_ref[:, :d] = xn
    hx_ref[:, d:] = meta


def _router(h2d, g, w_group, b_group, w_router, b_router, *, tm=512):
    t, d = h2d.shape
    ng, _, ne = w_router.shape
    assert (ng, ne) == (N_GROUPS, EXPERTS_PER_GROUP) and N_CLASSES <= ROUTER_LANES
    wr = jnp.concatenate([w_group, jnp.transpose(w_router, (1, 0, 2)).reshape(d, ng * ne)], axis=1)
    br = jnp.concatenate([b_group, b_router.reshape(ng * ne)])
    pad = ROUTER_LANES - wr.shape[1]
    wr = jnp.pad(wr, ((0, 0), (0, pad)))
    br = jnp.pad(br, (0, pad)).reshape(1, ROUTER_LANES)
    tri = jnp.asarray(np.tril(np.ones((tm, tm), np.float32), -1), BF16)
    return pl.pallas_call(
        functools.partial(_router_kernel, d=d),
        out_shape=[jax.ShapeDtypeStruct((t, d + ROUTER_LANES), F32), jax.ShapeDtypeStruct((1, ROUTER_LANES), F32)],
        grid=(t // tm,),
        in_specs=[pl.BlockSpec((tm, d), lambda i: (i, 0)), _resident((1, d)), _resident((d, ROUTER_LANES)),
                  _resident((1, ROUTER_LANES)), _resident((tm, tm))],
        out_specs=[pl.BlockSpec((tm, d + ROUTER_LANES), lambda i: (i, 0)),
                   pl.BlockSpec((1, ROUTER_LANES), lambda i: (0, 0))],
        scratch_shapes=[pltpu.VMEM((1, ROUTER_LANES), F32)],
        compiler_params=_params("arbitrary"),
        name="moe_router",
    )(h2d, g.reshape(1, d), wr, br, tri)


def _sorted_layout(hx, counts, d, n_tiles):
    cls = hx[:, d + META_CLASS].astype(jnp.int32)
    rank = hx[:, d + META_RANK].astype(jnp.int32)
    cnt = counts[0, :N_CLASSES].astype(jnp.int32)
    tiles = (cnt + EXPERT_TILE - 1) // EXPERT_TILE
    tile_end = jnp.cumsum(tiles)
    pos = ((tile_end - tiles) * EXPERT_TILE)[cls] + rank
    tile_cls = jnp.searchsorted(tile_end, jnp.arange(n_tiles, dtype=jnp.int32), side="right").astype(jnp.int32)
    valid = (tile_cls < N_CLASSES).astype(jnp.int32)
    last = jnp.max(jnp.where(cnt > 0, jnp.arange(N_CLASSES, dtype=jnp.int32), 0))
    tile_cls = jnp.where(valid == 1, tile_cls, last)
    grp, pair = tile_cls // len(PAIRS), tile_cls % len(PAIRS)
    pa = jnp.asarray(np.array([p[0] for p in PAIRS], np.int32))
    pb = jnp.asarray(np.array([p[1] for p in PAIRS], np.int32))
    return pos, grp * EXPERTS_PER_GROUP + pa[pair], grp * EXPERTS_PER_GROUP + pb[pair], valid


DISPATCH_CHUNK = 512


def _row_copy(src, dst, i, j, sem):
    return pltpu.make_async_copy(src.at[pl.ds(i, 1)], dst.at[pl.ds(j, 1)], sem)


def _dispatch_kernel(pos_ref, hx_ref, init_ref, hxs_ref, sems):
    del init_ref
    i = pl.program_id(0)
    n = pl.num_programs(0)
    base = i * DISPATCH_CHUNK
    slot = i % 2

    def issue(r, carry):
        _row_copy(hx_ref, hxs_ref, base + r, pos_ref[base + r], sems.at[slot]).start()
        return carry

    lax.fori_loop(0, DISPATCH_CHUNK, issue, 0)

    def wait_chunk(s):
        pltpu.make_async_copy(hx_ref.at[pl.ds(0, DISPATCH_CHUNK)], hxs_ref.at[pl.ds(0, DISPATCH_CHUNK)],
                              sems.at[s]).wait()

    @pl.when(i > 0)
    def _():
        wait_chunk(1 - slot)

    @pl.when(i == n - 1)
    def _():
        wait_chunk(slot)


def _dispatch(hx, pos, n_rows):
    t, w = hx.shape
    assert t % DISPATCH_CHUNK == 0
    return pl.pallas_call(
        _dispatch_kernel,
        out_shape=jax.ShapeDtypeStruct((n_rows, w), F32),
        grid_spec=pltpu.PrefetchScalarGridSpec(
            num_scalar_prefetch=1,
            grid=(t // DISPATCH_CHUNK,),
            in_specs=[pl.BlockSpec(memory_space=pl.ANY), pl.BlockSpec(memory_space=pl.ANY)],
            out_specs=pl.BlockSpec(memory_space=pl.ANY),
            scratch_shapes=[pltpu.SemaphoreType.DMA((2,))],
        ),
        input_output_aliases={2: 0},
        compiler_params=_params("arbitrary"),
        name="moe_dispatch",
    )(pos, hx, jnp.zeros((n_rows, w), F32))


def _experts_kernel(ea_ref, eb_ref, valid_ref, x_ref, wga_ref, wua_ref, wda_ref, wgb_ref, wub_ref, wdb_ref,
                    y_ref, *, d):
    del ea_ref, eb_ref
    i = pl.program_id(0)

    @pl.when(valid_ref[i] == 1)
    def _():
        x = x_ref[:, :d].astype(BF16)

        def hidden(wg_ref, wu_ref, lane):
            gate = jnp.dot(x, wg_ref[0], preferred_element_type=F32)
            up = jnp.dot(x, wu_ref[0], preferred_element_type=F32)
            return (gate * _sigmoid(gate) * up * x_ref[:, d + lane:d + lane + 1]).astype(BF16)

        y_ref[...] = (jnp.dot(hidden(wga_ref, wua_ref, META_WA), wda_ref[0], preferred_element_type=F32)
                      + jnp.dot(hidden(wgb_ref, wub_ref, META_WB), wdb_ref[0], preferred_element_type=F32))

    @pl.when(valid_ref[i] == 0)
    def _():
        y_ref[...] = jnp.zeros_like(y_ref)


def _experts(hxs, tile_ea, tile_eb, tile_valid, w_gate, w_up, w_down, *, d):
    n_rows, w = hxs.shape
    _, _, hid = w_gate.shape
    n_tiles = n_rows // EXPERT_TILE
    wg, wu, wd = w_gate.astype(BF16), w_up.astype(BF16), w_down.astype(BF16)
    up_a = pl.BlockSpec((1, d, hid), lambda i, ea, eb, v: (ea[i], 0, 0))
    up_b = pl.BlockSpec((1, d, hid), lambda i, ea, eb, v: (eb[i], 0, 0))
    down_a = pl.BlockSpec((1, hid, d), lambda i, ea, eb, v: (ea[i], 0, 0))
    down_b = pl.BlockSpec((1, hid, d), lambda i, ea, eb, v: (eb[i], 0, 0))
    return pl.pallas_call(
        functools.partial(_experts_kernel, d=d),
        out_shape=jax.ShapeDtypeStruct((n_rows, d), F32),
        grid_spec=pltpu.PrefetchScalarGridSpec(
            num_scalar_prefetch=3,
            grid=(n_tiles,),
            in_specs=[pl.BlockSpec((EXPERT_TILE, w), lambda i, ea, eb, v: (i, 0)),
                      up_a, up_a, down_a, up_b, up_b, down_b],
            out_specs=pl.BlockSpec((EXPERT_TILE, d), lambda i, ea, eb, v: (i, 0)),
        ),
        compiler_params=_params("arbitrary"),
        name="moe_experts",
    )(tile_ea, tile_eb, tile_valid, hxs, wg, wu, wd, wg, wu, wd)


def _combine_ple_kernel(pos_ref, h_ref, p_ref, g_ref, wg_ref, wp_ref, ys_ref, o_ref, ybuf, sems, *, tm):
    i = pl.program_id(0)
    n = pl.num_programs(0)
    slot = i % 2

    def gather(tile, s):
        def issue(r, carry):
            _row_copy(ys_ref, ybuf.at[s], pos_ref[tile * tm + r], r, sems.at[s]).start()
            return carry
        lax.fori_loop(0, tm, issue, 0)

    @pl.when(i == 0)
    def _():
        gather(0, 0)

    @pl.when(i + 1 < n)
    def _():
        gather(i + 1, 1 - slot)

    pltpu.make_async_copy(ys_ref.at[pl.ds(0, tm)], ybuf.at[slot], sems.at[slot]).wait()
    h = h_ref[...] + ybuf[slot]
    gate = _sigmoid(jnp.dot(_rms(h, g_ref[...]).astype(BF16), wg_ref[...], preferred_element_type=F32))
    proj = jnp.dot(p_ref[...].astype(BF16), wp_ref[...], preferred_element_type=F32)
    o_ref[...] = h + gate * proj


def _combine_ple(h2d, ys, pos, p2d, g, w_gate, w_proj, *, tm=512):
    t, d = h2d.shape
    pd = p2d.shape[1]
    tile = pl.BlockSpec((tm, d), lambda i, pos: (i, 0))
    res = lambda shape: pl.BlockSpec(shape, lambda i, pos: (0,) * len(shape), pipeline_mode=pl.Buffered(1))
    return pl.pallas_call(
        functools.partial(_combine_ple_kernel, tm=tm),
        out_shape=jax.ShapeDtypeStruct((t, d), F32),
        grid_spec=pltpu.PrefetchScalarGridSpec(
            num_scalar_prefetch=1,
            grid=(t // tm,),
            in_specs=[tile, pl.BlockSpec((tm, pd), lambda i, pos: (i, 0)), res((1, d)), res((d, d)), res((pd, d)),
                      pl.BlockSpec(memory_space=pl.ANY)],
            out_specs=tile,
            scratch_shapes=[pltpu.VMEM((2, tm, d), F32), pltpu.SemaphoreType.DMA((2,))],
        ),
        compiler_params=_params("arbitrary"),
        name="moe_combine_ple",
    )(pos, h2d, p2d, g.reshape(1, d), w_gate.astype(BF16), w_proj.astype(BF16), ys)


def _moe_ple(h2d, p2d, g_ffn, w_group, b_group, w_router, b_router, w_gate, w_up, w_down, g_ple, ple_gate, ple_proj):
    t, d = h2d.shape
    n_tiles = t // EXPERT_TILE + N_CLASSES
    hx, counts = _router(h2d, g_ffn, w_group, b_group, w_router, b_router)
    pos, tile_ea, tile_eb, tile_valid = _sorted_layout(hx, counts, d, n_tiles)
    hxs = _dispatch(hx, pos, n_tiles * EXPERT_TILE)
    ys = _experts(hxs, tile_ea, tile_eb, tile_valid, w_gate, w_up, w_down, d=d)
    return _combine_ple(h2d, ys, pos, p2d, g_ple, ple_gate, ple_proj)


def kernel(x, p, g_mix, g_ffn, g_ple, conv_w_pw1, conv_b_pw1, conv_w_dw, conv_b_dw, conv_ln_g, conv_ln_b, conv_w_pw2, conv_b_pw2, attn_w_qkv, attn_q_gain, attn_k_gain, attn_w_o, moe_w_group, moe_b_group, moe_w_router, moe_b_router, moe_w_gate, moe_w_up, moe_w_down, ple_w_gate, ple_w_proj):
    b, s, d = x.shape
    depth = g_mix.shape[0]
    t = b * s
    h = x
    for i in range(depth):
        if i % 2 == 0:
            c = i // 2
            h = _conv_mixer(h.reshape(b, s, d), g_mix[i], conv_w_pw1[c], conv_b_pw1[c], conv_w_dw[c],
                            conv_b_dw[c], conv_ln_g[c], conv_ln_b[c], conv_w_pw2[c], conv_b_pw2[c])
            h = h.reshape(t, d)
        else:
            a = i // 2
            h = h.reshape(t, d)
            q, k, v = _qkv(h, g_mix[i], attn_w_qkv[a], attn_q_gain[a], attn_k_gain[a])
            o = _moba(q.reshape(b, s, d), k.reshape(b, s, d), v.reshape(b, s, d), n_heads=N_HEADS)
            h = _matmul_residual(h, o.reshape(t, d), attn_w_o[a])
        h = _moe_ple(h, p[i].reshape(t, -1), g_ffn[i], moe_w_group[i], moe_b_group[i], moe_w_router[i],
                     moe_b_router[i], moe_w_gate[i], moe_w_up[i], moe_w_down[i], g_ple[i], ple_w_gate[i],
                     ple_w_proj[i])
    return h.reshape(b, s, d)
```

```python
import functools

import jax
import jax.numpy as jnp
import numpy as np
from jax import lax
from jax.experimental import pallas as pl
from jax.experimental.pallas import tpu as pltpu

F32 = jnp.float32
BF16 = jnp.bfloat16

EPS = 1e-6
NEG_INF = -1e30

N_HEADS = 8
MOBA_BLOCK = 256
MOBA_TOPK = 3
N_GROUPS = 4
EXPERTS_PER_GROUP = 4
N_EXPERTS = N_GROUPS * EXPERTS_PER_GROUP

V7X_VMEM_BYTES = 64 * 1024 * 1024
LANES = 128
SUBLANES = 8
VMEM_LIMIT = V7X_VMEM_BYTES * 7 // 8

ROUTER_LANES = LANES
EXPERT_LANE0 = N_GROUPS

CONV_HALO = 32
CONV_ROWS = 64
CONV_COLS = 128


def _params(*semantics):
    return pltpu.CompilerParams(dimension_semantics=semantics, vmem_limit_bytes=VMEM_LIMIT)


def _resident(shape):
    nd = len(shape)
    return pl.BlockSpec(shape, lambda *_: (0,) * nd, pipeline_mode=pl.Buffered(1))


def _rms(x, g):
    return x * lax.rsqrt(jnp.mean(x * x, axis=-1, keepdims=True) + EPS) * g


def _sigmoid(x):
    return 1.0 / (1.0 + jnp.exp(-x))


def _conv_mixer_kernel(x_ref, g_ref, w1_ref, b1_ref, wdw_ref, bdw_ref, lng_ref, lnb_ref,
                       w2_ref, b2_ref, o_ref, ubuf, cbuf, *, ts, d, kw):
    s = pl.program_id(1)
    x = x_ref[0]
    xn = _rms(x, g_ref[...])
    u = jnp.dot(xn.astype(BF16), w1_ref[...], preferred_element_type=F32) + b1_ref[...]
    glu = u[:, :d] * _sigmoid(u[:, d:])

    @pl.when(s == 0)
    def _():
        ubuf[0:CONV_HALO, :] = jnp.zeros((CONV_HALO, d), F32)

    ubuf[CONV_HALO:CONV_HALO + ts, :] = glu

    first = CONV_HALO - (kw - 1)

    def chunk(i, carry):
        cols = pl.ds(pl.multiple_of(i * CONV_COLS, CONV_COLS), CONV_COLS)
        for r0 in range(0, ts, CONV_ROWS):
            acc = None
            for j in range(SUBLANES):
                rows = CONV_ROWS + (SUBLANES if j else 0)
                sj = None
                for k in range(kw):
                    if (first + k) % SUBLANES == j:
                        a0 = r0 + (first + k) - j
                        term = ubuf[a0:a0 + rows, cols] * wdw_ref[k:k + 1, cols]
                        sj = term if sj is None else sj + term
                if sj is not None:
                    sj = sj[j:j + CONV_ROWS]
                    acc = sj if acc is None else acc + sj
            cbuf[r0:r0 + CONV_ROWS, cols] = acc
        return carry

    lax.fori_loop(0, d // CONV_COLS, chunk, 0)
    ubuf[0:CONV_HALO, :] = ubuf[ts:ts + CONV_HALO, :]

    c = cbuf[...] + bdw_ref[...]
    mu = jnp.mean(c, axis=-1, keepdims=True)
    xc = c - mu
    y = xc * lax.rsqrt(jnp.mean(xc * xc, axis=-1, keepdims=True) + EPS) * lng_ref[...] + lnb_ref[...]
    y = y * _sigmoid(y)
    o_ref[0] = x + jnp.dot(y.astype(BF16), w2_ref[...], preferred_element_type=F32) + b2_ref[...]


def _conv_mixer(h, g, w1, b1, wdw, bdw, lng, lnb, w2, b2, *, ts=256):
    b, s, d = h.shape
    kw = wdw.shape[0]
    assert s % ts == 0 and ts >= CONV_HALO >= kw - 1 and d % CONV_COLS == 0
    row = lambda a: a.reshape(1, -1)
    kern = functools.partial(_conv_mixer_kernel, ts=ts, d=d, kw=kw)
    return pl.pallas_call(
        kern,
        out_shape=jax.ShapeDtypeStruct((b, s, d), F32),
        grid=(b, s // ts),
        in_specs=[
            pl.BlockSpec((1, ts, d), lambda i, j: (i, j, 0)),
            _resident((1, d)), _resident((d, 2 * d)), _resident((1, 2 * d)),
            _resident((kw, d)), _resident((1, d)), _resident((1, d)), _resident((1, d)),
            _resident((d, d)), _resident((1, d)),
        ],
        out_specs=pl.BlockSpec((1, ts, d), lambda i, j: (i, j, 0)),
        scratch_shapes=[pltpu.VMEM((ts + CONV_HALO, d), F32), pltpu.VMEM((ts, d), F32)],
        compiler_params=_params("arbitrary", "arbitrary"),
        name="conv_mixer",
    )(h, row(g), w1.astype(BF16), row(b1), wdw, row(bdw), row(lng), row(lnb), w2.astype(BF16), row(b2))


def _qkv_kernel(x_ref, g_ref, w_ref, qg_ref, kg_ref, q_ref, k_ref, v_ref, *, d, hd):
    xn = _rms(x_ref[...], g_ref[...]).astype(BF16)
    qkv = jnp.dot(xn, w_ref[...], preferred_element_type=F32)
    for h in range(d // hd):
        cols = slice(h * hd, (h + 1) * hd)
        q_ref[:, cols] = _rms(qkv[:, h * hd:(h + 1) * hd], qg_ref[...]).astype(BF16)
        k_ref[:, cols] = _rms(qkv[:, d + h * hd:d + (h + 1) * hd], kg_ref[...]).astype(BF16)
    v_ref[...] = qkv[:, 2 * d:].astype(BF16)


def _qkv(h2d, g, w, qg, kg, *, tm=512):
    t, d = h2d.shape
    hd = qg.shape[0]
    kern = functools.partial(_qkv_kernel, d=d, hd=hd)
    tile = pl.BlockSpec((tm, d), lambda i: (i, 0))
    return pl.pallas_call(
        kern,
        out_shape=[jax.ShapeDtypeStruct((t, d), BF16)] * 3,
        grid=(t // tm,),
        in_specs=[tile, _resident((1, d)), _resident((d, 3 * d)), _resident((1, hd)), _resident((1, hd))],
        out_specs=[tile, tile, tile],
        compiler_params=_params("arbitrary"),
        name="attn_qkv",
    )(h2d, g.reshape(1, d), w.astype(BF16), qg.reshape(1, hd), kg.reshape(1, hd))


def _moba_kernel(slope_ref, q_ref, k_ref, v_ref, kaux_ref, vaux_ref, o_ref, kaug, vaug, zbuf, pbuf,
                 *, nb, blk, hd, topk):
    scale = hd ** -0.5
    negz = NEG_INF / scale
    slope = slope_ref[0][:, :1]
    kaug[:, :hd] = k_ref[0]
    kaug[:, hd:] = kaux_ref[...]
    vaug[:, :hd] = v_ref[0]
    vaug[:, hd:] = vaux_ref[...]

    km = jnp.concatenate([jnp.mean(k_ref[0, n * blk:(n + 1) * blk, :].astype(F32), axis=0, keepdims=True)
                          for n in range(nb)], axis=0)
    hi = km.astype(BF16).astype(F32)
    mid = (km - hi).astype(BF16).astype(F32)
    lo = (km - hi - mid).astype(BF16).astype(F32)
    pad = [jnp.zeros((-3 * nb % (2 * SUBLANES), hd), F32)] if 3 * nb % (2 * SUBLANES) else []
    km3 = jnp.concatenate([hi, mid, lo] + pad, axis=0).astype(BF16)

    r = lax.broadcasted_iota(jnp.int32, (blk, blk), 0)
    c = lax.broadcasted_iota(jnp.int32, (blk, blk), 1)
    causal = r >= c
    keypos = lax.broadcasted_iota(jnp.int32, (1, blk), 1).astype(F32)
    nt = (((1,), (1,)), ((), ()))

    for j in range(nb):
        q = q_ref[0, j * blk:(j + 1) * blk, :]
        select = j > topk
        if select:
            g3 = lax.dot_general(km3, q, nt, preferred_element_type=F32)
            gate = g3[0:nb] + g3[nb:2 * nb] + g3[2 * nb:3 * nb]
            g = [gate[m:m + 1, :] for m in range(j)]
            rows = []
            for n in range(j):
                beaten = jnp.zeros((1, blk), F32)
                for m in range(j):
                    if m != n:
                        ahead = (g[m] > g[n]) | (g[m] == g[n]) if m < n else (g[m] > g[n])
                        beaten = beaten + jnp.where(ahead, 1.0, 0.0)
                rows.append(jnp.where(beaten < topk, 0.0, negz))
            bias_t = jnp.concatenate(rows + [jnp.zeros((LANES - j, blk), F32)], axis=0)
            q = jnp.concatenate([q, bias_t.T.astype(BF16)], axis=1)

        mrun = None
        for n in range(j + 1):
            keys = kaug[n * blk:(n + 1) * blk, :] if select else k_ref[0, n * blk:(n + 1) * blk, :]
            z = lax.dot_general(q, keys, nt, preferred_element_type=F32)
            u = z * scale + slope * (keypos + float(n * blk))
            if n == j:
                u = jnp.where(causal, u, NEG_INF)
            zbuf[:, n * blk:(n + 1) * blk] = u
            for half in range(blk // LANES):
                f = u[:, half * LANES:(half + 1) * LANES]
                mrun = f if mrun is None else jnp.maximum(mrun, f)
        mb = jnp.broadcast_to(jnp.max(mrun, axis=-1, keepdims=True), (blk, LANES))
        for n in range(j + 1):
            for half in range(blk // LANES):
                cols = slice(n * blk + half * LANES, n * blk + (half + 1) * LANES)
                pbuf[:, cols] = jnp.exp(zbuf[:, cols] - mb).astype(BF16)
        acc = jnp.dot(pbuf[:, :(j + 1) * blk], vaug[:(j + 1) * blk, :], preferred_element_type=F32)
        o_ref[0, j * blk:(j + 1) * blk, :] = (acc[:, :hd] / acc[:, hd:hd + 1]).astype(BF16)


def _moba(q, k, v, *, n_heads):
    b, s, d = q.shape
    hd = d // n_heads
    blk = MOBA_BLOCK
    assert s % blk == 0 and hd == LANES and blk % LANES == 0
    nb = s // blk
    assert nb <= LANES
    topk = max(1, min(MOBA_TOPK, nb - 1))
    slopes = jnp.exp2(-8.0 * jnp.arange(1, n_heads + 1, dtype=F32) / n_heads)
    slopes = jnp.broadcast_to(slopes[:, None, None], (n_heads, 1, LANES))
    kaux = np.zeros((s, LANES), np.float32)
    kaux[np.arange(s), np.arange(s) // blk] = 1.0
    vaux = np.zeros((s, LANES), np.float32)
    vaux[:, 0] = 1.0
    kern = functools.partial(_moba_kernel, nb=nb, blk=blk, hd=hd, topk=topk)
    head = pl.BlockSpec((1, s, hd), lambda i, h: (i, 0, h))
    return pl.pallas_call(
        kern,
        out_shape=jax.ShapeDtypeStruct((b, s, d), BF16),
        grid=(b, n_heads),
        in_specs=[pl.BlockSpec((1, 1, LANES), lambda i, h: (h, 0, 0)), head, head, head,
                  _resident((s, LANES)), _resident((s, LANES))],
        out_specs=head,
        scratch_shapes=[pltpu.VMEM((s, hd + LANES), BF16), pltpu.VMEM((s, hd + LANES), BF16),
                        pltpu.VMEM((blk, s), F32), pltpu.VMEM((blk, s), BF16)],
        compiler_params=_params("arbitrary", "arbitrary"),
        name="moba_attn",
    )(slopes, q, k, v, jnp.asarray(kaux, BF16), jnp.asarray(vaux, BF16))


def _matmul_residual_kernel(h_ref, a_ref, w_ref, o_ref):
    o_ref[...] = h_ref[...] + jnp.dot(a_ref[...], w_ref[...], preferred_element_type=F32)


def _matmul_residual(h2d, a, w, *, tm=512):
    t, d = h2d.shape
    tile = pl.BlockSpec((tm, d), lambda i: (i, 0))
    return pl.pallas_call(
        _matmul_residual_kernel,
        out_shape=jax.ShapeDtypeStruct((t, d), F32),
        grid=(t // tm,),
        in_specs=[tile, pl.BlockSpec((tm, a.shape[1]), lambda i: (i, 0)), _resident(w.shape)],
        out_specs=tile,
        compiler_params=_params("arbitrary"),
        name="attn_out_proj",
    )(h2d, a, w.astype(BF16))


PAIRS = [(a, b) for a in range(EXPERTS_PER_GROUP) for b in range(a + 1, EXPERTS_PER_GROUP)]
N_CLASSES = N_GROUPS * len(PAIRS)
META_CLASS, META_RANK, META_WA, META_WB = 0, 1, 2, 3
EXPERT_TILE = 256


def _router_kernel(h_ref, g_ref, wr_ref, br_ref, tri_ref, hx_ref, route_ref, cnt_ref, running, *, d):
    @pl.when(pl.program_id(0) == 0)
    def _():
        running[...] = jnp.zeros_like(running)

    xn = _rms(h_ref[...], g_ref[...])
    logits = jnp.dot(xn, wr_ref[...], precision=lax.Precision.HIGHEST, preferred_element_type=F32) + br_ref[...]
    lane = lax.broadcasted_iota(jnp.int32, logits.shape, 1).astype(F32)

    def first_max(vals):
        mx = jnp.max(vals, axis=-1, keepdims=True)
        return mx, jnp.min(jnp.where(vals == mx, lane, float(ROUTER_LANES)), axis=-1, keepdims=True)

    is_g = lane < N_GROUPS
    gmax, gidx = first_max(jnp.where(is_g, logits, -jnp.inf))
    zg = jnp.sum(jnp.where(is_g, jnp.exp(logits - gmax), 0.0), axis=-1, keepdims=True)
    g_w = 1.0 / zg
    lo = EXPERT_LANE0 + gidx * EXPERTS_PER_GROUP
    in_grp = (lane >= lo) & (lane < lo + EXPERTS_PER_GROUP)
    el = jnp.where(in_grp, logits, -jnp.inf)
    m1, i1 = first_max(el)
    m2, i2 = first_max(jnp.where(lane == i1, -jnp.inf, el))
    p2 = jnp.exp(m2 - m1)
    w1 = g_w / (1.0 + p2)
    w2 = g_w * p2 / (1.0 + p2)
    e1, e2 = i1 - lo, i2 - lo
    ea, eb = jnp.minimum(e1, e2), jnp.maximum(e1, e2)
    wa, wb = jnp.where(e1 < e2, w1, w2), jnp.where(e1 < e2, w2, w1)
    pair = ea * (2 * EXPERTS_PER_GROUP - 1 - ea) * 0.5 + (eb - ea - 1.0)
    cls = gidx * float(len(PAIRS)) + pair
    onehot = jnp.where(lane == cls, 1.0, 0.0)
    before = jnp.dot(tri_ref[...], onehot.astype(BF16), preferred_element_type=F32) + running[...]
    rank = jnp.sum(onehot * before, axis=-1, keepdims=True)
    running[...] += jnp.sum(onehot, axis=0, keepdims=True)
    cnt_ref[...] = running[...]

    meta = (jnp.where(lane == META_CLASS, cls, 0.0) + jnp.where(lane == META_RANK, rank, 0.0)
            + jnp.where(lane == META_WA, wa, 0.0) + jnp.where(lane == META_WB, wb, 0.0))
    hx_ref[:, :d] = xn
    hx_ref[:, d:] = meta
    route_ref[...] = meta.T[:SUBLANES, :]


def _router(h2d, g, w_group, b_group, w_router, b_router, *, tm=512):
    t, d = h2d.shape
    ng, _, ne = w_router.shape
    assert (ng, ne) == (N_GROUPS, EXPERTS_PER_GROUP) and N_CLASSES <= ROUTER_LANES
    wr = jnp.concatenate([w_group, jnp.transpose(w_router, (1, 0, 2)).reshape(d, ng * ne)], axis=1)
    br = jnp.concatenate([b_group, b_router.reshape(ng * ne)])
    pad = ROUTER_LANES - wr.shape[1]
    wr = jnp.pad(wr, ((0, 0), (0, pad)))
    br = jnp.pad(br, (0, pad)).reshape(1, ROUTER_LANES)
    tri = jnp.asarray(np.tril(np.ones((tm, tm), np.float32), -1), BF16)
    return pl.pallas_call(
        functools.partial(_router_kernel, d=d),
        out_shape=[jax.ShapeDtypeStruct((t, d + ROUTER_LANES), F32), jax.ShapeDtypeStruct((SUBLANES, t), F32),
                   jax.ShapeDtypeStruct((1, ROUTER_LANES), F32)],
        grid=(t // tm,),
        in_specs=[pl.BlockSpec((tm, d), lambda i: (i, 0)), _resident((1, d)), _resident((d, ROUTER_LANES)),
                  _resident((1, ROUTER_LANES)), _resident((tm, tm))],
        out_specs=[pl.BlockSpec((tm, d + ROUTER_LANES), lambda i: (i, 0)),
                   pl.BlockSpec((SUBLANES, tm), lambda i: (0, i)),
                   pl.BlockSpec((1, ROUTER_LANES), lambda i: (0, 0))],
        scratch_shapes=[pltpu.VMEM((1, ROUTER_LANES), F32)],
        compiler_params=_params("arbitrary"),
        name="moe_router",
    )(h2d, g.reshape(1, d), wr, br, tri)


def _sorted_layout(route, counts, n_tiles):
    cls = route[META_CLASS].astype(jnp.int32)
    rank = route[META_RANK].astype(jnp.int32)
    cnt = counts[0, :N_CLASSES].astype(jnp.int32)
    tiles = (cnt + EXPERT_TILE - 1) // EXPERT_TILE
    ids = jnp.arange(N_CLASSES, dtype=jnp.int32)
    tile_end = jnp.sum(jnp.where(ids[None, :] <= ids[:, None], tiles[None, :], 0), axis=1)
    start = (tile_end - tiles) * EXPERT_TILE
    pos = jnp.sum(jnp.where(cls[None, :] == ids[:, None], start[:, None], 0), axis=0) + rank
    tile_ids = jnp.arange(n_tiles, dtype=jnp.int32)
    tile_cls = jnp.sum((tile_end[None, :] <= tile_ids[:, None]).astype(jnp.int32), axis=1)
    valid = (tile_cls < N_CLASSES).astype(jnp.int32)
    last = jnp.max(jnp.where(cnt > 0, ids, 0))
    tile_cls = jnp.where(valid == 1, tile_cls, last)
    grp, pair = tile_cls // len(PAIRS), tile_cls % len(PAIRS)
    pa = sum(jnp.where(pair == n, a, 0) for n, (a, _) in enumerate(PAIRS))
    pb = sum(jnp.where(pair == n, b, 0) for n, (_, b) in enumerate(PAIRS))
    return pos, grp * EXPERTS_PER_GROUP + pa, grp * EXPERTS_PER_GROUP + pb, valid


DISPATCH_CHUNK = 2048
ROW_DMA_UNROLL = 8


def _row_copy(src, dst, i, j, sem):
    return pltpu.make_async_copy(src.at[pl.ds(i, 1)], dst.at[pl.ds(j, 1)], sem)


def _dispatch_kernel(pos_ref, hx_ref, init_ref, hxs_ref, sem):
    del init_ref
    base = pl.program_id(0) * DISPATCH_CHUNK

    def issue(r, carry):
        _row_copy(hx_ref, hxs_ref, r, pos_ref[base + r], sem).start()
        return carry

    lax.fori_loop(0, DISPATCH_CHUNK, issue, 0, unroll=ROW_DMA_UNROLL)
    pltpu.make_async_copy(hx_ref, hxs_ref.at[pl.ds(0, DISPATCH_CHUNK)], sem).wait()


def _dispatch(hx, pos, n_rows):
    t, w = hx.shape
    assert t % DISPATCH_CHUNK == 0
    return pl.pallas_call(
        _dispatch_kernel,
        out_shape=jax.ShapeDtypeStruct((n_rows, w), F32),
        grid_spec=pltpu.PrefetchScalarGridSpec(
            num_scalar_prefetch=1,
            grid=(t // DISPATCH_CHUNK,),
            in_specs=[pl.BlockSpec((DISPATCH_CHUNK, w), lambda i, pos: (i, 0)), pl.BlockSpec(memory_space=pl.ANY)],
            out_specs=pl.BlockSpec(memory_space=pl.ANY),
            scratch_shapes=[pltpu.SemaphoreType.DMA(())],
        ),
        input_output_aliases={2: 0},
        compiler_params=_params("arbitrary"),
        name="moe_dispatch",
    )(pos, hx, jnp.zeros((n_rows, w), F32))


def _experts_kernel(ea_ref, eb_ref, valid_ref, x_ref, wga_ref, wua_ref, wda_ref, wgb_ref, wub_ref, wdb_ref,
                    y_ref, *, d):
    del ea_ref, eb_ref
    i = pl.program_id(0)

    @pl.when(valid_ref[i] == 1)
    def _():
        x = x_ref[:, :d].astype(BF16)

        def hidden(wg_ref, wu_ref, lane):
            gate = jnp.dot(x, wg_ref[0], preferred_element_type=F32)
            up = jnp.dot(x, wu_ref[0], preferred_element_type=F32)
            return (gate * _sigmoid(gate) * up * x_ref[:, d + lane:d + lane + 1]).astype(BF16)

        y_ref[...] = (jnp.dot(hidden(wga_ref, wua_ref, META_WA), wda_ref[0], preferred_element_type=F32)
                      + jnp.dot(hidden(wgb_ref, wub_ref, META_WB), wdb_ref[0], preferred_element_type=F32))

    @pl.when(valid_ref[i] == 0)
    def _():
        y_ref[...] = jnp.zeros_like(y_ref)


def _experts(hxs, tile_ea, tile_eb, tile_valid, w_gate, w_up, w_down, *, d):
    n_rows, w = hxs.shape
    _, _, hid = w_gate.shape
    n_tiles = n_rows // EXPERT_TILE
    wg, wu, wd = w_gate.astype(BF16), w_up.astype(BF16), w_down.astype(BF16)
    up_a = pl.BlockSpec((1, d, hid), lambda i, ea, eb, v: (ea[i], 0, 0))
    up_b = pl.BlockSpec((1, d, hid), lambda i, ea, eb, v: (eb[i], 0, 0))
    down_a = pl.BlockSpec((1, hid, d), lambda i, ea, eb, v: (ea[i], 0, 0))
    down_b = pl.BlockSpec((1, hid, d), lambda i, ea, eb, v: (eb[i], 0, 0))
    return pl.pallas_call(
        functools.partial(_experts_kernel, d=d),
        out_shape=jax.ShapeDtypeStruct((n_rows, d), F32),
        grid_spec=pltpu.PrefetchScalarGridSpec(
            num_scalar_prefetch=3,
            grid=(n_tiles,),
            in_specs=[pl.BlockSpec((EXPERT_TILE, w), lambda i, ea, eb, v: (i, 0)),
                      up_a, up_a, down_a, up_b, up_b, down_b],
            out_specs=pl.BlockSpec((EXPERT_TILE, d), lambda i, ea, eb, v: (i, 0)),
        ),
        compiler_params=_params("arbitrary"),
        name="moe_experts",
    )(tile_ea, tile_eb, tile_valid, hxs, wg, wu, wd, wg, wu, wd)


def _combine_ple_kernel(pos_ref, h_ref, p_ref, g_ref, wg_ref, wp_ref, ys_ref, o_ref, ybuf, sems, *, tm):
    i = pl.program_id(0)
    n = pl.num_programs(0)
    slot = i % 2

    def gather(tile, s):
        def issue(r, carry):
            _row_copy(ys_ref, ybuf.at[s], pos_ref[tile * tm + r], r, sems.at[s]).start()
            return carry
        lax.fori_loop(0, tm, issue, 0, unroll=ROW_DMA_UNROLL)

    @pl.when(i == 0)
    def _():
        gather(0, 0)

    @pl.when(i + 1 < n)
    def _():
        gather(i + 1, 1 - slot)

    pltpu.make_async_copy(ys_ref.at[pl.ds(0, tm)], ybuf.at[slot], sems.at[slot]).wait()
    h = h_ref[...] + ybuf[slot]
    gate = _sigmoid(jnp.dot(_rms(h, g_ref[...]).astype(BF16), wg_ref[...], preferred_element_type=F32))
    proj = jnp.dot(p_ref[...].astype(BF16), wp_ref[...], preferred_element_type=F32)
    o_ref[...] = h + gate * proj


def _combine_ple(h2d, ys, pos, p2d, g, w_gate, w_proj, *, tm=512):
    t, d = h2d.shape
    pd = p2d.shape[1]
    tile = pl.BlockSpec((tm, d), lambda i, pos: (i, 0))
    res = lambda shape: pl.BlockSpec(shape, lambda i, pos: (0,) * len(shape), pipeline_mode=pl.Buffered(1))
    return pl.pallas_call(
        functools.partial(_combine_ple_kernel, tm=tm),
        out_shape=jax.ShapeDtypeStruct((t, d), F32),
        grid_spec=pltpu.PrefetchScalarGridSpec(
            num_scalar_prefetch=1,
            grid=(t // tm,),
            in_specs=[tile, pl.BlockSpec((tm, pd), lambda i, pos: (i, 0)), res((1, d)), res((d, d)), res((pd, d)),
                      pl.BlockSpec(memory_space=pl.ANY)],
            out_specs=tile,
            scratch_shapes=[pltpu.VMEM((2, tm, d), F32), pltpu.SemaphoreType.DMA((2,))],
        ),
        compiler_params=_params("arbitrary"),
        name="moe_combine_ple",
    )(pos, h2d, p2d, g.reshape(1, d), w_gate.astype(BF16), w_proj.astype(BF16), ys)


def _moe_ple(h2d, p2d, g_ffn, w_group, b_group, w_router, b_router, w_gate, w_up, w_down, g_ple, ple_gate, ple_proj):
    t, d = h2d.shape
    n_tiles = t // EXPERT_TILE + N_CLASSES
    hx, route, counts = _router(h2d, g_ffn, w_group, b_group, w_router, b_router)
    pos, tile_ea, tile_eb, tile_valid = _sorted_layout(route, counts, n_tiles)
    hxs = _dispatch(hx, pos, n_tiles * EXPERT_TILE)
    ys = _experts(hxs, tile_ea, tile_eb, tile_valid, w_gate, w_up, w_down, d=d)
    return _combine_ple(h2d, ys, pos, p2d, g_ple, ple_gate, ple_proj)


def kernel(x, p, g_mix, g_ffn, g_ple, conv_w_pw1, conv_b_pw1, conv_w_dw, conv_b_dw, conv_ln_g, conv_ln_b, conv_w_pw2, conv_b_pw2, attn_w_qkv, attn_q_gain, attn_k_gain, attn_w_o, moe_w_group, moe_b_group, moe_w_router, moe_b_router, moe_w_gate, moe_w_up, moe_w_down, ple_w_gate, ple_w_proj):
    b, s, d = x.shape
    depth = g_mix.shape[0]
    t = b * s
    h = x
    for i in range(depth):
        if i % 2 == 0:
            c = i // 2
            h = _conv_mixer(h.reshape(b, s, d), g_mix[i], conv_w_pw1[c], conv_b_pw1[c], conv_w_dw[c],
                            conv_b_dw[c], conv_ln_g[c], conv_ln_b[c], conv_w_pw2[c], conv_b_pw2[c])
            h = h.reshape(t, d)
        else:
            a = i // 2
            h = h.reshape(t, d)
            q, k, v = _qkv(h, g_mix[i], attn_w_qkv[a], attn_q_gain[a], attn_k_gain[a])
            o = _moba(q.reshape(b, s, d), k.reshape(b, s, d), v.reshape(b, s, d), n_heads=N_HEADS)
            h = _matmul_residual(h, o.reshape(t, d), attn_w_o[a])
        h = _moe_ple(h, p[i].reshape(t, -1), g_ffn[i], moe_w_group[i], moe_b_group[i], moe_w_router[i],
                     moe_b_router[i], moe_w_gate[i], moe_w_up[i], moe_w_down[i], g_ple[i], ple_w_gate[i],
                     ple_w_proj[i])
    return h.reshape(b, s, d)
```

```python
import functools

import jax
import jax.numpy as jnp
import numpy as np
from jax import lax
from jax.experimental import pallas as pl
from jax.experimental.pallas import tpu as pltpu

F32 = jnp.float32
BF16 = jnp.bfloat16

EPS = 1e-6
NEG_INF = -1e30

N_HEADS = 8
MOBA_BLOCK = 256
MOBA_TOPK = 3
N_GROUPS = 4
EXPERTS_PER_GROUP = 4
N_EXPERTS = N_GROUPS * EXPERTS_PER_GROUP

V7X_VMEM_BYTES = 64 * 1024 * 1024
LANES = 128
SUBLANES = 8
VMEM_LIMIT = V7X_VMEM_BYTES * 7 // 8

ROUTER_LANES = LANES
EXPERT_LANE0 = N_GROUPS

CONV_HALO = 32
CONV_ROWS = 64
CONV_COLS = 128


def _params(*semantics):
    return pltpu.CompilerParams(dimension_semantics=semantics, vmem_limit_bytes=VMEM_LIMIT)


def _resident(shape):
    nd = len(shape)
    return pl.BlockSpec(shape, lambda *_: (0,) * nd, pipeline_mode=pl.Buffered(1))


def _rms(x, g):
    return x * lax.rsqrt(jnp.mean(x * x, axis=-1, keepdims=True) + EPS) * g


def _sigmoid(x):
    return 1.0 / (1.0 + jnp.exp(-x))


def _conv_mixer_kernel(x_ref, g_ref, w1_ref, b1_ref, wdw_ref, bdw_ref, lng_ref, lnb_ref,
                       w2_ref, b2_ref, o_ref, ubuf, cbuf, *, ts, d, kw):
    s = pl.program_id(1)
    x = x_ref[0]
    xn = _rms(x, g_ref[...])
    u = jnp.dot(xn.astype(BF16), w1_ref[...], preferred_element_type=F32) + b1_ref[...]
    glu = u[:, :d] * _sigmoid(u[:, d:])

    @pl.when(s == 0)
    def _():
        ubuf[0:CONV_HALO, :] = jnp.zeros((CONV_HALO, d), F32)

    ubuf[CONV_HALO:CONV_HALO + ts, :] = glu

    first = CONV_HALO - (kw - 1)

    def chunk(i, carry):
        cols = pl.ds(pl.multiple_of(i * CONV_COLS, CONV_COLS), CONV_COLS)
        for r0 in range(0, ts, CONV_ROWS):
            acc = None
            for j in range(SUBLANES):
                rows = CONV_ROWS + (SUBLANES if j else 0)
                sj = None
                for k in range(kw):
                    if (first + k) % SUBLANES == j:
                        a0 = r0 + (first + k) - j
                        term = ubuf[a0:a0 + rows, cols] * wdw_ref[k:k + 1, cols]
                        sj = term if sj is None else sj + term
                if sj is not None:
                    sj = sj[j:j + CONV_ROWS]
                    acc = sj if acc is None else acc + sj
            cbuf[r0:r0 + CONV_ROWS, cols] = acc
        return carry

    lax.fori_loop(0, d // CONV_COLS, chunk, 0)
    ubuf[0:CONV_HALO, :] = ubuf[ts:ts + CONV_HALO, :]

    c = cbuf[...] + bdw_ref[...]
    mu = jnp.mean(c, axis=-1, keepdims=True)
    xc = c - mu
    y = xc * lax.rsqrt(jnp.mean(xc * xc, axis=-1, keepdims=True) + EPS) * lng_ref[...] + lnb_ref[...]
    y = y * _sigmoid(y)
    o_ref[0] = x + jnp.dot(y.astype(BF16), w2_ref[...], preferred_element_type=F32) + b2_ref[...]


def _conv_mixer(h, g, w1, b1, wdw, bdw, lng, lnb, w2, b2, *, ts=256):
    b, s, d = h.shape
    kw = wdw.shape[0]
    assert s % ts == 0 and ts >= CONV_HALO >= kw - 1 and d % CONV_COLS == 0
    row = lambda a: a.reshape(1, -1)
    kern = functools.partial(_conv_mixer_kernel, ts=ts, d=d, kw=kw)
    return pl.pallas_call(
        kern,
        out_shape=jax.ShapeDtypeStruct((b, s, d), F32),
        grid=(b, s // ts),
        in_specs=[
            pl.BlockSpec((1, ts, d), lambda i, j: (i, j, 0)),
            _resident((1, d)), _resident((d, 2 * d)), _resident((1, 2 * d)),
            _resident((kw, d)), _resident((1, d)), _resident((1, d)), _resident((1, d)),
            _resident((d, d)), _resident((1, d)),
        ],
        out_specs=pl.BlockSpec((1, ts, d), lambda i, j: (i, j, 0)),
        scratch_shapes=[pltpu.VMEM((ts + CONV_HALO, d), F32), pltpu.VMEM((ts, d), F32)],
        compiler_params=_params("arbitrary", "arbitrary"),
        name="conv_mixer",
    )(h, row(g), w1.astype(BF16), row(b1), wdw, row(bdw), row(lng), row(lnb), w2.astype(BF16), row(b2))


def _qkv_kernel(x_ref, g_ref, w_ref, qg_ref, kg_ref, q_ref, k_ref, v_ref, *, d, hd):
    xn = _rms(x_ref[...], g_ref[...]).astype(BF16)
    qkv = jnp.dot(xn, w_ref[...], preferred_element_type=F32)
    for h in range(d // hd):
        cols = slice(h * hd, (h + 1) * hd)
        q_ref[:, cols] = _rms(qkv[:, h * hd:(h + 1) * hd], qg_ref[...]).astype(BF16)
        k_ref[:, cols] = _rms(qkv[:, d + h * hd:d + (h + 1) * hd], kg_ref[...]).astype(BF16)
    v_ref[...] = qkv[:, 2 * d:].astype(BF16)


def _qkv(h2d, g, w, qg, kg, *, tm=512):
    t, d = h2d.shape
    hd = qg.shape[0]
    kern = functools.partial(_qkv_kernel, d=d, hd=hd)
    tile = pl.BlockSpec((tm, d), lambda i: (i, 0))
    return pl.pallas_call(
        kern,
        out_shape=[jax.ShapeDtypeStruct((t, d), BF16)] * 3,
        grid=(t // tm,),
        in_specs=[tile, _resident((1, d)), _resident((d, 3 * d)), _resident((1, hd)), _resident((1, hd))],
        out_specs=[tile, tile, tile],
        compiler_params=_params("arbitrary"),
        name="attn_qkv",
    )(h2d, g.reshape(1, d), w.astype(BF16), qg.reshape(1, hd), kg.reshape(1, hd))


def _moba_kernel(slope_ref, q_ref, k_ref, v_ref, kaux_ref, vaux_ref, o_ref, kaug, vaug, zbuf, pbuf,
                 *, nb, blk, hd, topk):
    scale = hd ** -0.5
    negz = NEG_INF / scale
    slope = slope_ref[0][:, :1]
    kaug[:, :hd] = k_ref[0]
    kaug[:, hd:] = kaux_ref[...]
    vaug[:, :hd] = v_ref[0]
    vaug[:, hd:] = vaux_ref[...]

    km = jnp.concatenate([jnp.mean(k_ref[0, n * blk:(n + 1) * blk, :].astype(F32), axis=0, keepdims=True)
                          for n in range(nb)], axis=0)
    hi = km.astype(BF16).astype(F32)
    mid = (km - hi).astype(BF16).astype(F32)
    lo = (km - hi - mid).astype(BF16).astype(F32)
    pad = [jnp.zeros((-3 * nb % (2 * SUBLANES), hd), F32)] if 3 * nb % (2 * SUBLANES) else []
    km3 = jnp.concatenate([hi, mid, lo] + pad, axis=0).astype(BF16)

    r = lax.broadcasted_iota(jnp.int32, (blk, blk), 0)
    c = lax.broadcasted_iota(jnp.int32, (blk, blk), 1)
    causal = r >= c
    keypos = lax.broadcasted_iota(jnp.int32, (1, blk), 1).astype(F32)
    nt = (((1,), (1,)), ((), ()))

    for j in range(nb):
        q = q_ref[0, j * blk:(j + 1) * blk, :]
        select = j > topk
        if select:
            g3 = lax.dot_general(km3, q, nt, preferred_element_type=F32)
            gate = g3[0:nb] + g3[nb:2 * nb] + g3[2 * nb:3 * nb]
            g = [gate[m:m + 1, :] for m in range(j)]
            rows = []
            for n in range(j):
                beaten = jnp.zeros((1, blk), F32)
                for m in range(j):
                    if m != n:
                        ahead = (g[m] > g[n]) | (g[m] == g[n]) if m < n else (g[m] > g[n])
                        beaten = beaten + jnp.where(ahead, 1.0, 0.0)
                rows.append(jnp.where(beaten < topk, 0.0, negz))
            bias_t = jnp.concatenate(rows + [jnp.zeros((LANES - j, blk), F32)], axis=0)
            q = jnp.concatenate([q, bias_t.T.astype(BF16)], axis=1)

        mrun = None
        for n in range(j + 1):
            keys = kaug[n * blk:(n + 1) * blk, :] if select else k_ref[0, n * blk:(n + 1) * blk, :]
            z = lax.dot_general(q, keys, nt, preferred_element_type=F32)
            u = z * scale + slope * (keypos + float(n * blk))
            if n == j:
                u = jnp.where(causal, u, NEG_INF)
            zbuf[:, n * blk:(n + 1) * blk] = u
            for half in range(blk // LANES):
                f = u[:, half * LANES:(half + 1) * LANES]
                mrun = f if mrun is None else jnp.maximum(mrun, f)
        mb = jnp.broadcast_to(jnp.max(mrun, axis=-1, keepdims=True), (blk, LANES))
        for n in range(j + 1):
            for half in range(blk // LANES):
                cols = slice(n * blk + half * LANES, n * blk + (half + 1) * LANES)
                pbuf[:, cols] = jnp.exp(zbuf[:, cols] - mb).astype(BF16)
        acc = jnp.dot(pbuf[:, :(j + 1) * blk], vaug[:(j + 1) * blk, :], preferred_element_type=F32)
        o_ref[0, j * blk:(j + 1) * blk, :] = (acc[:, :hd] / acc[:, hd:hd + 1]).astype(BF16)


def _moba(q, k, v, *, n_heads):
    b, s, d = q.shape
    hd = d // n_heads
    blk = MOBA_BLOCK
    assert s % blk == 0 and hd == LANES and blk % LANES == 0
    nb = s // blk
    assert nb <= LANES
    topk = max(1, min(MOBA_TOPK, nb - 1))
    slopes = jnp.exp2(-8.0 * jnp.arange(1, n_heads + 1, dtype=F32) / n_heads)
    slopes = jnp.broadcast_to(slopes[:, None, None], (n_heads, 1, LANES))
    kaux = np.zeros((s, LANES), np.float32)
    kaux[np.arange(s), np.arange(s) // blk] = 1.0
    vaux = np.zeros((s, LANES), np.float32)
    vaux[:, 0] = 1.0
    kern = functools.partial(_moba_kernel, nb=nb, blk=blk, hd=hd, topk=topk)
    head = pl.BlockSpec((1, s, hd), lambda i, h: (i, 0, h))
    return pl.pallas_call(
        kern,
        out_shape=jax.ShapeDtypeStruct((b, s, d), BF16),
        grid=(b, n_heads),
        in_specs=[pl.BlockSpec((1, 1, LANES), lambda i, h: (h, 0, 0)), head, head, head,
                  _resident((s, LANES)), _resident((s, LANES))],
        out_specs=head,
        scratch_shapes=[pltpu.VMEM((s, hd + LANES), BF16), pltpu.VMEM((s, hd + LANES), BF16),
                        pltpu.VMEM((blk, s), F32), pltpu.VMEM((blk, s), BF16)],
        compiler_params=_params("arbitrary", "arbitrary"),
        name="moba_attn",
    )(slopes, q, k, v, jnp.asarray(kaux, BF16), jnp.asarray(vaux, BF16))


def _matmul_residual_kernel(h_ref, a_ref, w_ref, o_ref):
    o_ref[...] = h_ref[...] + jnp.dot(a_ref[...], w_ref[...], preferred_element_type=F32)


def _matmul_residual(h2d, a, w, *, tm=512):
    t, d = h2d.shape
    tile = pl.BlockSpec((tm, d), lambda i: (i, 0))
    return pl.pallas_call(
        _matmul_residual_kernel,
        out_shape=jax.ShapeDtypeStruct((t, d), F32),
        grid=(t // tm,),
        in_specs=[tile, pl.BlockSpec((tm, a.shape[1]), lambda i: (i, 0)), _resident(w.shape)],
        out_specs=tile,
        compiler_params=_params("arbitrary"),
        name="attn_out_proj",
    )(h2d, a, w.astype(BF16))


PAIRS = [(a, b) for a in range(EXPERTS_PER_GROUP) for b in range(a + 1, EXPERTS_PER_GROUP)]
N_CLASSES = N_GROUPS * len(PAIRS)
META_CLASS, META_RANK, META_WA, META_WB = 0, 1, 2, 3
EXPERT_TILE = 256


def _router_kernel(h_ref, g_ref, wr_ref, br_ref, tri_ref, hx_ref, route_ref, cnt_ref, running, *, d):
    @pl.when(pl.program_id(0) == 0)
    def _():
        running[...] = jnp.zeros_like(running)

    xn = _rms(h_ref[...], g_ref[...])
    logits = jnp.dot(xn, wr_ref[...], precision=lax.Precision.HIGHEST, preferred_element_type=F32) + br_ref[...]
    lane = lax.broadcasted_iota(jnp.int32, logits.shape, 1).astype(F32)

    def first_max(vals):
        mx = jnp.max(vals, axis=-1, keepdims=True)
        return mx, jnp.min(jnp.where(vals == mx, lane, float(ROUTER_LANES)), axis=-1, keepdims=True)

    is_g = lane < N_GROUPS
    gmax, gidx = first_max(jnp.where(is_g, logits, -jnp.inf))
    zg = jnp.sum(jnp.where(is_g, jnp.exp(logits - gmax), 0.0), axis=-1, keepdims=True)
    g_w = 1.0 / zg
    lo = EXPERT_LANE0 + gidx * EXPERTS_PER_GROUP
    in_grp = (lane >= lo) & (lane < lo + EXPERTS_PER_GROUP)
    el = jnp.where(in_grp, logits, -jnp.inf)
    m1, i1 = first_max(el)
    m2, i2 = first_max(jnp.where(lane == i1, -jnp.inf, el))
    p2 = jnp.exp(m2 - m1)
    w1 = g_w / (1.0 + p2)
    w2 = g_w * p2 / (1.0 + p2)
    e1, e2 = i1 - lo, i2 - lo
    ea, eb = jnp.minimum(e1, e2), jnp.maximum(e1, e2)
    wa, wb = jnp.where(e1 < e2, w1, w2), jnp.where(e1 < e2, w2, w1)
    pair = ea * (2 * EXPERTS_PER_GROUP - 1 - ea) * 0.5 + (eb - ea - 1.0)
    cls = gidx * float(len(PAIRS)) + pair
    onehot = jnp.where(lane == cls, 1.0, 0.0)
    before = jnp.dot(tri_ref[...], onehot.astype(BF16), preferred_element_type=F32) + running[...]
    rank = jnp.sum(onehot * before, axis=-1, keepdims=True)
    running[...] += jnp.sum(onehot, axis=0, keepdims=True)
    cnt_ref[...] = running[...]

    meta = (jnp.where(lane == META_CLASS, cls, 0.0) + jnp.where(lane == META_RANK, rank, 0.0)
            + jnp.where(lane == META_WA, wa, 0.0) + jnp.where(lane == META_WB, wb, 0.0))
    hx_ref[:, :d] = xn
    hx_ref[:, d:] = meta
    route_ref[...] = meta.T[:SUBLANES, :]


def _router(h2d, g, w_group, b_group, w_router, b_router, *, tm=512):
    t, d = h2d.shape
    ng, _, ne = w_router.shape
    assert (ng, ne) == (N_GROUPS, EXPERTS_PER_GROUP) and N_CLASSES <= ROUTER_LANES
    wr = jnp.concatenate([w_group, jnp.transpose(w_router, (1, 0, 2)).reshape(d, ng * ne)], axis=1)
    br = jnp.concatenate([b_group, b_router.reshape(ng * ne)])
    pad = ROUTER_LANES - wr.shape[1]
    wr = jnp.pad(wr, ((0, 0), (0, pad)))
    br = jnp.pad(br, (0, pad)).reshape(1, ROUTER_LANES)
    tri = jnp.asarray(np.tril(np.ones((tm, tm), np.float32), -1), BF16)
    return pl.pallas_call(
        functools.partial(_router_kernel, d=d),
        out_shape=[jax.ShapeDtypeStruct((t, d + ROUTER_LANES), F32), jax.ShapeDtypeStruct((SUBLANES, t), F32),
                   jax.ShapeDtypeStruct((1, ROUTER_LANES), F32)],
        grid=(t // tm,),
        in_specs=[pl.BlockSpec((tm, d), lambda i: (i, 0)), _resident((1, d)), _resident((d, ROUTER_LANES)),
                  _resident((1, ROUTER_LANES)), _resident((tm, tm))],
        out_specs=[pl.BlockSpec((tm, d + ROUTER_LANES), lambda i: (i, 0)),
                   pl.BlockSpec((SUBLANES, tm), lambda i: (0, i)),
                   pl.BlockSpec((1, ROUTER_LANES), lambda i: (0, 0))],
        scratch_shapes=[pltpu.VMEM((1, ROUTER_LANES), F32)],
        compiler_params=_params("arbitrary"),
        name="moe_router",
    )(h2d, g.reshape(1, d), wr, br, tri)


ITEM_VALID, ITEM_FIRST_OF_TILE, ITEM_NEW_A, ITEM_NEW_B = 1, 2, 4, 8


def _prefix_sum(v):
    ids = jnp.arange(v.shape[0], dtype=jnp.int32)
    return jnp.sum(jnp.where(ids[None, :] <= ids[:, None], v[None, :], 0), axis=1)


def _sorted_layout(route, counts, n_tiles, n_items):
    cls = route[META_CLASS].astype(jnp.int32)
    rank = route[META_RANK].astype(jnp.int32)
    cnt = counts[0, :N_CLASSES].astype(jnp.int32)
    ids = jnp.arange(N_CLASSES, dtype=jnp.int32)
    end = _prefix_sum(cnt)
    start = end - cnt
    pos = jnp.sum(jnp.where(cls[None, :] == ids[:, None], start[:, None], 0), axis=0) + rank

    first = start // EXPERT_TILE
    last = jnp.where(cnt > 0, (end - 1) // EXPERT_TILE, first - 1)
    items = last - first + 1
    item_end = _prefix_sum(items)
    k = jnp.arange(n_items, dtype=jnp.int32)
    item_cls = jnp.sum((item_end[None, :] <= k[:, None]).astype(jnp.int32), axis=1)
    valid = item_cls < N_CLASSES
    item_cls = jnp.where(valid, item_cls, jnp.max(jnp.where(cnt > 0, ids, 0)))
    sel = item_cls[:, None] == ids[None, :]
    pick = lambda v: jnp.sum(jnp.where(sel, v[None, :], 0), axis=1)
    tile = jnp.where(valid, pick(first) + k - pick(item_end - items), n_tiles - 1)
    lo = jnp.where(valid, jnp.clip(pick(start) - tile * EXPERT_TILE, 0, EXPERT_TILE), 0)
    hi = jnp.where(valid, jnp.clip(pick(end) - tile * EXPERT_TILE, 0, EXPERT_TILE), 0)

    grp, pair = item_cls // len(PAIRS), item_cls % len(PAIRS)
    ea = grp * EXPERTS_PER_GROUP + sum(jnp.where(pair == n, a, 0) for n, (a, _) in enumerate(PAIRS))
    eb = grp * EXPERTS_PER_GROUP + sum(jnp.where(pair == n, b, 0) for n, (_, b) in enumerate(PAIRS))
    prev = lambda v: jnp.concatenate([jnp.full((1,), -1, jnp.int32), v[:-1]])
    flags = (jnp.where(valid, ITEM_VALID, 0) + jnp.where(valid & (tile != prev(tile)), ITEM_FIRST_OF_TILE, 0)
             + jnp.where(ea != prev(ea), ITEM_NEW_A, 0) + jnp.where(eb != prev(eb), ITEM_NEW_B, 0))
    return pos, (tile, ea, eb, lo, hi, flags)


DISPATCH_CHUNK = 2048
ROW_DMA_UNROLL = 8


def _row_copy(src, dst, i, j, sem):
    return pltpu.make_async_copy(src.at[pl.ds(i, 1)], dst.at[pl.ds(j, 1)], sem)


def _dispatch_kernel(pos_ref, hx_ref, hxs_ref, sem):
    base = pl.program_id(0) * DISPATCH_CHUNK

    def issue(r, carry):
        _row_copy(hx_ref, hxs_ref, r, pos_ref[base + r], sem).start()
        return carry

    lax.fori_loop(0, DISPATCH_CHUNK, issue, 0, unroll=ROW_DMA_UNROLL)
    pltpu.make_async_copy(hx_ref, hxs_ref.at[pl.ds(0, DISPATCH_CHUNK)], sem).wait()


def _dispatch(hx, pos):
    t, w = hx.shape
    assert t % DISPATCH_CHUNK == 0
    return pl.pallas_call(
        _dispatch_kernel,
        out_shape=jax.ShapeDtypeStruct((t, w), F32),
        grid_spec=pltpu.PrefetchScalarGridSpec(
            num_scalar_prefetch=1,
            grid=(t // DISPATCH_CHUNK,),
            in_specs=[pl.BlockSpec((DISPATCH_CHUNK, w), lambda i, pos: (i, 0))],
            out_specs=pl.BlockSpec(memory_space=pl.ANY),
            scratch_shapes=[pltpu.SemaphoreType.DMA(())],
        ),
        compiler_params=_params("arbitrary"),
        name="moe_dispatch",
    )(pos, hx)


def _experts_kernel(tile_ref, ea_ref, eb_ref, lo_ref, hi_ref, flag_ref, x_ref, wga_ref, wua_ref, wda_ref,
                    wgb_ref, wub_ref, wdb_ref, y_ref, w1, w2, *, d, hid):
    del tile_ref, ea_ref, eb_ref
    k = pl.program_id(0)
    flags = flag_ref[k]

    @pl.when((flags & ITEM_NEW_A) != 0)
    def _():
        w1[:, 0:hid] = wga_ref[...].astype(BF16)
        w1[:, hid:2 * hid] = wua_ref[...].astype(BF16)
        w2[0:hid, :] = wda_ref[...].astype(BF16)

    @pl.when((flags & ITEM_NEW_B) != 0)
    def _():
        w1[:, 2 * hid:3 * hid] = wgb_ref[...].astype(BF16)
        w1[:, 3 * hid:] = wub_ref[...].astype(BF16)
        w2[hid:, :] = wdb_ref[...].astype(BF16)

    @pl.when((flags & ITEM_VALID) != 0)
    def _():
        row = lax.broadcasted_iota(jnp.int32, (EXPERT_TILE, 1), 0)
        inside = (row >= lo_ref[k]) & (row < hi_ref[k])
        wa = jnp.where(inside, x_ref[:, d + META_WA:d + META_WA + 1], 0.0)
        wb = jnp.where(inside, x_ref[:, d + META_WB:d + META_WB + 1], 0.0)
        gu = jnp.dot(x_ref[:, :d].astype(BF16), w1[...], preferred_element_type=F32)
        ga, gb = gu[:, 0:hid], gu[:, 2 * hid:3 * hid]
        ha = ga * _sigmoid(ga) * gu[:, hid:2 * hid] * wa
        hb = gb * _sigmoid(gb) * gu[:, 3 * hid:] * wb
        y = jnp.dot(jnp.concatenate([ha, hb], axis=1).astype(BF16), w2[...], preferred_element_type=F32)

        @pl.when((flags & ITEM_FIRST_OF_TILE) != 0)
        def _():
            y_ref[...] = y

        @pl.when((flags & ITEM_FIRST_OF_TILE) == 0)
        def _():
            y_ref[...] += y


def _experts(hxs, items, w_gate, w_up, w_down, layer, *, d):
    t, w = hxs.shape
    hid = w_gate.shape[-1]
    n_items = items[0].shape[0]
    up_a = pl.BlockSpec((None, None, d, hid), lambda k, tile, ea, eb, lo, hi, fl: (layer, ea[k], 0, 0))
    up_b = pl.BlockSpec((None, None, d, hid), lambda k, tile, ea, eb, lo, hi, fl: (layer, eb[k], 0, 0))
    down_a = pl.BlockSpec((None, None, hid, d), lambda k, tile, ea, eb, lo, hi, fl: (layer, ea[k], 0, 0))
    down_b = pl.BlockSpec((None, None, hid, d), lambda k, tile, ea, eb, lo, hi, fl: (layer, eb[k], 0, 0))
    return pl.pallas_call(
        functools.partial(_experts_kernel, d=d, hid=hid),
        out_shape=jax.ShapeDtypeStruct((t, d), F32),
        grid_spec=pltpu.PrefetchScalarGridSpec(
            num_scalar_prefetch=len(items),
            grid=(n_items,),
            in_specs=[pl.BlockSpec((EXPERT_TILE, w), lambda k, tile, ea, eb, lo, hi, fl: (tile[k], 0)),
                      up_a, up_a, down_a, up_b, up_b, down_b],
            out_specs=pl.BlockSpec((EXPERT_TILE, d), lambda k, tile, ea, eb, lo, hi, fl: (tile[k], 0)),
            scratch_shapes=[pltpu.VMEM((d, 4 * hid), BF16), pltpu.VMEM((2 * hid, d), BF16)],
        ),
        compiler_params=_params("arbitrary"),
        name="moe_experts",
    )(*items, hxs, w_gate, w_up, w_down, w_gate, w_up, w_down)


def _combine_ple_kernel(pos_ref, h_ref, p_ref, g_ref, wg_ref, wp_ref, ys_ref, o_ref, ybuf, sems, *, tm):
    i = pl.program_id(0)
    n = pl.num_programs(0)
    slot = i % 2

    def wait(s):
        pltpu.make_async_copy(ys_ref.at[pl.ds(0, tm)], ybuf.at[s], sems.at[s]).wait()

    @pl.when(i == 0)
    def _():
        def issue(r, carry):
            _row_copy(ys_ref, ybuf.at[0], pos_ref[r], r, sems.at[0]).start()
            return carry
        lax.fori_loop(0, tm, issue, 0, unroll=ROW_DMA_UNROLL)

    nxt = jnp.minimum(i + 1, n - 1) * tm
    for r in range(tm):
        _row_copy(ys_ref, ybuf.at[1 - slot], pos_ref[nxt + r], r, sems.at[1 - slot]).start()

    wait(slot)
    h = h_ref[...] + ybuf[slot]
    gate = _sigmoid(jnp.dot(_rms(h, g_ref[...]).astype(BF16), wg_ref[...], preferred_element_type=F32))
    proj = jnp.dot(p_ref[...].astype(BF16), wp_ref[...], preferred_element_type=F32)
    o_ref[...] = h + gate * proj

    @pl.when(i == n - 1)
    def _():
        wait(1 - slot)


def _combine_ple(h2d, ys, pos, p3d, layer, g, w_gate, w_proj, *, tm=512):
    t, d = h2d.shape
    pd = p3d.shape[-1]
    tile = pl.BlockSpec((tm, d), lambda i, pos: (i, 0))
    res = lambda shape: pl.BlockSpec(shape, lambda i, pos: (0,) * len(shape), pipeline_mode=pl.Buffered(1))
    return pl.pallas_call(
        functools.partial(_combine_ple_kernel, tm=tm),
        out_shape=jax.ShapeDtypeStruct((t, d), F32),
        grid_spec=pltpu.PrefetchScalarGridSpec(
            num_scalar_prefetch=1,
            grid=(t // tm,),
            in_specs=[tile, pl.BlockSpec((None, tm, pd), lambda i, pos: (layer, i, 0)), res((1, d)), res((d, d)),
                      res((pd, d)), pl.BlockSpec(memory_space=pl.ANY)],
            out_specs=tile,
            scratch_shapes=[pltpu.VMEM((2, tm, d), F32), pltpu.SemaphoreType.DMA((2,))],
        ),
        compiler_params=_params("arbitrary"),
        name="moe_combine_ple",
    )(pos, h2d, p3d, g.reshape(1, d), w_gate.astype(BF16), w_proj.astype(BF16), ys)


def _moe_ple(h2d, p3d, layer, g_ffn, w_group, b_group, w_router, b_router, w_gate, w_up, w_down, g_ple,
             ple_gate, ple_proj):
    t, d = h2d.shape
    assert t % EXPERT_TILE == 0
    n_tiles = t // EXPERT_TILE
    n_items = n_tiles + N_CLASSES - 1
    hx, route, counts = _router(h2d, g_ffn, w_group, b_group, w_router, b_router)
    pos, items = _sorted_layout(route, counts, n_tiles, n_items)
    hxs = _dispatch(hx, pos)
    ys = _experts(hxs, items, w_gate, w_up, w_down, layer, d=d)
    return _combine_ple(h2d, ys, pos, p3d, layer, g_ple, ple_gate, ple_proj)


def kernel(x, p, g_mix, g_ffn, g_ple, conv_w_pw1, conv_b_pw1, conv_w_dw, conv_b_dw, conv_ln_g, conv_ln_b, conv_w_pw2, conv_b_pw2, attn_w_qkv, attn_q_gain, attn_k_gain, attn_w_o, moe_w_group, moe_b_group, moe_w_router, moe_b_router, moe_w_gate, moe_w_up, moe_w_down, ple_w_gate, ple_w_proj):
    b, s, d = x.shape
    depth = g_mix.shape[0]
    t = b * s
    h = x
    for i in range(depth):
        if i % 2 == 0:
            c = i // 2
            h = _conv_mixer(h.reshape(b, s, d), g_mix[i], conv_w_pw1[c], conv_b_pw1[c], conv_w_dw[c],
                            conv_b_dw[c], conv_ln_g[c], conv_ln_b[c], conv_w_pw2[c], conv_b_pw2[c])
            h = h.reshape(t, d)
        else:
            a = i // 2
            h = h.reshape(t, d)
            q, k, v = _qkv(h, g_mix[i], attn_w_qkv[a], attn_q_gain[a], attn_k_gain[a])
            o = _moba(q.reshape(b, s, d), k.reshape(b, s, d), v.reshape(b, s, d), n_heads=N_HEADS)
            h = _matmul_residual(h, o.reshape(t, d), attn_w_o[a])
        h = _moe_ple(h, p.reshape(depth, t, -1), i, g_ffn[i], moe_w_group[i], moe_b_group[i], moe_w_router[i],
                     moe_b_router[i], moe_w_gate, moe_w_up, moe_w_down, g_ple[i], ple_w_gate[i], ple_w_proj[i])
    return h.reshape(b, s, d)
```

```python
import functools

import jax
import jax.numpy as jnp
import numpy as np
from jax import lax
from jax.experimental import pallas as pl
from jax.experimental.pallas import tpu as pltpu

F32 = jnp.float32
BF16 = jnp.bfloat16

EPS = 1e-6
NEG_INF = -1e30

N_HEADS = 8
MOBA_BLOCK = 256
MOBA_TOPK = 3
N_GROUPS = 4
EXPERTS_PER_GROUP = 4
N_EXPERTS = N_GROUPS * EXPERTS_PER_GROUP

V7X_VMEM_BYTES = 64 * 1024 * 1024
LANES = 128
SUBLANES = 8
VMEM_LIMIT = V7X_VMEM_BYTES * 7 // 8

ROUTER_LANES = LANES
EXPERT_LANE0 = N_GROUPS

CONV_HALO = 32
CONV_ROWS = 64
CONV_COLS = 128
CONV_PANEL = 256


def _params(*semantics):
    return pltpu.CompilerParams(dimension_semantics=semantics, vmem_limit_bytes=VMEM_LIMIT)


def _resident(shape):
    nd = len(shape)
    return pl.BlockSpec(shape, lambda *_: (0,) * nd, pipeline_mode=pl.Buffered(1))


def _rms(x, g):
    return x * lax.rsqrt(jnp.mean(x * x, axis=-1, keepdims=True) + EPS) * g


def _sigmoid(x):
    return 1.0 / (1.0 + jnp.exp(-x))


def _conv_mixer_kernel(x_ref, g_ref, w1_ref, b1_ref, wdw_ref, bdw_ref, lng_ref, lnb_ref,
                       w2_ref, b2_ref, o_ref, xnbuf, ubuf, cbuf, *, ts, d, kw):
    s = pl.program_id(1)
    x = x_ref[0]
    xnbuf[...] = _rms(x, g_ref[...]).astype(BF16)

    @pl.when(s == 0)
    def _():
        ubuf[0:CONV_HALO, :] = jnp.zeros((CONV_HALO, d), F32)

    def glu_panel(c0):
        a = jnp.dot(xnbuf[...], w1_ref[:, c0:c0 + CONV_PANEL], preferred_element_type=F32)
        a = a + b1_ref[:, c0:c0 + CONV_PANEL]
        g = jnp.dot(xnbuf[...], w1_ref[:, d + c0:d + c0 + CONV_PANEL], preferred_element_type=F32)
        g = g + b1_ref[:, d + c0:d + c0 + CONV_PANEL]
        ubuf[CONV_HALO:CONV_HALO + ts, c0:c0 + CONV_PANEL] = a * _sigmoid(g)

    first = CONV_HALO - (kw - 1)

    def conv_panel(c0):
        for cc in range(c0, c0 + CONV_PANEL, CONV_COLS):
            cols = slice(cc, cc + CONV_COLS)
            for r0 in range(0, ts, CONV_ROWS):
                acc = None
                for j in range(SUBLANES):
                    rows = CONV_ROWS + (SUBLANES if j else 0)
                    sj = None
                    for k in range(kw):
                        if (first + k) % SUBLANES == j:
                            a0 = r0 + (first + k) - j
                            term = ubuf[a0:a0 + rows, cols] * wdw_ref[k:k + 1, cols]
                            sj = term if sj is None else sj + term
                    if sj is not None:
                        sj = sj[j:j + CONV_ROWS]
                        acc = sj if acc is None else acc + sj
                cbuf[r0:r0 + CONV_ROWS, cols] = acc
        ubuf[0:CONV_HALO, c0:c0 + CONV_PANEL] = ubuf[ts:ts + CONV_HALO, c0:c0 + CONV_PANEL]

    glu_panel(0)
    for c0 in range(0, d, CONV_PANEL):
        if c0 + CONV_PANEL < d:
            glu_panel(c0 + CONV_PANEL)
        conv_panel(c0)

    c = cbuf[...] + bdw_ref[...]
    mu = jnp.mean(c, axis=-1, keepdims=True)
    xc = c - mu
    y = xc * lax.rsqrt(jnp.mean(xc * xc, axis=-1, keepdims=True) + EPS) * lng_ref[...] + lnb_ref[...]
    y = y * _sigmoid(y)
    o_ref[0] = x + jnp.dot(y.astype(BF16), w2_ref[...], preferred_element_type=F32) + b2_ref[...]


def _conv_mixer(h, g, w1, b1, wdw, bdw, lng, lnb, w2, b2, *, ts=512):
    b, s, d = h.shape
    kw = wdw.shape[0]
    assert s % ts == 0 and ts % CONV_ROWS == 0 and ts >= CONV_HALO >= kw - 1
    assert CONV_HALO % SUBLANES == 0 and d % CONV_PANEL == 0 and CONV_PANEL % CONV_COLS == 0
    row = lambda a: a.reshape(1, -1)
    kern = functools.partial(_conv_mixer_kernel, ts=ts, d=d, kw=kw)
    return pl.pallas_call(
        kern,
        out_shape=jax.ShapeDtypeStruct((b, s, d), F32),
        grid=(b, s // ts),
        in_specs=[
            pl.BlockSpec((1, ts, d), lambda i, j: (i, j, 0)),
            _resident((1, d)), _resident((d, 2 * d)), _resident((1, 2 * d)),
            _resident((kw, d)), _resident((1, d)), _resident((1, d)), _resident((1, d)),
            _resident((d, d)), _resident((1, d)),
        ],
        out_specs=pl.BlockSpec((1, ts, d), lambda i, j: (i, j, 0)),
        scratch_shapes=[pltpu.VMEM((ts, d), BF16), pltpu.VMEM((ts + CONV_HALO, d), F32), pltpu.VMEM((ts, d), F32)],
        compiler_params=_params("arbitrary", "arbitrary"),
        name="conv_mixer",
    )(h, row(g), w1.astype(BF16), row(b1), wdw, row(bdw), row(lng), row(lnb), w2.astype(BF16), row(b2))


def _qkv_kernel(x_ref, g_ref, w_ref, qg_ref, kg_ref, q_ref, k_ref, v_ref, *, d, hd):
    xn = _rms(x_ref[...], g_ref[...]).astype(BF16)
    qkv = jnp.dot(xn, w_ref[...], preferred_element_type=F32)
    for h in range(d // hd):
        cols = slice(h * hd, (h + 1) * hd)
        q_ref[:, cols] = _rms(qkv[:, h * hd:(h + 1) * hd], qg_ref[...]).astype(BF16)
        k_ref[:, cols] = _rms(qkv[:, d + h * hd:d + (h + 1) * hd], kg_ref[...]).astype(BF16)
    v_ref[...] = qkv[:, 2 * d:].astype(BF16)


def _qkv(h2d, g, w, qg, kg, *, tm=512):
    t, d = h2d.shape
    hd = qg.shape[0]
    kern = functools.partial(_qkv_kernel, d=d, hd=hd)
    tile = pl.BlockSpec((tm, d), lambda i: (i, 0))
    return pl.pallas_call(
        kern,
        out_shape=[jax.ShapeDtypeStruct((t, d), BF16)] * 3,
        grid=(t // tm,),
        in_specs=[tile, _resident((1, d)), _resident((d, 3 * d)), _resident((1, hd)), _resident((1, hd))],
        out_specs=[tile, tile, tile],
        compiler_params=_params("arbitrary"),
        name="attn_qkv",
    )(h2d, g.reshape(1, d), w.astype(BF16), qg.reshape(1, hd), kg.reshape(1, hd))


def _moba_kernel(slope_ref, q_ref, k_ref, v_ref, kaux_ref, vaux_ref, o_ref, kaug, vaug, zbuf, pbuf,
                 *, nb, blk, hd, topk):
    scale = hd ** -0.5
    negz = NEG_INF / scale
    slope = slope_ref[0][:, :1]
    kaug[:, :hd] = k_ref[0]
    kaug[:, hd:] = kaux_ref[...]
    vaug[:, :hd] = v_ref[0]
    vaug[:, hd:] = vaux_ref[...]

    km = jnp.concatenate([jnp.mean(k_ref[0, n * blk:(n + 1) * blk, :].astype(F32), axis=0, keepdims=True)
                          for n in range(nb)], axis=0)
    hi = km.astype(BF16).astype(F32)
    mid = (km - hi).astype(BF16).astype(F32)
    lo = (km - hi - mid).astype(BF16).astype(F32)
    pad = [jnp.zeros((-3 * nb % (2 * SUBLANES), hd), F32)] if 3 * nb % (2 * SUBLANES) else []
    km3 = jnp.concatenate([hi, mid, lo] + pad, axis=0).astype(BF16)

    r = lax.broadcasted_iota(jnp.int32, (blk, blk), 0)
    c = lax.broadcasted_iota(jnp.int32, (blk, blk), 1)
    causal = r >= c
    keypos = lax.broadcasted_iota(jnp.int32, (1, blk), 1).astype(F32)
    nt = (((1,), (1,)), ((), ()))

    for j in range(nb):
        q = q_ref[0, j * blk:(j + 1) * blk, :]
        select = j > topk
        if select:
            g3 = lax.dot_general(km3, q, nt, preferred_element_type=F32)
            gate = g3[0:nb] + g3[nb:2 * nb] + g3[2 * nb:3 * nb]
            g = [gate[m:m + 1, :] for m in range(j)]
            rows = []
            for n in range(j):
                beaten = jnp.zeros((1, blk), F32)
                for m in range(j):
                    if m != n:
                        ahead = (g[m] > g[n]) | (g[m] == g[n]) if m < n else (g[m] > g[n])
                        beaten = beaten + jnp.where(ahead, 1.0, 0.0)
                rows.append(jnp.where(beaten < topk, 0.0, negz))
            bias_t = jnp.concatenate(rows + [jnp.zeros((LANES - j, blk), F32)], axis=0)
            q = jnp.concatenate([q, bias_t.T.astype(BF16)], axis=1)

        mrun = None
        for n in range(j + 1):
            keys = kaug[n * blk:(n + 1) * blk, :] if select else k_ref[0, n * blk:(n + 1) * blk, :]
            z = lax.dot_general(q, keys, nt, preferred_element_type=F32)
            u = z * scale + slope * (keypos + float(n * blk))
            if n == j:
                u = jnp.where(causal, u, NEG_INF)
            zbuf[:, n * blk:(n + 1) * blk] = u
            for half in range(blk // LANES):
                f = u[:, half * LANES:(half + 1) * LANES]
                mrun = f if mrun is None else jnp.maximum(mrun, f)
        mb = jnp.broadcast_to(jnp.max(mrun, axis=-1, keepdims=True), (blk, LANES))
        for n in range(j + 1):
            for half in range(blk // LANES):
                cols = slice(n * blk + half * LANES, n * blk + (half + 1) * LANES)
                pbuf[:, cols] = jnp.exp(zbuf[:, cols] - mb).astype(BF16)
        acc = jnp.dot(pbuf[:, :(j + 1) * blk], vaug[:(j + 1) * blk, :], preferred_element_type=F32)
        o_ref[0, j * blk:(j + 1) * blk, :] = (acc[:, :hd] / acc[:, hd:hd + 1]).astype(BF16)


def _moba(q, k, v, *, n_heads):
    b, s, d = q.shape
    hd = d // n_heads
    blk = MOBA_BLOCK
    assert s % blk == 0 and hd == LANES and blk % LANES == 0
    nb = s // blk
    assert nb <= LANES
    topk = max(1, min(MOBA_TOPK, nb - 1))
    slopes = jnp.exp2(-8.0 * jnp.arange(1, n_heads + 1, dtype=F32) / n_heads)
    slopes = jnp.broadcast_to(slopes[:, None, None], (n_heads, 1, LANES))
    kaux = np.zeros((s, LANES), np.float32)
    kaux[np.arange(s), np.arange(s) // blk] = 1.0
    vaux = np.zeros((s, LANES), np.float32)
    vaux[:, 0] = 1.0
    kern = functools.partial(_moba_kernel, nb=nb, blk=blk, hd=hd, topk=topk)
    head = pl.BlockSpec((1, s, hd), lambda i, h: (i, 0, h))
    return pl.pallas_call(
        kern,
        out_shape=jax.ShapeDtypeStruct((b, s, d), BF16),
        grid=(b, n_heads),
        in_specs=[pl.BlockSpec((1, 1, LANES), lambda i, h: (h, 0, 0)), head, head, head,
                  _resident((s, LANES)), _resident((s, LANES))],
        out_specs=head,
        scratch_shapes=[pltpu.VMEM((s, hd + LANES), BF16), pltpu.VMEM((s, hd + LANES), BF16),
                        pltpu.VMEM((blk, s), F32), pltpu.VMEM((blk, s), BF16)],
        compiler_params=_params("arbitrary", "arbitrary"),
        name="moba_attn",
    )(slopes, q, k, v, jnp.asarray(kaux, BF16), jnp.asarray(vaux, BF16))


def _matmul_residual_kernel(h_ref, a_ref, w_ref, o_ref):
    o_ref[...] = h_ref[...] + jnp.dot(a_ref[...], w_ref[...], preferred_element_type=F32)


def _matmul_residual(h2d, a, w, *, tm=512):
    t, d = h2d.shape
    tile = pl.BlockSpec((tm, d), lambda i: (i, 0))
    return pl.pallas_call(
        _matmul_residual_kernel,
        out_shape=jax.ShapeDtypeStruct((t, d), F32),
        grid=(t // tm,),
        in_specs=[tile, pl.BlockSpec((tm, a.shape[1]), lambda i: (i, 0)), _resident(w.shape)],
        out_specs=tile,
        compiler_params=_params("arbitrary"),
        name="attn_out_proj",
    )(h2d, a, w.astype(BF16))


PAIRS = [(a, b) for a in range(EXPERTS_PER_GROUP) for b in range(a + 1, EXPERTS_PER_GROUP)]
N_CLASSES = N_GROUPS * len(PAIRS)
META_CLASS, META_RANK, META_WA, META_WB = 0, 1, 2, 3
EXPERT_TILE = 256


def _router_kernel(h_ref, g_ref, wr_ref, br_ref, tri_ref, hx_ref, route_ref, cnt_ref, running, *, d):
    @pl.when(pl.program_id(0) == 0)
    def _():
        running[...] = jnp.zeros_like(running)

    xn = _rms(h_ref[...], g_ref[...])
    wr = wr_ref[...]
    xh, wh = xn.astype(BF16), wr.astype(BF16)
    xl, wl = (xn - xh.astype(F32)).astype(BF16), (wr - wh.astype(F32)).astype(BF16)
    logits = (jnp.dot(xh, wh, preferred_element_type=F32) + jnp.dot(xl, wh, preferred_element_type=F32)
              + jnp.dot(xh, wl, preferred_element_type=F32)) + br_ref[...]
    lane = lax.broadcasted_iota(jnp.int32, logits.shape, 1).astype(F32)

    def first_max(vals):
        mx = jnp.max(vals, axis=-1, keepdims=True)
        return mx, jnp.min(jnp.where(vals == mx, lane, float(ROUTER_LANES)), axis=-1, keepdims=True)

    is_g = lane < N_GROUPS
    gmax, gidx = first_max(jnp.where(is_g, logits, -jnp.inf))
    zg = jnp.sum(jnp.where(is_g, jnp.exp(logits - gmax), 0.0), axis=-1, keepdims=True)
    g_w = 1.0 / zg
    lo = EXPERT_LANE0 + gidx * EXPERTS_PER_GROUP
    in_grp = (lane >= lo) & (lane < lo + EXPERTS_PER_GROUP)
    el = jnp.where(in_grp, logits, -jnp.inf)
    m1, i1 = first_max(el)
    m2, i2 = first_max(jnp.where(lane == i1, -jnp.inf, el))
    p2 = jnp.exp(m2 - m1)
    w1 = g_w / (1.0 + p2)
    w2 = g_w * p2 / (1.0 + p2)
    e1, e2 = i1 - lo, i2 - lo
    ea, eb = jnp.minimum(e1, e2), jnp.maximum(e1, e2)
    wa, wb = jnp.where(e1 < e2, w1, w2), jnp.where(e1 < e2, w2, w1)
    pair = ea * (2 * EXPERTS_PER_GROUP - 1 - ea) * 0.5 + (eb - ea - 1.0)
    cls = gidx * float(len(PAIRS)) + pair
    onehot = jnp.where(lane == cls, 1.0, 0.0)
    before = jnp.dot(tri_ref[...], onehot.astype(BF16), preferred_element_type=F32) + running[...]
    rank = jnp.sum(onehot * before, axis=-1, keepdims=True)
    running[...] += jnp.sum(onehot, axis=0, keepdims=True)
    cnt_ref[...] = running[...]

    meta = (jnp.where(lane == META_CLASS, cls, 0.0) + jnp.where(lane == META_RANK, rank, 0.0)
            + jnp.where(lane == META_WA, wa, 0.0) + jnp.where(lane == META_WB, wb, 0.0))
    hx_ref[:, :d] = xn
    hx_ref[:, d:] = meta
    route_ref[...] = meta.T[:SUBLANES, :]


def _router(h2d, g, w_group, b_group, w_router, b_router, *, tm=512):
    t, d = h2d.shape
    ng, _, ne = w_router.shape
    assert (ng, ne) == (N_GROUPS, EXPERTS_PER_GROUP) and N_CLASSES <= ROUTER_LANES
    wr = jnp.concatenate([w_group, jnp.transpose(w_router, (1, 0, 2)).reshape(d, ng * ne)], axis=1)
    br = jnp.concatenate([b_group, b_router.reshape(ng * ne)])
    pad = ROUTER_LANES - wr.shape[1]
    wr = jnp.pad(wr, ((0, 0), (0, pad)))
    br = jnp.pad(br, (0, pad)).reshape(1, ROUTER_LANES)
    tri = jnp.asarray(np.tril(np.ones((tm, tm), np.float32), -1), BF16)
    return pl.pallas_call(
        functools.partial(_router_kernel, d=d),
        out_shape=[jax.ShapeDtypeStruct((t, d + ROUTER_LANES), F32), jax.ShapeDtypeStruct((SUBLANES, t), F32),
                   jax.ShapeDtypeStruct((1, ROUTER_LANES), F32)],
        grid=(t // tm,),
        in_specs=[pl.BlockSpec((tm, d), lambda i: (i, 0)), _resident((1, d)), _resident((d, ROUTER_LANES)),
                  _resident((1, ROUTER_LANES)), _resident((tm, tm))],
        out_specs=[pl.BlockSpec((tm, d + ROUTER_LANES), lambda i: (i, 0)),
                   pl.BlockSpec((SUBLANES, tm), lambda i: (0, i)),
                   pl.BlockSpec((1, ROUTER_LANES), lambda i: (0, 0))],
        scratch_shapes=[pltpu.VMEM((1, ROUTER_LANES), F32)],
        compiler_params=_params("arbitrary"),
        name="moe_router",
    )(h2d, g.reshape(1, d), wr, br, tri)


ITEM_VALID, ITEM_FIRST_OF_TILE, ITEM_NEW_A, ITEM_NEW_B = 1, 2, 4, 8


def _prefix_sum(v):
    ids = jnp.arange(v.shape[0], dtype=jnp.int32)
    return jnp.sum(jnp.where(ids[None, :] <= ids[:, None], v[None, :], 0), axis=1)


def _sorted_layout(route, counts, n_tiles, n_items):
    cls = route[META_CLASS].astype(jnp.int32)
    rank = route[META_RANK].astype(jnp.int32)
    cnt = counts[0, :N_CLASSES].astype(jnp.int32)
    ids = jnp.arange(N_CLASSES, dtype=jnp.int32)
    end = _prefix_sum(cnt)
    start = end - cnt
    pos = jnp.sum(jnp.where(cls[None, :] == ids[:, None], start[:, None], 0), axis=0) + rank

    first = start // EXPERT_TILE
    last = jnp.where(cnt > 0, (end - 1) // EXPERT_TILE, first - 1)
    items = last - first + 1
    item_end = _prefix_sum(items)
    k = jnp.arange(n_items, dtype=jnp.int32)
    item_cls = jnp.sum((item_end[None, :] <= k[:, None]).astype(jnp.int32), axis=1)
    valid = item_cls < N_CLASSES
    item_cls = jnp.where(valid, item_cls, jnp.max(jnp.where(cnt > 0, ids, 0)))
    sel = item_cls[:, None] == ids[None, :]
    pick = lambda v: jnp.sum(jnp.where(sel, v[None, :], 0), axis=1)
    tile = jnp.where(valid, pick(first) + k - pick(item_end - items), n_tiles - 1)
    lo = jnp.where(valid, jnp.clip(pick(start) - tile * EXPERT_TILE, 0, EXPERT_TILE), 0)
    hi = jnp.where(valid, jnp.clip(pick(end) - tile * EXPERT_TILE, 0, EXPERT_TILE), 0)

    grp, pair = item_cls // len(PAIRS), item_cls % len(PAIRS)
    ea = grp * EXPERTS_PER_GROUP + sum(jnp.where(pair == n, a, 0) for n, (a, _) in enumerate(PAIRS))
    eb = grp * EXPERTS_PER_GROUP + sum(jnp.where(pair == n, b, 0) for n, (_, b) in enumerate(PAIRS))
    prev = lambda v: jnp.concatenate([jnp.full((1,), -1, jnp.int32), v[:-1]])
    flags = (jnp.where(valid, ITEM_VALID, 0) + jnp.where(valid & (tile != prev(tile)), ITEM_FIRST_OF_TILE, 0)
             + jnp.where(ea != prev(ea), ITEM_NEW_A, 0) + jnp.where(eb != prev(eb), ITEM_NEW_B, 0))
    return pos, (tile, ea, eb, lo, hi, flags)


DISPATCH_CHUNK = 2048
ROW_DMA_UNROLL = 8


def _row_copy(src, dst, i, j, sem):
    return pltpu.make_async_copy(src.at[pl.ds(i, 1)], dst.at[pl.ds(j, 1)], sem)


def _dispatch_kernel(pos_ref, hx_ref, hxs_ref, sem):
    base = pl.program_id(0) * DISPATCH_CHUNK

    def issue(r, carry):
        _row_copy(hx_ref, hxs_ref, r, pos_ref[base + r], sem).start()
        return carry

    lax.fori_loop(0, DISPATCH_CHUNK, issue, 0, unroll=ROW_DMA_UNROLL)
    pltpu.make_async_copy(hx_ref, hxs_ref.at[pl.ds(0, DISPATCH_CHUNK)], sem).wait()


def _dispatch(hx, pos):
    t, w = hx.shape
    assert t % DISPATCH_CHUNK == 0
    return pl.pallas_call(
        _dispatch_kernel,
        out_shape=jax.ShapeDtypeStruct((t, w), F32),
        grid_spec=pltpu.PrefetchScalarGridSpec(
            num_scalar_prefetch=1,
            grid=(t // DISPATCH_CHUNK,),
            in_specs=[pl.BlockSpec((DISPATCH_CHUNK, w), lambda i, pos: (i, 0))],
            out_specs=pl.BlockSpec(memory_space=pl.ANY),
            scratch_shapes=[pltpu.SemaphoreType.DMA(())],
        ),
        compiler_params=_params("arbitrary"),
        name="moe_dispatch",
    )(pos, hx)


def _experts_kernel(tile_ref, ea_ref, eb_ref, lo_ref, hi_ref, flag_ref, x_ref, wga_ref, wua_ref, wda_ref,
                    wgb_ref, wub_ref, wdb_ref, y_ref, w1, w2, *, d, hid):
    del tile_ref, ea_ref, eb_ref
    k = pl.program_id(0)
    flags = flag_ref[k]

    @pl.when((flags & ITEM_NEW_A) != 0)
    def _():
        w1[:, 0:hid] = wga_ref[...].astype(BF16)
        w1[:, hid:2 * hid] = wua_ref[...].astype(BF16)
        w2[0:hid, :] = wda_ref[...].astype(BF16)

    @pl.when((flags & ITEM_NEW_B) != 0)
    def _():
        w1[:, 2 * hid:3 * hid] = wgb_ref[...].astype(BF16)
        w1[:, 3 * hid:] = wub_ref[...].astype(BF16)
        w2[hid:, :] = wdb_ref[...].astype(BF16)

    @pl.when((flags & ITEM_VALID) != 0)
    def _():
        row = lax.broadcasted_iota(jnp.int32, (EXPERT_TILE, 1), 0)
        inside = (row >= lo_ref[k]) & (row < hi_ref[k])
        wa = jnp.where(inside, x_ref[:, d + META_WA:d + META_WA + 1], 0.0)
        wb = jnp.where(inside, x_ref[:, d + META_WB:d + META_WB + 1], 0.0)
        gu = jnp.dot(x_ref[:, :d].astype(BF16), w1[...], preferred_element_type=F32)
        ga, gb = gu[:, 0:hid], gu[:, 2 * hid:3 * hid]
        ha = ga * _sigmoid(ga) * gu[:, hid:2 * hid] * wa
        hb = gb * _sigmoid(gb) * gu[:, 3 * hid:] * wb
        y = jnp.dot(jnp.concatenate([ha, hb], axis=1).astype(BF16), w2[...], preferred_element_type=F32)

        @pl.when((flags & ITEM_FIRST_OF_TILE) != 0)
        def _():
            y_ref[...] = y

        @pl.when((flags & ITEM_FIRST_OF_TILE) == 0)
        def _():
            y_ref[...] += y


def _experts(hxs, items, w_gate, w_up, w_down, layer, *, d):
    t, w = hxs.shape
    hid = w_gate.shape[-1]
    n_items = items[0].shape[0]
    up_a = pl.BlockSpec((None, None, d, hid), lambda k, tile, ea, eb, lo, hi, fl: (layer, ea[k], 0, 0))
    up_b = pl.BlockSpec((None, None, d, hid), lambda k, tile, ea, eb, lo, hi, fl: (layer, eb[k], 0, 0))
    down_a = pl.BlockSpec((None, None, hid, d), lambda k, tile, ea, eb, lo, hi, fl: (layer, ea[k], 0, 0))
    down_b = pl.BlockSpec((None, None, hid, d), lambda k, tile, ea, eb, lo, hi, fl: (layer, eb[k], 0, 0))
    return pl.pallas_call(
        functools.partial(_experts_kernel, d=d, hid=hid),
        out_shape=jax.ShapeDtypeStruct((t, d), F32),
        grid_spec=pltpu.PrefetchScalarGridSpec(
            num_scalar_prefetch=len(items),
            grid=(n_items,),
            in_specs=[pl.BlockSpec((EXPERT_TILE, w), lambda k, tile, ea, eb, lo, hi, fl: (tile[k], 0)),
                      up_a, up_a, down_a, up_b, up_b, down_b],
            out_specs=pl.BlockSpec((EXPERT_TILE, d), lambda k, tile, ea, eb, lo, hi, fl: (tile[k], 0)),
            scratch_shapes=[pltpu.VMEM((d, 4 * hid), BF16), pltpu.VMEM((2 * hid, d), BF16)],
        ),
        compiler_params=_params("arbitrary"),
        name="moe_experts",
    )(*items, hxs, w_gate, w_up, w_down, w_gate, w_up, w_down)


def _combine_ple_kernel(pos_ref, h_ref, p_ref, g_ref, wg_ref, wp_ref, ys_ref, o_ref, ybuf, sems, *, tm):
    i = pl.program_id(0)
    n = pl.num_programs(0)
    slot = i % 2

    def wait(s):
        pltpu.make_async_copy(ys_ref.at[pl.ds(0, tm)], ybuf.at[s], sems.at[s]).wait()

    @pl.when(i == 0)
    def _():
        def issue(r, carry):
            _row_copy(ys_ref, ybuf.at[0], pos_ref[r], r, sems.at[0]).start()
            return carry
        lax.fori_loop(0, tm, issue, 0, unroll=ROW_DMA_UNROLL)

    nxt = jnp.minimum(i + 1, n - 1) * tm
    for r in range(tm):
        _row_copy(ys_ref, ybuf.at[1 - slot], pos_ref[nxt + r], r, sems.at[1 - slot]).start()

    wait(slot)
    h = h_ref[...] + ybuf[slot]
    gate = _sigmoid(jnp.dot(_rms(h, g_ref[...]).astype(BF16), wg_ref[...], preferred_element_type=F32))
    proj = jnp.dot(p_ref[...].astype(BF16), wp_ref[...], preferred_element_type=F32)
    o_ref[...] = h + gate * proj

    @pl.when(i == n - 1)
    def _():
        wait(1 - slot)


def _combine_ple(h2d, ys, pos, p3d, layer, g, w_gate, w_proj, *, tm=512):
    t, d = h2d.shape
    pd = p3d.shape[-1]
    tile = pl.BlockSpec((tm, d), lambda i, pos: (i, 0))
    res = lambda shape: pl.BlockSpec(shape, lambda i, pos: (0,) * len(shape), pipeline_mode=pl.Buffered(1))
    return pl.pallas_call(
        functools.partial(_combine_ple_kernel, tm=tm),
        out_shape=jax.ShapeDtypeStruct((t, d), F32),
        grid_spec=pltpu.PrefetchScalarGridSpec(
            num_scalar_prefetch=1,
            grid=(t // tm,),
            in_specs=[tile, pl.BlockSpec((None, tm, pd), lambda i, pos: (layer, i, 0)), res((1, d)), res((d, d)),
                      res((pd, d)), pl.BlockSpec(memory_space=pl.ANY)],
            out_specs=tile,
            scratch_shapes=[pltpu.VMEM((2, tm, d), F32), pltpu.SemaphoreType.DMA((2,))],
        ),
        compiler_params=_params("arbitrary"),
        name="moe_combine_ple",
    )(pos, h2d, p3d, g.reshape(1, d), w_gate.astype(BF16), w_proj.astype(BF16), ys)


def _moe_ple(h2d, p3d, layer, g_ffn, w_group, b_group, w_router, b_router, w_gate, w_up, w_down, g_ple,
             ple_gate, ple_proj):
    t, d = h2d.shape
    assert t % EXPERT_TILE == 0
    n_tiles = t // EXPERT_TILE
    n_items = n_tiles + N_CLASSES - 1
    hx, route, counts = _router(h2d, g_ffn, w_group, b_group, w_router, b_router)
    pos, items = _sorted_layout(route, counts, n_tiles, n_items)
    hxs = _dispatch(hx, pos)
    ys = _experts(hxs, items, w_gate, w_up, w_down, layer, d=d)
    return _combine_ple(h2d, ys, pos, p3d, layer, g_ple, ple_gate, ple_proj)


def kernel(x, p, g_mix, g_ffn, g_ple, conv_w_pw1, conv_b_pw1, conv_w_dw, conv_b_dw, conv_ln_g, conv_ln_b, conv_w_pw2, conv_b_pw2, attn_w_qkv, attn_q_gain, attn_k_gain, attn_w_o, moe_w_group, moe_b_group, moe_w_router, moe_b_router, moe_w_gate, moe_w_up, moe_w_down, ple_w_gate, ple_w_proj):
    b, s, d = x.shape
    depth = g_mix.shape[0]
    t = b * s
    h = x
    for i in range(depth):
        if i % 2 == 0:
            c = i // 2
            h = _conv_mixer(h.reshape(b, s, d), g_mix[i], conv_w_pw1[c], conv_b_pw1[c], conv_w_dw[c],
                            conv_b_dw[c], conv_ln_g[c], conv_ln_b[c], conv_w_pw2[c], conv_b_pw2[c])
            h = h.reshape(t, d)
        else:
            a = i // 2
            h = h.reshape(t, d)
            q, k, v = _qkv(h, g_mix[i], attn_w_qkv[a], attn_q_gain[a], attn_k_gain[a])
            o = _moba(q.reshape(b, s, d), k.reshape(b, s, d), v.reshape(b, s, d), n_heads=N_HEADS)
            h = _matmul_residual(h, o.reshape(t, d), attn_w_o[a])
        h = _moe_ple(h, p.reshape(depth, t, -1), i, g_ffn[i], moe_w_group[i], moe_b_group[i], moe_w_router[i],
                     moe_b_router[i], moe_w_gate, moe_w_up, moe_w_down, g_ple[i], ple_w_gate[i], ple_w_proj[i])
    return h.reshape(b, s, d)
```

```python
import functools

import jax
import jax.numpy as jnp
import numpy as np
from jax import lax
from jax.experimental import pallas as pl
from jax.experimental.pallas import tpu as pltpu

F32 = jnp.float32
BF16 = jnp.bfloat16

EPS = 1e-6
NEG_INF = -1e30

N_HEADS = 8
MOBA_BLOCK = 256
MOBA_TOPK = 3
N_GROUPS = 4
EXPERTS_PER_GROUP = 4
N_EXPERTS = N_GROUPS * EXPERTS_PER_GROUP

V7X_VMEM_BYTES = 64 * 1024 * 1024
LANES = 128
SUBLANES = 8
VMEM_LIMIT = V7X_VMEM_BYTES * 7 // 8

ROUTER_LANES = LANES
EXPERT_LANE0 = N_GROUPS

CONV_HALO = 32
CONV_ROWS = 64
CONV_COLS = 128
CONV_PANEL = 256


def _params(*semantics):
    return pltpu.CompilerParams(dimension_semantics=semantics, vmem_limit_bytes=VMEM_LIMIT)


def _resident(shape):
    nd = len(shape)
    return pl.BlockSpec(shape, lambda *_: (0,) * nd, pipeline_mode=pl.Buffered(1))


def _rms(x, g):
    return x * lax.rsqrt(jnp.mean(x * x, axis=-1, keepdims=True) + EPS) * g


def _sigmoid(x):
    return 1.0 / (1.0 + jnp.exp(-x))


def _conv_mixer_kernel(x_ref, g_ref, w1_ref, b1_ref, wdw_ref, bdw_ref, lng_ref, lnb_ref,
                       w2_ref, b2_ref, o_ref, xnbuf, ubuf, cbuf, *, ts, d, kw):
    s = pl.program_id(1)
    x = x_ref[0]
    xnbuf[...] = _rms(x, g_ref[...]).astype(BF16)

    @pl.when(s == 0)
    def _():
        ubuf[0:CONV_HALO, :] = jnp.zeros((CONV_HALO, d), F32)

    def glu_panel(c0):
        a = jnp.dot(xnbuf[...], w1_ref[:, c0:c0 + CONV_PANEL], preferred_element_type=F32)
        a = a + b1_ref[:, c0:c0 + CONV_PANEL]
        g = jnp.dot(xnbuf[...], w1_ref[:, d + c0:d + c0 + CONV_PANEL], preferred_element_type=F32)
        g = g + b1_ref[:, d + c0:d + c0 + CONV_PANEL]
        ubuf[CONV_HALO:CONV_HALO + ts, c0:c0 + CONV_PANEL] = a * _sigmoid(g)

    first = CONV_HALO - (kw - 1)

    def conv_panel(c0):
        for cc in range(c0, c0 + CONV_PANEL, CONV_COLS):
            cols = slice(cc, cc + CONV_COLS)
            for r0 in range(0, ts, CONV_ROWS):
                acc = None
                for j in range(SUBLANES):
                    rows = CONV_ROWS + (SUBLANES if j else 0)
                    sj = None
                    for k in range(kw):
                        if (first + k) % SUBLANES == j:
                            a0 = r0 + (first + k) - j
                            term = ubuf[a0:a0 + rows, cols] * wdw_ref[k:k + 1, cols]
                            sj = term if sj is None else sj + term
                    if sj is not None:
                        sj = sj[j:j + CONV_ROWS]
                        acc = sj if acc is None else acc + sj
                cbuf[r0:r0 + CONV_ROWS, cols] = acc
        ubuf[0:CONV_HALO, c0:c0 + CONV_PANEL] = ubuf[ts:ts + CONV_HALO, c0:c0 + CONV_PANEL]

    glu_panel(0)
    for c0 in range(0, d, CONV_PANEL):
        if c0 + CONV_PANEL < d:
            glu_panel(c0 + CONV_PANEL)
        conv_panel(c0)

    c = cbuf[...] + bdw_ref[...]
    mu = jnp.mean(c, axis=-1, keepdims=True)
    xc = c - mu
    y = xc * lax.rsqrt(jnp.mean(xc * xc, axis=-1, keepdims=True) + EPS) * lng_ref[...] + lnb_ref[...]
    y = y * _sigmoid(y)
    o_ref[0] = x + jnp.dot(y.astype(BF16), w2_ref[...], preferred_element_type=F32) + b2_ref[...]


def _conv_mixer(h, g, w1, b1, wdw, bdw, lng, lnb, w2, b2, *, ts=512):
    b, s, d = h.shape
    kw = wdw.shape[0]
    assert s % ts == 0 and ts % CONV_ROWS == 0 and ts >= CONV_HALO >= kw - 1
    assert CONV_HALO % SUBLANES == 0 and d % CONV_PANEL == 0 and CONV_PANEL % CONV_COLS == 0
    row = lambda a: a.reshape(1, -1)
    kern = functools.partial(_conv_mixer_kernel, ts=ts, d=d, kw=kw)
    return pl.pallas_call(
        kern,
        out_shape=jax.ShapeDtypeStruct((b, s, d), F32),
        grid=(b, s // ts),
        in_specs=[
            pl.BlockSpec((1, ts, d), lambda i, j: (i, j, 0)),
            _resident((1, d)), _resident((d, 2 * d)), _resident((1, 2 * d)),
            _resident((kw, d)), _resident((1, d)), _resident((1, d)), _resident((1, d)),
            _resident((d, d)), _resident((1, d)),
        ],
        out_specs=pl.BlockSpec((1, ts, d), lambda i, j: (i, j, 0)),
        scratch_shapes=[pltpu.VMEM((ts, d), BF16), pltpu.VMEM((ts + CONV_HALO, d), F32), pltpu.VMEM((ts, d), F32)],
        compiler_params=_params("arbitrary", "arbitrary"),
        name="conv_mixer",
    )(h, row(g), w1.astype(BF16), row(b1), wdw, row(bdw), row(lng), row(lnb), w2.astype(BF16), row(b2))


def _qkv_kernel(x_ref, g_ref, w_ref, qg_ref, kg_ref, q_ref, k_ref, v_ref, *, d, hd):
    xn = _rms(x_ref[...], g_ref[...]).astype(BF16)
    qkv = jnp.dot(xn, w_ref[...], preferred_element_type=F32)
    for h in range(d // hd):
        cols = slice(h * hd, (h + 1) * hd)
        q_ref[:, cols] = _rms(qkv[:, h * hd:(h + 1) * hd], qg_ref[...]).astype(BF16)
        k_ref[:, cols] = _rms(qkv[:, d + h * hd:d + (h + 1) * hd], kg_ref[...]).astype(BF16)
    v_ref[...] = qkv[:, 2 * d:].astype(BF16)


def _qkv(h2d, g, w, qg, kg, *, tm=512):
    t, d = h2d.shape
    hd = qg.shape[0]
    kern = functools.partial(_qkv_kernel, d=d, hd=hd)
    tile = pl.BlockSpec((tm, d), lambda i: (i, 0))
    return pl.pallas_call(
        kern,
        out_shape=[jax.ShapeDtypeStruct((t, d), BF16)] * 3,
        grid=(t // tm,),
        in_specs=[tile, _resident((1, d)), _resident((d, 3 * d)), _resident((1, hd)), _resident((1, hd))],
        out_specs=[tile, tile, tile],
        compiler_params=_params("arbitrary"),
        name="attn_qkv",
    )(h2d, g.reshape(1, d), w.astype(BF16), qg.reshape(1, hd), kg.reshape(1, hd))


def _moba_kernel(slope_ref, q_ref, k_ref, v_ref, kaux_ref, vaux_ref, o_ref, kaug, vaug, *bufs,
                 nb, blk, hd, topk):
    scale = hd ** -0.5
    negz = NEG_INF / scale
    slope = slope_ref[0][:, :1]
    kaug[:, :hd] = k_ref[0]
    kaug[:, hd:] = kaux_ref[...]
    vaug[:, :hd] = v_ref[0]
    vaug[:, hd:] = vaux_ref[...]

    km = jnp.concatenate([jnp.mean(k_ref[0, n * blk:(n + 1) * blk, :].astype(F32), axis=0, keepdims=True)
                          for n in range(nb)], axis=0)
    hi = km.astype(BF16).astype(F32)
    mid = (km - hi).astype(BF16).astype(F32)
    lo = (km - hi - mid).astype(BF16).astype(F32)
    pad = [jnp.zeros((-3 * nb % (2 * SUBLANES), hd), F32)] if 3 * nb % (2 * SUBLANES) else []
    km3 = jnp.concatenate([hi, mid, lo] + pad, axis=0).astype(BF16)

    r = lax.broadcasted_iota(jnp.int32, (blk, blk), 0)
    c = lax.broadcasted_iota(jnp.int32, (blk, blk), 1)
    causal = r >= c
    keypos = lax.broadcasted_iota(jnp.int32, (1, blk), 1).astype(F32)
    nt = (((1,), (1,)), ((), ()))

    def scores(j):
        zbuf = bufs[j]
        q = q_ref[0, j * blk:(j + 1) * blk, :]
        select = j > topk
        if select:
            g3 = lax.dot_general(km3, q, nt, preferred_element_type=F32)
            gate = g3[0:nb] + g3[nb:2 * nb] + g3[2 * nb:3 * nb]
            g = [gate[m:m + 1, :] for m in range(j)]
            rows = []
            for n in range(j):
                beaten = jnp.zeros((1, blk), F32)
                for m in range(j):
                    if m != n:
                        ahead = (g[m] > g[n]) | (g[m] == g[n]) if m < n else (g[m] > g[n])
                        beaten = beaten + jnp.where(ahead, 1.0, 0.0)
                rows.append(jnp.where(beaten < topk, 0.0, negz))
            bias_t = jnp.concatenate(rows + [jnp.zeros((LANES - j, blk), F32)], axis=0)
            q = jnp.concatenate([q, bias_t.T.astype(BF16)], axis=1)
        mrun = None
        for n in range(j + 1):
            keys = kaug[n * blk:(n + 1) * blk, :] if select else k_ref[0, n * blk:(n + 1) * blk, :]
            z = lax.dot_general(q, keys, nt, preferred_element_type=F32)
            u = z * scale + slope * (keypos + float(n * blk))
            if n == j:
                u = jnp.where(causal, u, NEG_INF)
            zbuf[:, n * blk:(n + 1) * blk] = u
            for half in range(blk // LANES):
                f = u[:, half * LANES:(half + 1) * LANES]
                mrun = f if mrun is None else jnp.maximum(mrun, f)
        return jnp.broadcast_to(jnp.max(mrun, axis=-1, keepdims=True), (blk, LANES))

    def attend(j, mb):
        zbuf, pbuf = bufs[j], bufs[nb + j]
        for n in range(j + 1):
            for half in range(blk // LANES):
                cols = slice(n * blk + half * LANES, n * blk + (half + 1) * LANES)
                pbuf[:, cols] = jnp.exp(zbuf[:, cols] - mb).astype(BF16)
        acc = jnp.dot(pbuf[:, :(j + 1) * blk], vaug[:(j + 1) * blk, :], preferred_element_type=F32)
        o_ref[0, j * blk:(j + 1) * blk, :] = (acc[:, :hd] / acc[:, hd:hd + 1]).astype(BF16)

    order = list(range(nb - 1, -1, -1))
    mb_next = scores(order[0])
    for i, j in enumerate(order):
        mb = mb_next
        if i + 1 < nb:
            mb_next = scores(order[i + 1])
        attend(j, mb)


def _moba(q, k, v, *, n_heads):
    b, s, d = q.shape
    hd = d // n_heads
    blk = MOBA_BLOCK
    assert s % blk == 0 and hd == LANES and blk % LANES == 0
    nb = s // blk
    assert nb <= LANES
    topk = max(1, min(MOBA_TOPK, nb - 1))
    slopes = jnp.exp2(-8.0 * jnp.arange(1, n_heads + 1, dtype=F32) / n_heads)
    slopes = jnp.broadcast_to(slopes[:, None, None], (n_heads, 1, LANES))
    kaux = np.zeros((s, LANES), np.float32)
    kaux[np.arange(s), np.arange(s) // blk] = 1.0
    vaux = np.zeros((s, LANES), np.float32)
    vaux[:, 0] = 1.0
    kern = functools.partial(_moba_kernel, nb=nb, blk=blk, hd=hd, topk=topk)
    head = pl.BlockSpec((1, s, hd), lambda i, h: (i, 0, h))
    return pl.pallas_call(
        kern,
        out_shape=jax.ShapeDtypeStruct((b, s, d), BF16),
        grid=(b, n_heads),
        in_specs=[pl.BlockSpec((1, 1, LANES), lambda i, h: (h, 0, 0)), head, head, head,
                  _resident((s, LANES)), _resident((s, LANES))],
        out_specs=head,
        scratch_shapes=([pltpu.VMEM((s, hd + LANES), BF16), pltpu.VMEM((s, hd + LANES), BF16)]
                        + [pltpu.VMEM((blk, (j + 1) * blk), F32) for j in range(nb)]
                        + [pltpu.VMEM((blk, (j + 1) * blk), BF16) for j in range(nb)]),
        compiler_params=_params("arbitrary", "arbitrary"),
        name="moba_attn",
    )(slopes, q, k, v, jnp.asarray(kaux, BF16), jnp.asarray(vaux, BF16))


def _matmul_residual_kernel(h_ref, a_ref, w_ref, o_ref):
    o_ref[...] = h_ref[...] + jnp.dot(a_ref[...], w_ref[...], preferred_element_type=F32)


def _matmul_residual(h2d, a, w, *, tm=512):
    t, d = h2d.shape
    tile = pl.BlockSpec((tm, d), lambda i: (i, 0))
    return pl.pallas_call(
        _matmul_residual_kernel,
        out_shape=jax.ShapeDtypeStruct((t, d), F32),
        grid=(t // tm,),
        in_specs=[tile, pl.BlockSpec((tm, a.shape[1]), lambda i: (i, 0)), _resident(w.shape)],
        out_specs=tile,
        compiler_params=_params("arbitrary"),
        name="attn_out_proj",
    )(h2d, a, w.astype(BF16))


def _pair_order(n):
    todo = [(a, b) for a in range(n) for b in range(a + 1, n)]
    order = [todo.pop(0)]
    while todo:
        a, b = order[-1]
        nxt = next((p for p in todo if p[0] == a or p[1] == b), todo[0])
        todo.remove(nxt)
        order.append(nxt)
    return order


PAIRS = _pair_order(EXPERTS_PER_GROUP)
N_CLASSES = N_GROUPS * len(PAIRS)
META_CLASS, META_RANK, META_WA, META_WB = 0, 1, 2, 3
EXPERT_TILE = 256


def _router_kernel(h_ref, g_ref, wr_ref, br_ref, tri_ref, hx_ref, route_ref, cnt_ref, running, *, d):
    @pl.when(pl.program_id(0) == 0)
    def _():
        running[...] = jnp.zeros_like(running)

    xn = _rms(h_ref[...], g_ref[...])
    wr = wr_ref[...]
    xh, wh = xn.astype(BF16), wr.astype(BF16)
    xl, wl = (xn - xh.astype(F32)).astype(BF16), (wr - wh.astype(F32)).astype(BF16)
    logits = (jnp.dot(xh, wh, preferred_element_type=F32) + jnp.dot(xl, wh, preferred_element_type=F32)
              + jnp.dot(xh, wl, preferred_element_type=F32)) + br_ref[...]
    lane = lax.broadcasted_iota(jnp.int32, logits.shape, 1).astype(F32)

    def first_max(vals):
        mx = jnp.max(vals, axis=-1, keepdims=True)
        return mx, jnp.min(jnp.where(vals == mx, lane, float(ROUTER_LANES)), axis=-1, keepdims=True)

    is_g = lane < N_GROUPS
    gmax, gidx = first_max(jnp.where(is_g, logits, -jnp.inf))
    zg = jnp.sum(jnp.where(is_g, jnp.exp(logits - gmax), 0.0), axis=-1, keepdims=True)
    g_w = 1.0 / zg
    lo = EXPERT_LANE0 + gidx * EXPERTS_PER_GROUP
    in_grp = (lane >= lo) & (lane < lo + EXPERTS_PER_GROUP)
    el = jnp.where(in_grp, logits, -jnp.inf)
    m1, i1 = first_max(el)
    m2, i2 = first_max(jnp.where(lane == i1, -jnp.inf, el))
    p2 = jnp.exp(m2 - m1)
    w1 = g_w / (1.0 + p2)
    w2 = g_w * p2 / (1.0 + p2)
    e1, e2 = i1 - lo, i2 - lo
    ea, eb = jnp.minimum(e1, e2), jnp.maximum(e1, e2)
    wa, wb = jnp.where(e1 < e2, w1, w2), jnp.where(e1 < e2, w2, w1)
    pair = sum(jnp.where((ea == a) & (eb == b), float(n), 0.0) for n, (a, b) in enumerate(PAIRS))
    cls = gidx * float(len(PAIRS)) + pair
    onehot = jnp.where(lane == cls, 1.0, 0.0)
    before = jnp.dot(tri_ref[...], onehot.astype(BF16), preferred_element_type=F32) + running[...]
    rank = jnp.sum(onehot * before, axis=-1, keepdims=True)
    running[...] += jnp.sum(onehot, axis=0, keepdims=True)
    cnt_ref[...] = running[...]

    meta = (jnp.where(lane == META_CLASS, cls, 0.0) + jnp.where(lane == META_RANK, rank, 0.0)
            + jnp.where(lane == META_WA, wa, 0.0) + jnp.where(lane == META_WB, wb, 0.0))
    hx_ref[:, 0, :d] = xn
    hx_ref[:, 0, d:] = meta
    route_ref[...] = meta.T[:SUBLANES, :]


def _router(h2d, g, w_group, b_group, w_router, b_router, *, tm=512):
    t, d = h2d.shape
    ng, _, ne = w_router.shape
    assert (ng, ne) == (N_GROUPS, EXPERTS_PER_GROUP) and N_CLASSES <= ROUTER_LANES
    wr = jnp.concatenate([w_group, jnp.transpose(w_router, (1, 0, 2)).reshape(d, ng * ne)], axis=1)
    br = jnp.concatenate([b_group, b_router.reshape(ng * ne)])
    pad = ROUTER_LANES - wr.shape[1]
    wr = jnp.pad(wr, ((0, 0), (0, pad)))
    br = jnp.pad(br, (0, pad)).reshape(1, ROUTER_LANES)
    tri = jnp.asarray(np.tril(np.ones((tm, tm), np.float32), -1), BF16)
    return pl.pallas_call(
        functools.partial(_router_kernel, d=d),
        out_shape=[jax.ShapeDtypeStruct((t, 1, d + ROUTER_LANES), F32), jax.ShapeDtypeStruct((SUBLANES, t), F32),
                   jax.ShapeDtypeStruct((1, ROUTER_LANES), F32)],
        grid=(t // tm,),
        in_specs=[pl.BlockSpec((tm, d), lambda i: (i, 0)), _resident((1, d)), _resident((d, ROUTER_LANES)),
                  _resident((1, ROUTER_LANES)), _resident((tm, tm))],
        out_specs=[pl.BlockSpec((tm, 1, d + ROUTER_LANES), lambda i: (i, 0, 0)),
                   pl.BlockSpec((SUBLANES, tm), lambda i: (0, i)),
                   pl.BlockSpec((1, ROUTER_LANES), lambda i: (0, 0))],
        scratch_shapes=[pltpu.VMEM((1, ROUTER_LANES), F32)],
        compiler_params=_params("arbitrary"),
        name="moe_router",
    )(h2d, g.reshape(1, d), wr, br, tri)


ITEM_VALID, ITEM_FIRST_OF_TILE, ITEM_NEW_A, ITEM_NEW_B = 1, 2, 4, 8


def _prefix_sum(v):
    ids = jnp.arange(v.shape[0], dtype=jnp.int32)
    return jnp.sum(jnp.where(ids[None, :] <= ids[:, None], v[None, :], 0), axis=1)


def _sorted_layout(route, counts, n_tiles, n_items):
    cls = route[META_CLASS].astype(jnp.int32)
    rank = route[META_RANK].astype(jnp.int32)
    cnt = counts[0, :N_CLASSES].astype(jnp.int32)
    ids = jnp.arange(N_CLASSES, dtype=jnp.int32)
    end = _prefix_sum(cnt)
    start = end - cnt
    pos = jnp.sum(jnp.where(cls[None, :] == ids[:, None], start[:, None], 0), axis=0) + rank

    first = start // EXPERT_TILE
    last = jnp.where(cnt > 0, (end - 1) // EXPERT_TILE, first - 1)
    items = last - first + 1
    item_end = _prefix_sum(items)
    k = jnp.arange(n_items, dtype=jnp.int32)
    item_cls = jnp.sum((item_end[None, :] <= k[:, None]).astype(jnp.int32), axis=1)
    valid = item_cls < N_CLASSES
    item_cls = jnp.where(valid, item_cls, jnp.max(jnp.where(cnt > 0, ids, 0)))
    sel = item_cls[:, None] == ids[None, :]
    pick = lambda v: jnp.sum(jnp.where(sel, v[None, :], 0), axis=1)
    tile = jnp.where(valid, pick(first) + k - pick(item_end - items), n_tiles - 1)
    lo = jnp.where(valid, jnp.clip(pick(start) - tile * EXPERT_TILE, 0, EXPERT_TILE), 0)
    hi = jnp.where(valid, jnp.clip(pick(end) - tile * EXPERT_TILE, 0, EXPERT_TILE), 0)

    grp, pair = item_cls // len(PAIRS), item_cls % len(PAIRS)
    ea = grp * EXPERTS_PER_GROUP + sum(jnp.where(pair == n, a, 0) for n, (a, _) in enumerate(PAIRS))
    eb = grp * EXPERTS_PER_GROUP + sum(jnp.where(pair == n, b, 0) for n, (_, b) in enumerate(PAIRS))
    prev = lambda v: jnp.concatenate([jnp.full((1,), -1, jnp.int32), v[:-1]])
    flags = (jnp.where(valid, ITEM_VALID, 0) + jnp.where(valid & (tile != prev(tile)), ITEM_FIRST_OF_TILE, 0)
             + jnp.where(ea != prev(ea), ITEM_NEW_A, 0) + jnp.where(eb != prev(eb), ITEM_NEW_B, 0))
    return pos, (tile, ea, eb, lo, hi, flags)


DISPATCH_CHUNK = 2048
ROW_DMA_UNROLL = 8


def _row_copy(src, dst, i, j, sem):
    return pltpu.make_async_copy(src.at[pl.ds(i, 1)], dst.at[pl.ds(j, 1)], sem)


def _dispatch_kernel(pos_ref, hx_ref, hxs_ref, sem):
    base = pl.program_id(0) * DISPATCH_CHUNK

    def issue(r, carry):
        _row_copy(hx_ref, hxs_ref, r, pos_ref[base + r], sem).start()
        return carry

    lax.fori_loop(0, DISPATCH_CHUNK, issue, 0, unroll=ROW_DMA_UNROLL)
    pltpu.make_async_copy(hx_ref, hxs_ref.at[pl.ds(0, DISPATCH_CHUNK)], sem).wait()


def _dispatch(hx, pos):
    t, _, w = hx.shape
    assert t % DISPATCH_CHUNK == 0
    return pl.pallas_call(
        _dispatch_kernel,
        out_shape=jax.ShapeDtypeStruct((t, 1, w), F32),
        grid_spec=pltpu.PrefetchScalarGridSpec(
            num_scalar_prefetch=1,
            grid=(t // DISPATCH_CHUNK,),
            in_specs=[pl.BlockSpec((DISPATCH_CHUNK, 1, w), lambda i, pos: (i, 0, 0))],
            out_specs=pl.BlockSpec(memory_space=pl.ANY),
            scratch_shapes=[pltpu.SemaphoreType.DMA(())],
        ),
        compiler_params=_params("arbitrary"),
        name="moe_dispatch",
    )(pos, hx)


def _experts_kernel(tile_ref, ea_ref, eb_ref, lo_ref, hi_ref, flag_ref, x_ref, wga_ref, wua_ref, wda_ref,
                    wgb_ref, wub_ref, wdb_ref, y_ref, w1, w2, xs, *, d, hid):
    del tile_ref, ea_ref, eb_ref
    k = pl.program_id(0)
    flags = flag_ref[k]

    @pl.when((flags & ITEM_NEW_A) != 0)
    def _():
        w1[:, 0:hid] = wga_ref[...].astype(BF16)
        w1[:, hid:2 * hid] = wua_ref[...].astype(BF16)
        w2[0:hid, :] = wda_ref[...].astype(BF16)

    @pl.when((flags & ITEM_NEW_B) != 0)
    def _():
        w1[:, 2 * hid:3 * hid] = wgb_ref[...].astype(BF16)
        w1[:, 3 * hid:] = wub_ref[...].astype(BF16)
        w2[hid:, :] = wdb_ref[...].astype(BF16)

    @pl.when((flags & ITEM_VALID) != 0)
    def _():
        row = lax.broadcasted_iota(jnp.int32, (EXPERT_TILE, 1), 0)
        inside = (row >= lo_ref[k]) & (row < hi_ref[k])
        xs[...] = x_ref[:, 0, :]
        wa = jnp.where(inside, xs[:, d + META_WA:d + META_WA + 1], 0.0)
        wb = jnp.where(inside, xs[:, d + META_WB:d + META_WB + 1], 0.0)
        gu = jnp.dot(xs[:, :d].astype(BF16), w1[...], preferred_element_type=F32)
        ga, gb = gu[:, 0:hid], gu[:, 2 * hid:3 * hid]
        ha = ga * _sigmoid(ga) * gu[:, hid:2 * hid] * wa
        hb = gb * _sigmoid(gb) * gu[:, 3 * hid:] * wb
        y = jnp.dot(jnp.concatenate([ha, hb], axis=1).astype(BF16), w2[...], preferred_element_type=F32)

        @pl.when((flags & ITEM_FIRST_OF_TILE) != 0)
        def _():
            y_ref[...] = y

        @pl.when((flags & ITEM_FIRST_OF_TILE) == 0)
        def _():
            y_ref[...] += y


def _experts(hxs, items, w_gate, w_up, w_down, layer, *, d):
    t, _, w = hxs.shape
    hid = w_gate.shape[-1]
    n_items = items[0].shape[0]
    up_a = pl.BlockSpec((None, None, d, hid), lambda k, tile, ea, eb, lo, hi, fl: (layer, ea[k], 0, 0))
    up_b = pl.BlockSpec((None, None, d, hid), lambda k, tile, ea, eb, lo, hi, fl: (layer, eb[k], 0, 0))
    down_a = pl.BlockSpec((None, None, hid, d), lambda k, tile, ea, eb, lo, hi, fl: (layer, ea[k], 0, 0))
    down_b = pl.BlockSpec((None, None, hid, d), lambda k, tile, ea, eb, lo, hi, fl: (layer, eb[k], 0, 0))
    return pl.pallas_call(
        functools.partial(_experts_kernel, d=d, hid=hid),
        out_shape=jax.ShapeDtypeStruct((t, d), F32),
        grid_spec=pltpu.PrefetchScalarGridSpec(
            num_scalar_prefetch=len(items),
            grid=(n_items,),
            in_specs=[pl.BlockSpec((EXPERT_TILE, 1, w), lambda k, tile, ea, eb, lo, hi, fl: (tile[k], 0, 0)),
                      up_a, up_a, down_a, up_b, up_b, down_b],
            out_specs=pl.BlockSpec((EXPERT_TILE, d), lambda k, tile, ea, eb, lo, hi, fl: (tile[k], 0)),
            scratch_shapes=[pltpu.VMEM((d, 4 * hid), BF16), pltpu.VMEM((2 * hid, d), BF16),
                            pltpu.VMEM((EXPERT_TILE, w), F32)],
        ),
        compiler_params=_params("arbitrary"),
        name="moe_experts",
    )(*items, hxs, w_gate, w_up, w_down, w_gate, w_up, w_down)


def _combine_ple_kernel(pos_ref, h_ref, p_ref, g_ref, wg_ref, wp_ref, ys_ref, o_ref, ybuf, sems, *, tm):
    i = pl.program_id(0)
    n = pl.num_programs(0)
    slot = i % 2

    def wait(s):
        pltpu.make_async_copy(ys_ref.at[pl.ds(0, tm)], ybuf.at[s], sems.at[s]).wait()

    @pl.when(i == 0)
    def _():
        def issue(r, carry):
            _row_copy(ys_ref, ybuf.at[0], pos_ref[r], r, sems.at[0]).start()
            return carry
        lax.fori_loop(0, tm, issue, 0, unroll=ROW_DMA_UNROLL)

    nxt = jnp.minimum(i + 1, n - 1) * tm
    for r in range(tm):
        _row_copy(ys_ref, ybuf.at[1 - slot], pos_ref[nxt + r], r, sems.at[1 - slot]).start()

    wait(slot)
    h = h_ref[...] + ybuf[slot]
    gate = _sigmoid(jnp.dot(_rms(h, g_ref[...]).astype(BF16), wg_ref[...], preferred_element_type=F32))
    proj = jnp.dot(p_ref[...].astype(BF16), wp_ref[...], preferred_element_type=F32)
    o_ref[...] = h + gate * proj

    @pl.when(i == n - 1)
    def _():
        wait(1 - slot)


def _combine_ple(h2d, ys, pos, p3d, layer, g, w_gate, w_proj, *, tm=512):
    t, d = h2d.shape
    pd = p3d.shape[-1]
    tile = pl.BlockSpec((tm, d), lambda i, pos: (i, 0))
    res = lambda shape: pl.BlockSpec(shape, lambda i, pos: (0,) * len(shape), pipeline_mode=pl.Buffered(1))
    return pl.pallas_call(
        functools.partial(_combine_ple_kernel, tm=tm),
        out_shape=jax.ShapeDtypeStruct((t, d), F32),
        grid_spec=pltpu.PrefetchScalarGridSpec(
            num_scalar_prefetch=1,
            grid=(t // tm,),
            in_specs=[tile, pl.BlockSpec((None, tm, pd), lambda i, pos: (layer, i, 0)), res((1, d)), res((d, d)),
                      res((pd, d)), pl.BlockSpec(memory_space=pl.ANY)],
            out_specs=tile,
            scratch_shapes=[pltpu.VMEM((2, tm, d), F32), pltpu.SemaphoreType.DMA((2,))],
        ),
        compiler_params=_params("arbitrary"),
        name="moe_combine_ple",
    )(pos, h2d, p3d, g.reshape(1, d), w_gate.astype(BF16), w_proj.astype(BF16), ys)


def _moe_ple(h2d, p3d, layer, g_ffn, w_group, b_group, w_router, b_router, w_gate, w_up, w_down, g_ple,
             ple_gate, ple_proj):
    t, d = h2d.shape
    assert t % EXPERT_TILE == 0
    n_tiles = t // EXPERT_TILE
    n_items = n_tiles + N_CLASSES - 1
    hx, route, counts = _router(h2d, g_ffn, w_group, b_group, w_router, b_router)
    pos, items = _sorted_layout(route, counts, n_tiles, n_items)
    hxs = _dispatch(hx, pos)
    ys = _experts(hxs, items, w_gate, w_up, w_down, layer, d=d)
    return _combine_ple(h2d, ys, pos, p3d, layer, g_ple, ple_gate, ple_proj)


def kernel(x, p, g_mix, g_ffn, g_ple, conv_w_pw1, conv_b_pw1, conv_w_dw, conv_b_dw, conv_ln_g, conv_ln_b, conv_w_pw2, conv_b_pw2, attn_w_qkv, attn_q_gain, attn_k_gain, attn_w_o, moe_w_group, moe_b_group, moe_w_router, moe_b_router, moe_w_gate, moe_w_up, moe_w_down, ple_w_gate, ple_w_proj):
    b, s, d = x.shape
    depth = g_mix.shape[0]
    t = b * s
    h = x
    for i in range(depth):
        if i % 2 == 0:
            c = i // 2
            h = _conv_mixer(h.reshape(b, s, d), g_mix[i], conv_w_pw1[c], conv_b_pw1[c], conv_w_dw[c],
                            conv_b_dw[c], conv_ln_g[c], conv_ln_b[c], conv_w_pw2[c], conv_b_pw2[c])
            h = h.reshape(t, d)
        else:
            a = i // 2
            h = h.reshape(t, d)
            q, k, v = _qkv(h, g_mix[i], attn_w_qkv[a], attn_q_gain[a], attn_k_gain[a])
            o = _moba(q.reshape(b, s, d), k.reshape(b, s, d), v.reshape(b, s, d), n_heads=N_HEADS)
            h = _matmul_residual(h, o.reshape(t, d), attn_w_o[a])
        h = _moe_ple(h, p.reshape(depth, t, -1), i, g_ffn[i], moe_w_group[i], moe_b_group[i], moe_w_router[i],
                     moe_b_router[i], moe_w_gate, moe_w_up, moe_w_down, g_ple[i], ple_w_gate[i], ple_w_proj[i])
    return h.reshape(b, s, d)
```

```python
import functools

import jax
import jax.numpy as jnp
import numpy as np
from jax import lax
from jax.experimental import pallas as pl
from jax.experimental.pallas import tpu as pltpu

F32 = jnp.float32
BF16 = jnp.bfloat16

EPS = 1e-6
NEG_INF = -1e30

N_HEADS = 8
MOBA_BLOCK = 256
MOBA_TOPK = 3
N_GROUPS = 4
EXPERTS_PER_GROUP = 4
N_EXPERTS = N_GROUPS * EXPERTS_PER_GROUP

V7X_VMEM_BYTES = 64 * 1024 * 1024
LANES = 128
SUBLANES = 8
VMEM_LIMIT = V7X_VMEM_BYTES * 7 // 8

ROUTER_LANES = LANES
EXPERT_LANE0 = N_GROUPS

CONV_HALO = 32
CONV_ROWS = 64
CONV_COLS = 128
CONV_PANEL = 256


def _params(*semantics):
    return pltpu.CompilerParams(dimension_semantics=semantics, vmem_limit_bytes=VMEM_LIMIT)


def _resident(shape):
    nd = len(shape)
    return pl.BlockSpec(shape, lambda *_: (0,) * nd, pipeline_mode=pl.Buffered(1))


def _rms(x, g):
    return x * lax.rsqrt(jnp.mean(x * x, axis=-1, keepdims=True) + EPS) * g


def _sigmoid(x):
    return 1.0 / (1.0 + jnp.exp(-x))


def _conv_mixer_kernel(x_ref, g_ref, w1_ref, b1_ref, wdw_ref, bdw_ref, lng_ref, lnb_ref,
                       w2_ref, b2_ref, o_ref, xnbuf, ubuf, cbuf, *, ts, d, kw):
    s = pl.program_id(1)
    x = x_ref[0]
    xnbuf[...] = _rms(x, g_ref[...]).astype(BF16)

    @pl.when(s == 0)
    def _():
        ubuf[0:CONV_HALO, :] = jnp.zeros((CONV_HALO, d), F32)

    def glu_panel(c0):
        a = jnp.dot(xnbuf[...], w1_ref[:, c0:c0 + CONV_PANEL], preferred_element_type=F32)
        a = a + b1_ref[:, c0:c0 + CONV_PANEL]
        g = jnp.dot(xnbuf[...], w1_ref[:, d + c0:d + c0 + CONV_PANEL], preferred_element_type=F32)
        g = g + b1_ref[:, d + c0:d + c0 + CONV_PANEL]
        ubuf[CONV_HALO:CONV_HALO + ts, c0:c0 + CONV_PANEL] = a * _sigmoid(g)

    first = CONV_HALO - (kw - 1)

    def conv_panel(c0):
        for cc in range(c0, c0 + CONV_PANEL, CONV_COLS):
            cols = slice(cc, cc + CONV_COLS)
            for r0 in range(0, ts, CONV_ROWS):
                acc = None
                for j in range(SUBLANES):
                    rows = CONV_ROWS + (SUBLANES if j else 0)
                    sj = None
                    for k in range(kw):
                        if (first + k) % SUBLANES == j:
                            a0 = r0 + (first + k) - j
                            term = ubuf[a0:a0 + rows, cols] * wdw_ref[k:k + 1, cols]
                            sj = term if sj is None else sj + term
                    if sj is not None:
                        sj = sj[j:j + CONV_ROWS]
                        acc = sj if acc is None else acc + sj
                cbuf[r0:r0 + CONV_ROWS, cols] = acc
        ubuf[0:CONV_HALO, c0:c0 + CONV_PANEL] = ubuf[ts:ts + CONV_HALO, c0:c0 + CONV_PANEL]

    glu_panel(0)
    for c0 in range(0, d, CONV_PANEL):
        if c0 + CONV_PANEL < d:
            glu_panel(c0 + CONV_PANEL)
        conv_panel(c0)

    c = cbuf[...] + bdw_ref[...]
    mu = jnp.mean(c, axis=-1, keepdims=True)
    xc = c - mu
    y = xc * lax.rsqrt(jnp.mean(xc * xc, axis=-1, keepdims=True) + EPS) * lng_ref[...] + lnb_ref[...]
    y = y * _sigmoid(y)
    o_ref[0] = x + jnp.dot(y.astype(BF16), w2_ref[...], preferred_element_type=F32) + b2_ref[...]


def _conv_mixer(h, g, w1, b1, wdw, bdw, lng, lnb, w2, b2, *, ts=512):
    b, s, d = h.shape
    kw = wdw.shape[0]
    assert s % ts == 0 and ts % CONV_ROWS == 0 and ts >= CONV_HALO >= kw - 1
    assert CONV_HALO % SUBLANES == 0 and d % CONV_PANEL == 0 and CONV_PANEL % CONV_COLS == 0
    row = lambda a: a.reshape(1, -1)
    kern = functools.partial(_conv_mixer_kernel, ts=ts, d=d, kw=kw)
    return pl.pallas_call(
        kern,
        out_shape=jax.ShapeDtypeStruct((b, s, d), F32),
        grid=(b, s // ts),
        in_specs=[
            pl.BlockSpec((1, ts, d), lambda i, j: (i, j, 0)),
            _resident((1, d)), _resident((d, 2 * d)), _resident((1, 2 * d)),
            _resident((kw, d)), _resident((1, d)), _resident((1, d)), _resident((1, d)),
            _resident((d, d)), _resident((1, d)),
        ],
        out_specs=pl.BlockSpec((1, ts, d), lambda i, j: (i, j, 0)),
        scratch_shapes=[pltpu.VMEM((ts, d), BF16), pltpu.VMEM((ts + CONV_HALO, d), F32), pltpu.VMEM((ts, d), F32)],
        compiler_params=_params("arbitrary", "arbitrary"),
        name="conv_mixer",
    )(h, row(g), w1.astype(BF16), row(b1), wdw, row(bdw), row(lng), row(lnb), w2.astype(BF16), row(b2))


def _qkv_kernel(x_ref, g_ref, w_ref, qg_ref, kg_ref, q_ref, k_ref, v_ref, *, d, hd):
    xn = _rms(x_ref[...], g_ref[...]).astype(BF16)
    qkv = jnp.dot(xn, w_ref[...], preferred_element_type=F32)
    for h in range(d // hd):
        cols = slice(h * hd, (h + 1) * hd)
        q_ref[:, cols] = _rms(qkv[:, h * hd:(h + 1) * hd], qg_ref[...]).astype(BF16)
        k_ref[:, cols] = _rms(qkv[:, d + h * hd:d + (h + 1) * hd], kg_ref[...]).astype(BF16)
    v_ref[...] = qkv[:, 2 * d:].astype(BF16)


def _qkv(h2d, g, w, qg, kg, *, tm=512):
    t, d = h2d.shape
    hd = qg.shape[0]
    kern = functools.partial(_qkv_kernel, d=d, hd=hd)
    tile = pl.BlockSpec((tm, d), lambda i: (i, 0))
    return pl.pallas_call(
        kern,
        out_shape=[jax.ShapeDtypeStruct((t, d), BF16)] * 3,
        grid=(t // tm,),
        in_specs=[tile, _resident((1, d)), _resident((d, 3 * d)), _resident((1, hd)), _resident((1, hd))],
        out_specs=[tile, tile, tile],
        compiler_params=_params("arbitrary"),
        name="attn_qkv",
    )(h2d, g.reshape(1, d), w.astype(BF16), qg.reshape(1, hd), kg.reshape(1, hd))


def _moba_kernel(slope_ref, q_ref, k_ref, v_ref, kaux_ref, vaux_ref, o_ref, kaug, vaug, *bufs,
                 nb, blk, hd, topk):
    scale = hd ** -0.5
    negz = NEG_INF / scale
    slope = slope_ref[0][:, :1]
    kaug[:, :hd] = k_ref[0]
    kaug[:, hd:] = kaux_ref[...]
    vaug[:, :hd] = v_ref[0]
    vaug[:, hd:] = vaux_ref[...]

    km = jnp.concatenate([jnp.mean(k_ref[0, n * blk:(n + 1) * blk, :].astype(F32), axis=0, keepdims=True)
                          for n in range(nb)], axis=0)
    hi = km.astype(BF16).astype(F32)
    mid = (km - hi).astype(BF16).astype(F32)
    lo = (km - hi - mid).astype(BF16).astype(F32)
    pad = [jnp.zeros((-3 * nb % (2 * SUBLANES), hd), F32)] if 3 * nb % (2 * SUBLANES) else []
    km3 = jnp.concatenate([hi, mid, lo] + pad, axis=0).astype(BF16)

    r = lax.broadcasted_iota(jnp.int32, (blk, blk), 0)
    c = lax.broadcasted_iota(jnp.int32, (blk, blk), 1)
    causal = r >= c
    keypos = lax.broadcasted_iota(jnp.int32, (1, blk), 1).astype(F32)
    nt = (((1,), (1,)), ((), ()))

    def scores(j):
        zbuf = bufs[j]
        q = q_ref[0, j * blk:(j + 1) * blk, :]
        select = j > topk
        if select:
            g3 = lax.dot_general(km3, q, nt, preferred_element_type=F32)
            gate = g3[0:nb] + g3[nb:2 * nb] + g3[2 * nb:3 * nb]
            g = [gate[m:m + 1, :] for m in range(j)]
            rows = []
            for n in range(j):
                beaten = jnp.zeros((1, blk), F32)
                for m in range(j):
                    if m != n:
                        ahead = (g[m] > g[n]) | (g[m] == g[n]) if m < n else (g[m] > g[n])
                        beaten = beaten + jnp.where(ahead, 1.0, 0.0)
                rows.append(jnp.where(beaten < topk, 0.0, negz))
            bias_t = jnp.concatenate(rows + [jnp.zeros((LANES - j, blk), F32)], axis=0)
            q = jnp.concatenate([q, bias_t.T.astype(BF16)], axis=1)
        mrun = None
        for n in range(j + 1):
            keys = kaug[n * blk:(n + 1) * blk, :] if select else k_ref[0, n * blk:(n + 1) * blk, :]
            z = lax.dot_general(q, keys, nt, preferred_element_type=F32)
            u = z * scale + slope * (keypos + float(n * blk))
            if n == j:
                u = jnp.where(causal, u, NEG_INF)
            zbuf[:, n * blk:(n + 1) * blk] = u
            for half in range(blk // LANES):
                f = u[:, half * LANES:(half + 1) * LANES]
                mrun = f if mrun is None else jnp.maximum(mrun, f)
        return jnp.broadcast_to(jnp.max(mrun, axis=-1, keepdims=True), (blk, LANES))

    def attend(j, mb):
        zbuf, pbuf = bufs[j], bufs[nb + j]
        for n in range(j + 1):
            for half in range(blk // LANES):
                cols = slice(n * blk + half * LANES, n * blk + (half + 1) * LANES)
                pbuf[:, cols] = jnp.exp(zbuf[:, cols] - mb).astype(BF16)
        acc = jnp.dot(pbuf[:, :(j + 1) * blk], vaug[:(j + 1) * blk, :], preferred_element_type=F32)
        o_ref[0, j * blk:(j + 1) * blk, :] = (acc[:, :hd] / acc[:, hd:hd + 1]).astype(BF16)

    order = list(range(nb - 1, -1, -1))
    mb_next = scores(order[0])
    for i, j in enumerate(order):
        mb = mb_next
        if i + 1 < nb:
            mb_next = scores(order[i + 1])
        attend(j, mb)


def _moba(q, k, v, *, n_heads):
    b, s, d = q.shape
    hd = d // n_heads
    blk = MOBA_BLOCK
    assert s % blk == 0 and hd == LANES and blk % LANES == 0
    nb = s // blk
    assert nb <= LANES
    topk = max(1, min(MOBA_TOPK, nb - 1))
    slopes = jnp.exp2(-8.0 * jnp.arange(1, n_heads + 1, dtype=F32) / n_heads)
    slopes = jnp.broadcast_to(slopes[:, None, None], (n_heads, 1, LANES))
    kaux = np.zeros((s, LANES), np.float32)
    kaux[np.arange(s), np.arange(s) // blk] = 1.0
    vaux = np.zeros((s, LANES), np.float32)
    vaux[:, 0] = 1.0
    kern = functools.partial(_moba_kernel, nb=nb, blk=blk, hd=hd, topk=topk)
    head = pl.BlockSpec((1, s, hd), lambda i, h: (i, 0, h))
    return pl.pallas_call(
        kern,
        out_shape=jax.ShapeDtypeStruct((b, s, d), BF16),
        grid=(b, n_heads),
        in_specs=[pl.BlockSpec((1, 1, LANES), lambda i, h: (h, 0, 0)), head, head, head,
                  _resident((s, LANES)), _resident((s, LANES))],
        out_specs=head,
        scratch_shapes=([pltpu.VMEM((s, hd + LANES), BF16), pltpu.VMEM((s, hd + LANES), BF16)]
                        + [pltpu.VMEM((blk, (j + 1) * blk), F32) for j in range(nb)]
                        + [pltpu.VMEM((blk, (j + 1) * blk), BF16) for j in range(nb)]),
        compiler_params=_params("arbitrary", "arbitrary"),
        name="moba_attn",
    )(slopes, q, k, v, jnp.asarray(kaux, BF16), jnp.asarray(vaux, BF16))


def _matmul_residual_kernel(h_ref, a_ref, w_ref, o_ref):
    o_ref[...] = h_ref[...] + jnp.dot(a_ref[...], w_ref[...], preferred_element_type=F32)


def _matmul_residual(h2d, a, w, *, tm=512):
    t, d = h2d.shape
    tile = pl.BlockSpec((tm, d), lambda i: (i, 0))
    return pl.pallas_call(
        _matmul_residual_kernel,
        out_shape=jax.ShapeDtypeStruct((t, d), F32),
        grid=(t // tm,),
        in_specs=[tile, pl.BlockSpec((tm, a.shape[1]), lambda i: (i, 0)), _resident(w.shape)],
        out_specs=tile,
        compiler_params=_params("arbitrary"),
        name="attn_out_proj",
    )(h2d, a, w.astype(BF16))


def _pair_order(n):
    todo = [(a, b) for a in range(n) for b in range(a + 1, n)]
    order = [todo.pop(0)]
    while todo:
        a, b = order[-1]
        nxt = next((p for p in todo if p[0] == a or p[1] == b), todo[0])
        todo.remove(nxt)
        order.append(nxt)
    return order


PAIRS = _pair_order(EXPERTS_PER_GROUP)
N_CLASSES = N_GROUPS * len(PAIRS)
META_CLASS, META_RANK, META_WA, META_WB = 0, 1, 2, 3
EXPERT_TILE = 256


def _router_kernel(h_ref, g_ref, wr_ref, br_ref, tri_ref, hx_ref, route_ref, cnt_ref, running, *, d):
    @pl.when(pl.program_id(0) == 0)
    def _():
        running[...] = jnp.zeros_like(running)

    xn = _rms(h_ref[...], g_ref[...])
    wr = wr_ref[...]
    xh, wh = xn.astype(BF16), wr.astype(BF16)
    xl, wl = (xn - xh.astype(F32)).astype(BF16), (wr - wh.astype(F32)).astype(BF16)
    logits = (jnp.dot(xh, wh, preferred_element_type=F32) + jnp.dot(xl, wh, preferred_element_type=F32)
              + jnp.dot(xh, wl, preferred_element_type=F32)) + br_ref[...]
    lane = lax.broadcasted_iota(jnp.int32, logits.shape, 1).astype(F32)

    def first_max(vals):
        mx = jnp.max(vals, axis=-1, keepdims=True)
        return mx, jnp.min(jnp.where(vals == mx, lane, float(ROUTER_LANES)), axis=-1, keepdims=True)

    is_g = lane < N_GROUPS
    gmax, gidx = first_max(jnp.where(is_g, logits, -jnp.inf))
    zg = jnp.sum(jnp.where(is_g, jnp.exp(logits - gmax), 0.0), axis=-1, keepdims=True)
    g_w = 1.0 / zg
    lo = EXPERT_LANE0 + gidx * EXPERTS_PER_GROUP
    in_grp = (lane >= lo) & (lane < lo + EXPERTS_PER_GROUP)
    el = jnp.where(in_grp, logits, -jnp.inf)
    m1, i1 = first_max(el)
    m2, i2 = first_max(jnp.where(lane == i1, -jnp.inf, el))
    p2 = jnp.exp(m2 - m1)
    w1 = g_w / (1.0 + p2)
    w2 = g_w * p2 / (1.0 + p2)
    e1, e2 = i1 - lo, i2 - lo
    ea, eb = jnp.minimum(e1, e2), jnp.maximum(e1, e2)
    wa, wb = jnp.where(e1 < e2, w1, w2), jnp.where(e1 < e2, w2, w1)
    pair = sum(jnp.where((ea == a) & (eb == b), float(n), 0.0) for n, (a, b) in enumerate(PAIRS))
    cls = gidx * float(len(PAIRS)) + pair
    onehot = jnp.where(lane == cls, 1.0, 0.0)
    before = jnp.dot(tri_ref[...], onehot.astype(BF16), preferred_element_type=F32) + running[...]
    rank = jnp.sum(onehot * before, axis=-1, keepdims=True)
    running[...] += jnp.sum(onehot, axis=0, keepdims=True)
    cnt_ref[...] = running[...]

    meta = (jnp.where(lane == META_CLASS, cls, 0.0) + jnp.where(lane == META_RANK, rank, 0.0)
            + jnp.where(lane == META_WA, wa, 0.0) + jnp.where(lane == META_WB, wb, 0.0))
    hx_ref[:, 0, :d] = xn
    hx_ref[:, 0, d:] = meta
    route_ref[...] = meta.T[:SUBLANES, :]


def _router(h2d, g, w_group, b_group, w_router, b_router, *, tm=512):
    t, d = h2d.shape
    ng, _, ne = w_router.shape
    assert (ng, ne) == (N_GROUPS, EXPERTS_PER_GROUP) and N_CLASSES <= ROUTER_LANES
    wr = jnp.concatenate([w_group, jnp.transpose(w_router, (1, 0, 2)).reshape(d, ng * ne)], axis=1)
    br = jnp.concatenate([b_group, b_router.reshape(ng * ne)])
    pad = ROUTER_LANES - wr.shape[1]
    wr = jnp.pad(wr, ((0, 0), (0, pad)))
    br = jnp.pad(br, (0, pad)).reshape(1, ROUTER_LANES)
    tri = jnp.asarray(np.tril(np.ones((tm, tm), np.float32), -1), BF16)
    return pl.pallas_call(
        functools.partial(_router_kernel, d=d),
        out_shape=[jax.ShapeDtypeStruct((t, 1, d + ROUTER_LANES), F32), jax.ShapeDtypeStruct((SUBLANES, t), F32),
                   jax.ShapeDtypeStruct((1, ROUTER_LANES), F32)],
        grid=(t // tm,),
        in_specs=[pl.BlockSpec((tm, d), lambda i: (i, 0)), _resident((1, d)), _resident((d, ROUTER_LANES)),
                  _resident((1, ROUTER_LANES)), _resident((tm, tm))],
        out_specs=[pl.BlockSpec((tm, 1, d + ROUTER_LANES), lambda i: (i, 0, 0)),
                   pl.BlockSpec((SUBLANES, tm), lambda i: (0, i)),
                   pl.BlockSpec((1, ROUTER_LANES), lambda i: (0, 0))],
        scratch_shapes=[pltpu.VMEM((1, ROUTER_LANES), F32)],
        compiler_params=_params("arbitrary"),
        name="moe_router",
    )(h2d, g.reshape(1, d), wr, br, tri)


ITEM_VALID, ITEM_FIRST_OF_TILE, ITEM_NEW_A, ITEM_NEW_B, ITEM_MORE_A, ITEM_MORE_B = 1, 2, 4, 8, 16, 32


def _prefix_sum(v):
    ids = jnp.arange(v.shape[0], dtype=jnp.int32)
    return jnp.sum(jnp.where(ids[None, :] <= ids[:, None], v[None, :], 0), axis=1)


def _sorted_layout(route, counts, n_tiles, n_items):
    cls = route[META_CLASS].astype(jnp.int32)
    rank = route[META_RANK].astype(jnp.int32)
    cnt = counts[0, :N_CLASSES].astype(jnp.int32)
    ids = jnp.arange(N_CLASSES, dtype=jnp.int32)
    end = _prefix_sum(cnt)
    start = end - cnt
    pos = jnp.sum(jnp.where(cls[None, :] == ids[:, None], start[:, None], 0), axis=0) + rank

    first = start // EXPERT_TILE
    last = jnp.where(cnt > 0, (end - 1) // EXPERT_TILE, first - 1)
    items = last - first + 1
    item_end = _prefix_sum(items)
    k = jnp.arange(n_items, dtype=jnp.int32)
    item_cls = jnp.sum((item_end[None, :] <= k[:, None]).astype(jnp.int32), axis=1)
    valid = item_cls < N_CLASSES
    item_cls = jnp.where(valid, item_cls, jnp.max(jnp.where(cnt > 0, ids, 0)))
    sel = item_cls[:, None] == ids[None, :]
    pick = lambda v: jnp.sum(jnp.where(sel, v[None, :], 0), axis=1)
    tile = jnp.where(valid, pick(first) + k - pick(item_end - items), n_tiles - 1)
    lo = jnp.where(valid, jnp.clip(pick(start) - tile * EXPERT_TILE, 0, EXPERT_TILE), 0)
    hi = jnp.where(valid, jnp.clip(pick(end) - tile * EXPERT_TILE, 0, EXPERT_TILE), 0)

    grp, pair = item_cls // len(PAIRS), item_cls % len(PAIRS)
    ea = grp * EXPERTS_PER_GROUP + sum(jnp.where(pair == n, a, 0) for n, (a, _) in enumerate(PAIRS))
    eb = grp * EXPERTS_PER_GROUP + sum(jnp.where(pair == n, b, 0) for n, (_, b) in enumerate(PAIRS))
    prev = lambda v: jnp.concatenate([jnp.full((1,), -1, jnp.int32), v[:-1]])
    new_a, new_b = ea != prev(ea), eb != prev(eb)

    def upcoming(e, is_new):
        later = is_new[None, :] & (k[None, :] > k[:, None])
        nxt = jnp.min(jnp.where(later, k[None, :], n_items), axis=1)
        return jnp.sum(jnp.where(k[None, :] == nxt[:, None], e[None, :], 0), axis=1), nxt < n_items

    next_a, more_a = upcoming(ea, new_a)
    next_b, more_b = upcoming(eb, new_b)
    flags = (jnp.where(valid, ITEM_VALID, 0) + jnp.where(valid & (tile != prev(tile)), ITEM_FIRST_OF_TILE, 0)
             + jnp.where(new_a, ITEM_NEW_A, 0) + jnp.where(new_b, ITEM_NEW_B, 0)
             + jnp.where(more_a, ITEM_MORE_A, 0) + jnp.where(more_b, ITEM_MORE_B, 0))
    return pos, (tile, ea, eb, next_a, next_b, lo, hi, flags)


DISPATCH_CHUNK = 2048
ROW_DMA_UNROLL = 8


def _row_copy(src, dst, i, j, sem):
    return pltpu.make_async_copy(src.at[pl.ds(i, 1)], dst.at[pl.ds(j, 1)], sem)


def _dispatch_kernel(pos_ref, hx_ref, hxs_ref, sem):
    base = pl.program_id(0) * DISPATCH_CHUNK

    def issue(r, carry):
        _row_copy(hx_ref, hxs_ref, r, pos_ref[base + r], sem).start()
        return carry

    lax.fori_loop(0, DISPATCH_CHUNK, issue, 0, unroll=ROW_DMA_UNROLL)
    pltpu.make_async_copy(hx_ref, hxs_ref.at[pl.ds(0, DISPATCH_CHUNK)], sem).wait()


def _dispatch(hx, pos):
    t, _, w = hx.shape
    assert t % DISPATCH_CHUNK == 0
    return pl.pallas_call(
        _dispatch_kernel,
        out_shape=jax.ShapeDtypeStruct((t, 1, w), F32),
        grid_spec=pltpu.PrefetchScalarGridSpec(
            num_scalar_prefetch=1,
            grid=(t // DISPATCH_CHUNK,),
            in_specs=[pl.BlockSpec((DISPATCH_CHUNK, 1, w), lambda i, pos: (i, 0, 0))],
            out_specs=pl.BlockSpec(memory_space=pl.ANY),
            scratch_shapes=[pltpu.SemaphoreType.DMA(())],
        ),
        compiler_params=_params("arbitrary"),
        name="moe_dispatch",
    )(pos, hx)


def _experts_kernel(tile_ref, ea_ref, eb_ref, na_ref, nb_ref, lo_ref, hi_ref, flag_ref, x_ref, wg_hbm, wu_hbm,
                    wd_hbm, y_ref, w1, w2, xs, stage_g, stage_u, stage_d, sems, *, d, hid, layer):
    del tile_ref
    k = pl.program_id(0)
    flags = flag_ref[k]

    def refresh(slot, now_ref, next_ref, new_bit, more_bit):
        def copies(e):
            return [pltpu.make_async_copy(src.at[layer, e], dst.at[slot], sems.at[slot, n])
                    for n, (src, dst) in enumerate(((wg_hbm, stage_g), (wu_hbm, stage_u), (wd_hbm, stage_d)))]

        @pl.when((flags & new_bit) != 0)
        def _():
            @pl.when(k == 0)
            def _():
                for c in copies(now_ref[k]):
                    c.start()

            for c in copies(now_ref[k]):
                c.wait()
            w1[:, 2 * slot * hid:(2 * slot + 1) * hid] = stage_g[slot].astype(BF16)
            w1[:, (2 * slot + 1) * hid:(2 * slot + 2) * hid] = stage_u[slot].astype(BF16)
            w2[slot * hid:(slot + 1) * hid, :] = stage_d[slot].astype(BF16)

            @pl.when((flags & more_bit) != 0)
            def _():
                for c in copies(next_ref[k]):
                    c.start()

    refresh(0, ea_ref, na_ref, ITEM_NEW_A, ITEM_MORE_A)
    refresh(1, eb_ref, nb_ref, ITEM_NEW_B, ITEM_MORE_B)

    @pl.when((flags & ITEM_VALID) != 0)
    def _():
        row = lax.broadcasted_iota(jnp.int32, (EXPERT_TILE, 1), 0)
        inside = (row >= lo_ref[k]) & (row < hi_ref[k])
        xs[...] = x_ref[:, 0, :]
        wa = jnp.where(inside, xs[:, d + META_WA:d + META_WA + 1], 0.0)
        wb = jnp.where(inside, xs[:, d + META_WB:d + META_WB + 1], 0.0)
        gu = jnp.dot(xs[:, :d].astype(BF16), w1[...], preferred_element_type=F32)
        ga, gb = gu[:, 0:hid], gu[:, 2 * hid:3 * hid]
        ha = ga * _sigmoid(ga) * gu[:, hid:2 * hid] * wa
        hb = gb * _sigmoid(gb) * gu[:, 3 * hid:] * wb
        y = jnp.dot(jnp.concatenate([ha, hb], axis=1).astype(BF16), w2[...], preferred_element_type=F32)

        @pl.when((flags & ITEM_FIRST_OF_TILE) != 0)
        def _():
            y_ref[...] = y

        @pl.when((flags & ITEM_FIRST_OF_TILE) == 0)
        def _():
            y_ref[...] += y


def _experts(hxs, items, w_gate, w_up, w_down, layer, *, d):
    t, _, w = hxs.shape
    hid = w_gate.shape[-1]
    n_items = items[0].shape[0]
    hbm = pl.BlockSpec(memory_space=pl.ANY)
    return pl.pallas_call(
        functools.partial(_experts_kernel, d=d, hid=hid, layer=layer),
        out_shape=jax.ShapeDtypeStruct((t, d), F32),
        grid_spec=pltpu.PrefetchScalarGridSpec(
            num_scalar_prefetch=len(items),
            grid=(n_items,),
            in_specs=[pl.BlockSpec((EXPERT_TILE, 1, w), lambda k, tile, *_: (tile[k], 0, 0)), hbm, hbm, hbm],
            out_specs=pl.BlockSpec((EXPERT_TILE, d), lambda k, tile, *_: (tile[k], 0)),
            scratch_shapes=[pltpu.VMEM((d, 4 * hid), BF16), pltpu.VMEM((2 * hid, d), BF16),
                            pltpu.VMEM((EXPERT_TILE, w), F32),
                            pltpu.VMEM((2, d, hid), F32), pltpu.VMEM((2, d, hid), F32), pltpu.VMEM((2, hid, d), F32),
                            pltpu.SemaphoreType.DMA((2, 3))],
        ),
        compiler_params=_params("arbitrary"),
        name="moe_experts",
    )(*items, hxs, w_gate, w_up, w_down)


def _combine_ple_kernel(pos_ref, h_ref, p_ref, g_ref, wg_ref, wp_ref, ys_ref, o_ref, ybuf, sems, *, tm):
    i = pl.program_id(0)
    n = pl.num_programs(0)
    slot = i % 2

    def wait(s):
        pltpu.make_async_copy(ys_ref.at[pl.ds(0, tm)], ybuf.at[s], sems.at[s]).wait()

    @pl.when(i == 0)
    def _():
        def issue(r, carry):
            _row_copy(ys_ref, ybuf.at[0], pos_ref[r], r, sems.at[0]).start()
            return carry
        lax.fori_loop(0, tm, issue, 0, unroll=ROW_DMA_UNROLL)

    nxt = jnp.minimum(i + 1, n - 1) * tm
    for r in range(tm):
        _row_copy(ys_ref, ybuf.at[1 - slot], pos_ref[nxt + r], r, sems.at[1 - slot]).start()

    wait(slot)
    h = h_ref[...] + ybuf[slot]
    gate = _sigmoid(jnp.dot(_rms(h, g_ref[...]).astype(BF16), wg_ref[...], preferred_element_type=F32))
    proj = jnp.dot(p_ref[...].astype(BF16), wp_ref[...], preferred_element_type=F32)
    o_ref[...] = h + gate * proj

    @pl.when(i == n - 1)
    def _():
        wait(1 - slot)


def _combine_ple(h2d, ys, pos, p3d, layer, g, w_gate, w_proj, *, tm=512):
    t, d = h2d.shape
    pd = p3d.shape[-1]
    tile = pl.BlockSpec((tm, d), lambda i, pos: (i, 0))
    res = lambda shape: pl.BlockSpec(shape, lambda i, pos: (0,) * len(shape), pipeline_mode=pl.Buffered(1))
    return pl.pallas_call(
        functools.partial(_combine_ple_kernel, tm=tm),
        out_shape=jax.ShapeDtypeStruct((t, d), F32),
        grid_spec=pltpu.PrefetchScalarGridSpec(
            num_scalar_prefetch=1,
            grid=(t // tm,),
            in_specs=[tile, pl.BlockSpec((None, tm, pd), lambda i, pos: (layer, i, 0)), res((1, d)), res((d, d)),
                      res((pd, d)), pl.BlockSpec(memory_space=pl.ANY)],
            out_specs=tile,
            scratch_shapes=[pltpu.VMEM((2, tm, d), F32), pltpu.SemaphoreType.DMA((2,))],
        ),
        compiler_params=_params("arbitrary"),
        name="moe_combine_ple",
    )(pos, h2d, p3d, g.reshape(1, d), w_gate.astype(BF16), w_proj.astype(BF16), ys)


def _moe_ple(h2d, p3d, layer, g_ffn, w_group, b_group, w_router, b_router, w_gate, w_up, w_down, g_ple,
             ple_gate, ple_proj):
    t, d = h2d.shape
    assert t % EXPERT_TILE == 0
    n_tiles = t // EXPERT_TILE
    n_items = n_tiles + N_CLASSES - 1
    hx, route, counts = _router(h2d, g_ffn, w_group, b_group, w_router, b_router)
    pos, items = _sorted_layout(route, counts, n_tiles, n_items)
    hxs = _dispatch(hx, pos)
    ys = _experts(hxs, items, w_gate, w_up, w_down, layer, d=d)
    return _combine_ple(h2d, ys, pos, p3d, layer, g_ple, ple_gate, ple_proj)


def kernel(x, p, g_mix, g_ffn, g_ple, conv_w_pw1, conv_b_pw1, conv_w_dw, conv_b_dw, conv_ln_g, conv_ln_b, conv_w_pw2, conv_b_pw2, attn_w_qkv, attn_q_gain, attn_k_gain, attn_w_o, moe_w_group, moe_b_group, moe_w_router, moe_b_router, moe_w_gate, moe_w_up, moe_w_down, ple_w_gate, ple_w_proj):
    b, s, d = x.shape
    depth = g_mix.shape[0]
    t = b * s
    h = x
    for i in range(depth):
        if i % 2 == 0:
            c = i // 2
            h = _conv_mixer(h.reshape(b, s, d), g_mix[i], conv_w_pw1[c], conv_b_pw1[c], conv_w_dw[c],
                            conv_b_dw[c], conv_ln_g[c], conv_ln_b[c], conv_w_pw2[c], conv_b_pw2[c])
            h = h.reshape(t, d)
        else:
            a = i // 2
            h = h.reshape(t, d)
            q, k, v = _qkv(h, g_mix[i], attn_w_qkv[a], attn_q_gain[a], attn_k_gain[a])
            o = _moba(q.reshape(b, s, d), k.reshape(b, s, d), v.reshape(b, s, d), n_heads=N_HEADS)
            h = _matmul_residual(h, o.reshape(t, d), attn_w_o[a])
        h = _moe_ple(h, p.reshape(depth, t, -1), i, g_ffn[i], moe_w_group[i], moe_b_group[i], moe_w_router[i],
                     moe_b_router[i], moe_w_gate, moe_w_up, moe_w_down, g_ple[i], ple_w_gate[i], ple_w_proj[i])
    return h.reshape(b, s, d)
```

```python
import functools

import jax
import jax.numpy as jnp
import numpy as np
from jax import lax
from jax.experimental import pallas as pl
from jax.experimental.pallas import tpu as pltpu

F32 = jnp.float32
BF16 = jnp.bfloat16

EPS = 1e-6
NEG_INF = -1e30

N_HEADS = 8
MOBA_BLOCK = 256
MOBA_TOPK = 3
N_GROUPS = 4
EXPERTS_PER_GROUP = 4
N_EXPERTS = N_GROUPS * EXPERTS_PER_GROUP

V7X_VMEM_BYTES = 64 * 1024 * 1024
LANES = 128
SUBLANES = 8
VMEM_LIMIT = V7X_VMEM_BYTES * 7 // 8

ROUTER_LANES = LANES
EXPERT_LANE0 = N_GROUPS

CONV_HALO = 32
CONV_ROWS = 64
CONV_COLS = 128
CONV_PANEL = 256


def _params(*semantics):
    return pltpu.CompilerParams(dimension_semantics=semantics, vmem_limit_bytes=VMEM_LIMIT)


def _resident(shape):
    nd = len(shape)
    return pl.BlockSpec(shape, lambda *_: (0,) * nd, pipeline_mode=pl.Buffered(1))


def _rms(x, g):
    return x * lax.rsqrt(jnp.mean(x * x, axis=-1, keepdims=True) + EPS) * g


def _sigmoid(x):
    return 1.0 / (1.0 + jnp.exp(-x))


def _conv_mixer_kernel(x_ref, g_ref, w1_ref, b1_ref, wdw_ref, bdw_ref, lng_ref, lnb_ref,
                       w2_ref, b2_ref, o_ref, xnbuf, ubuf, cbuf, *, ts, d, kw):
    s = pl.program_id(1)
    x = x_ref[0]
    xnbuf[...] = _rms(x, g_ref[...]).astype(BF16)

    @pl.when(s == 0)
    def _():
        ubuf[0:CONV_HALO, :] = jnp.zeros((CONV_HALO, d), F32)

    def glu_panel(c0):
        a = jnp.dot(xnbuf[...], w1_ref[:, c0:c0 + CONV_PANEL], preferred_element_type=F32)
        a = a + b1_ref[:, c0:c0 + CONV_PANEL]
        g = jnp.dot(xnbuf[...], w1_ref[:, d + c0:d + c0 + CONV_PANEL], preferred_element_type=F32)
        g = g + b1_ref[:, d + c0:d + c0 + CONV_PANEL]
        ubuf[CONV_HALO:CONV_HALO + ts, c0:c0 + CONV_PANEL] = a * _sigmoid(g)

    first = CONV_HALO - (kw - 1)

    def conv_panel(c0):
        for cc in range(c0, c0 + CONV_PANEL, CONV_COLS):
            cols = slice(cc, cc + CONV_COLS)
            for r0 in range(0, ts, CONV_ROWS):
                acc = None
                for j in range(SUBLANES):
                    rows = CONV_ROWS + (SUBLANES if j else 0)
                    sj = None
                    for k in range(kw):
                        if (first + k) % SUBLANES == j:
                            a0 = r0 + (first + k) - j
                            term = ubuf[a0:a0 + rows, cols] * wdw_ref[k:k + 1, cols]
                            sj = term if sj is None else sj + term
                    if sj is not None:
                        sj = sj[j:j + CONV_ROWS]
                        acc = sj if acc is None else acc + sj
                cbuf[r0:r0 + CONV_ROWS, cols] = acc
        ubuf[0:CONV_HALO, c0:c0 + CONV_PANEL] = ubuf[ts:ts + CONV_HALO, c0:c0 + CONV_PANEL]

    glu_panel(0)
    for c0 in range(0, d, CONV_PANEL):
        if c0 + CONV_PANEL < d:
            glu_panel(c0 + CONV_PANEL)
        conv_panel(c0)

    c = cbuf[...] + bdw_ref[...]
    mu = jnp.mean(c, axis=-1, keepdims=True)
    xc = c - mu
    y = xc * lax.rsqrt(jnp.mean(xc * xc, axis=-1, keepdims=True) + EPS) * lng_ref[...] + lnb_ref[...]
    y = y * _sigmoid(y)
    o_ref[0] = x + jnp.dot(y.astype(BF16), w2_ref[...], preferred_element_type=F32) + b2_ref[...]


def _conv_mixer(h, g, w1, b1, wdw, bdw, lng, lnb, w2, b2, *, ts=512):
    b, s, d = h.shape
    kw = wdw.shape[0]
    assert s % ts == 0 and ts % CONV_ROWS == 0 and ts >= CONV_HALO >= kw - 1
    assert CONV_HALO % SUBLANES == 0 and d % CONV_PANEL == 0 and CONV_PANEL % CONV_COLS == 0
    row = lambda a: a.reshape(1, -1)
    kern = functools.partial(_conv_mixer_kernel, ts=ts, d=d, kw=kw)
    return pl.pallas_call(
        kern,
        out_shape=jax.ShapeDtypeStruct((b, s, d), F32),
        grid=(b, s // ts),
        in_specs=[
            pl.BlockSpec((1, ts, d), lambda i, j: (i, j, 0)),
            _resident((1, d)), _resident((d, 2 * d)), _resident((1, 2 * d)),
            _resident((kw, d)), _resident((1, d)), _resident((1, d)), _resident((1, d)),
            _resident((d, d)), _resident((1, d)),
        ],
        out_specs=pl.BlockSpec((1, ts, d), lambda i, j: (i, j, 0)),
        scratch_shapes=[pltpu.VMEM((ts, d), BF16), pltpu.VMEM((ts + CONV_HALO, d), F32), pltpu.VMEM((ts, d), F32)],
        compiler_params=_params("arbitrary", "arbitrary"),
        name="conv_mixer",
    )(h, row(g), w1.astype(BF16), row(b1), wdw, row(bdw), row(lng), row(lnb), w2.astype(BF16), row(b2))


def _qkv_kernel(x_ref, g_ref, w_ref, qg_ref, kg_ref, q_ref, k_ref, v_ref, *, d, hd):
    xn = _rms(x_ref[...], g_ref[...]).astype(BF16)
    qkv = jnp.dot(xn, w_ref[...], preferred_element_type=F32)
    for h in range(d // hd):
        cols = slice(h * hd, (h + 1) * hd)
        q_ref[:, cols] = _rms(qkv[:, h * hd:(h + 1) * hd], qg_ref[...]).astype(BF16)
        k_ref[:, cols] = _rms(qkv[:, d + h * hd:d + (h + 1) * hd], kg_ref[...]).astype(BF16)
    v_ref[...] = qkv[:, 2 * d:].astype(BF16)


def _qkv(h2d, g, w, qg, kg, *, tm=512):
    t, d = h2d.shape
    hd = qg.shape[0]
    kern = functools.partial(_qkv_kernel, d=d, hd=hd)
    tile = pl.BlockSpec((tm, d), lambda i: (i, 0))
    return pl.pallas_call(
        kern,
        out_shape=[jax.ShapeDtypeStruct((t, d), BF16)] * 3,
        grid=(t // tm,),
        in_specs=[tile, _resident((1, d)), _resident((d, 3 * d)), _resident((1, hd)), _resident((1, hd))],
        out_specs=[tile, tile, tile],
        compiler_params=_params("arbitrary"),
        name="attn_qkv",
    )(h2d, g.reshape(1, d), w.astype(BF16), qg.reshape(1, hd), kg.reshape(1, hd))


def _moba_kernel(slope_ref, q_ref, k_ref, v_ref, kaux_ref, vaux_ref, o_ref, kaug, vaug, *bufs,
                 nb, blk, hd, topk):
    scale = hd ** -0.5
    negz = NEG_INF / scale
    slope = slope_ref[0][:, :1]
    kaug[:, :hd] = k_ref[0]
    kaug[:, hd:] = kaux_ref[...]
    vaug[:, :hd] = v_ref[0]
    vaug[:, hd:] = vaux_ref[...]

    km = jnp.concatenate([jnp.mean(k_ref[0, n * blk:(n + 1) * blk, :].astype(F32), axis=0, keepdims=True)
                          for n in range(nb)], axis=0)
    hi = km.astype(BF16).astype(F32)
    mid = (km - hi).astype(BF16).astype(F32)
    lo = (km - hi - mid).astype(BF16).astype(F32)
    pad = [jnp.zeros((-3 * nb % (2 * SUBLANES), hd), F32)] if 3 * nb % (2 * SUBLANES) else []
    km3 = jnp.concatenate([hi, mid, lo] + pad, axis=0).astype(BF16)

    r = lax.broadcasted_iota(jnp.int32, (blk, blk), 0)
    c = lax.broadcasted_iota(jnp.int32, (blk, blk), 1)
    causal = r >= c
    keypos = lax.broadcasted_iota(jnp.int32, (1, blk), 1).astype(F32)
    nt = (((1,), (1,)), ((), ()))

    def scores(j):
        zbuf = bufs[j]
        q = q_ref[0, j * blk:(j + 1) * blk, :]
        select = j > topk
        if select:
            g3 = lax.dot_general(km3, q, nt, preferred_element_type=F32)
            gate = g3[0:nb] + g3[nb:2 * nb] + g3[2 * nb:3 * nb]
            g = [gate[m:m + 1, :] for m in range(j)]
            rows = []
            for n in range(j):
                beaten = jnp.zeros((1, blk), F32)
                for m in range(j):
                    if m != n:
                        ahead = (g[m] > g[n]) | (g[m] == g[n]) if m < n else (g[m] > g[n])
                        beaten = beaten + jnp.where(ahead, 1.0, 0.0)
                rows.append(jnp.where(beaten < topk, 0.0, negz))
            bias_t = jnp.concatenate(rows + [jnp.zeros((LANES - j, blk), F32)], axis=0)
            q = jnp.concatenate([q, bias_t.T.astype(BF16)], axis=1)
        mrun = None
        for n in range(j + 1):
            keys = kaug[n * blk:(n + 1) * blk, :] if select else k_ref[0, n * blk:(n + 1) * blk, :]
            z = lax.dot_general(q, keys, nt, preferred_element_type=F32)
            u = z * scale + slope * (keypos + float(n * blk))
            if n == j:
                u = jnp.where(causal, u, NEG_INF)
            zbuf[:, n * blk:(n + 1) * blk] = u
            for half in range(blk // LANES):
                f = u[:, half * LANES:(half + 1) * LANES]
                mrun = f if mrun is None else jnp.maximum(mrun, f)
        return jnp.broadcast_to(jnp.max(mrun, axis=-1, keepdims=True), (blk, LANES))

    def attend(j, mb):
        zbuf, pbuf = bufs[j], bufs[nb + j]
        for n in range(j + 1):
            for half in range(blk // LANES):
                cols = slice(n * blk + half * LANES, n * blk + (half + 1) * LANES)
                pbuf[:, cols] = jnp.exp(zbuf[:, cols] - mb).astype(BF16)
        acc = jnp.dot(pbuf[:, :(j + 1) * blk], vaug[:(j + 1) * blk, :], preferred_element_type=F32)
        o_ref[0, j * blk:(j + 1) * blk, :] = (acc[:, :hd] / acc[:, hd:hd + 1]).astype(BF16)

    order = list(range(nb - 1, -1, -1))
    mb_next = scores(order[0])
    for i, j in enumerate(order):
        mb = mb_next
        if i + 1 < nb:
            mb_next = scores(order[i + 1])
        attend(j, mb)


def _moba(q, k, v, *, n_heads):
    b, s, d = q.shape
    hd = d // n_heads
    blk = MOBA_BLOCK
    assert s % blk == 0 and hd == LANES and blk % LANES == 0
    nb = s // blk
    assert nb <= LANES
    topk = max(1, min(MOBA_TOPK, nb - 1))
    slopes = jnp.exp2(-8.0 * jnp.arange(1, n_heads + 1, dtype=F32) / n_heads)
    slopes = jnp.broadcast_to(slopes[:, None, None], (n_heads, 1, LANES))
    kaux = np.zeros((s, LANES), np.float32)
    kaux[np.arange(s), np.arange(s) // blk] = 1.0
    vaux = np.zeros((s, LANES), np.float32)
    vaux[:, 0] = 1.0
    kern = functools.partial(_moba_kernel, nb=nb, blk=blk, hd=hd, topk=topk)
    head = pl.BlockSpec((1, s, hd), lambda i, h: (i, 0, h))
    return pl.pallas_call(
        kern,
        out_shape=jax.ShapeDtypeStruct((b, s, d), BF16),
        grid=(b, n_heads),
        in_specs=[pl.BlockSpec((1, 1, LANES), lambda i, h: (h, 0, 0)), head, head, head,
                  _resident((s, LANES)), _resident((s, LANES))],
        out_specs=head,
        scratch_shapes=([pltpu.VMEM((s, hd + LANES), BF16), pltpu.VMEM((s, hd + LANES), BF16)]
                        + [pltpu.VMEM((blk, (j + 1) * blk), F32) for j in range(nb)]
                        + [pltpu.VMEM((blk, (j + 1) * blk), BF16) for j in range(nb)]),
        compiler_params=_params("arbitrary", "arbitrary"),
        name="moba_attn",
    )(slopes, q, k, v, jnp.asarray(kaux, BF16), jnp.asarray(vaux, BF16))


def _matmul_residual_kernel(h_ref, a_ref, w_ref, o_ref):
    o_ref[...] = h_ref[...] + jnp.dot(a_ref[...], w_ref[...], preferred_element_type=F32)


def _matmul_residual(h2d, a, w, *, tm=512):
    t, d = h2d.shape
    tile = pl.BlockSpec((tm, d), lambda i: (i, 0))
    return pl.pallas_call(
        _matmul_residual_kernel,
        out_shape=jax.ShapeDtypeStruct((t, d), F32),
        grid=(t // tm,),
        in_specs=[tile, pl.BlockSpec((tm, a.shape[1]), lambda i: (i, 0)), _resident(w.shape)],
        out_specs=tile,
        compiler_params=_params("arbitrary"),
        name="attn_out_proj",
    )(h2d, a, w.astype(BF16))


def _pair_order(n):
    todo = [(a, b) for a in range(n) for b in range(a + 1, n)]
    order = [todo.pop(0)]
    while todo:
        a, b = order[-1]
        nxt = next((p for p in todo if p[0] == a or p[1] == b), todo[0])
        todo.remove(nxt)
        order.append(nxt)
    return order


PAIRS = _pair_order(EXPERTS_PER_GROUP)
N_CLASSES = N_GROUPS * len(PAIRS)
META_CLASS, META_RANK, META_WA, META_WB = 0, 1, 2, 3
EXPERT_TILE = 256


def _router_kernel(h_ref, g_ref, wr_ref, br_ref, tri_ref, route_ref, cnt_ref, running):
    @pl.when(pl.program_id(0) == 0)
    def _():
        running[...] = jnp.zeros_like(running)

    xn = _rms(h_ref[...], g_ref[...])
    wr = wr_ref[...]
    xh, wh = xn.astype(BF16), wr.astype(BF16)
    xl, wl = (xn - xh.astype(F32)).astype(BF16), (wr - wh.astype(F32)).astype(BF16)
    logits = (jnp.dot(xh, wh, preferred_element_type=F32) + jnp.dot(xl, wh, preferred_element_type=F32)
              + jnp.dot(xh, wl, preferred_element_type=F32)) + br_ref[...]
    lane = lax.broadcasted_iota(jnp.int32, logits.shape, 1).astype(F32)

    def first_max(vals):
        mx = jnp.max(vals, axis=-1, keepdims=True)
        return mx, jnp.min(jnp.where(vals == mx, lane, float(ROUTER_LANES)), axis=-1, keepdims=True)

    is_g = lane < N_GROUPS
    gmax, gidx = first_max(jnp.where(is_g, logits, -jnp.inf))
    zg = jnp.sum(jnp.where(is_g, jnp.exp(logits - gmax), 0.0), axis=-1, keepdims=True)
    g_w = 1.0 / zg
    lo = EXPERT_LANE0 + gidx * EXPERTS_PER_GROUP
    in_grp = (lane >= lo) & (lane < lo + EXPERTS_PER_GROUP)
    el = jnp.where(in_grp, logits, -jnp.inf)
    m1, i1 = first_max(el)
    m2, i2 = first_max(jnp.where(lane == i1, -jnp.inf, el))
    p2 = jnp.exp(m2 - m1)
    w1 = g_w / (1.0 + p2)
    w2 = g_w * p2 / (1.0 + p2)
    e1, e2 = i1 - lo, i2 - lo
    ea, eb = jnp.minimum(e1, e2), jnp.maximum(e1, e2)
    wa, wb = jnp.where(e1 < e2, w1, w2), jnp.where(e1 < e2, w2, w1)
    pair = sum(jnp.where((ea == a) & (eb == b), float(n), 0.0) for n, (a, b) in enumerate(PAIRS))
    cls = gidx * float(len(PAIRS)) + pair
    onehot = jnp.where(lane == cls, 1.0, 0.0)
    before = jnp.dot(tri_ref[...], onehot.astype(BF16), preferred_element_type=F32) + running[...]
    rank = jnp.sum(onehot * before, axis=-1, keepdims=True)
    running[...] += jnp.sum(onehot, axis=0, keepdims=True)
    cnt_ref[...] = running[...]

    meta = (jnp.where(lane == META_CLASS, cls, 0.0) + jnp.where(lane == META_RANK, rank, 0.0)
            + jnp.where(lane == META_WA, wa, 0.0) + jnp.where(lane == META_WB, wb, 0.0))
    route_ref[...] = meta.T[:SUBLANES, :]


def _router(h2d, g, w_group, b_group, w_router, b_router, *, tm=512):
    t, d = h2d.shape
    ng, _, ne = w_router.shape
    assert (ng, ne) == (N_GROUPS, EXPERTS_PER_GROUP) and N_CLASSES <= ROUTER_LANES
    wr = jnp.concatenate([w_group, jnp.transpose(w_router, (1, 0, 2)).reshape(d, ng * ne)], axis=1)
    br = jnp.concatenate([b_group, b_router.reshape(ng * ne)])
    pad = ROUTER_LANES - wr.shape[1]
    wr = jnp.pad(wr, ((0, 0), (0, pad)))
    br = jnp.pad(br, (0, pad)).reshape(1, ROUTER_LANES)
    tri = jnp.asarray(np.tril(np.ones((tm, tm), np.float32), -1), BF16)
    return pl.pallas_call(
        _router_kernel,
        out_shape=[jax.ShapeDtypeStruct((SUBLANES, t), F32), jax.ShapeDtypeStruct((1, ROUTER_LANES), F32)],
        grid=(t // tm,),
        in_specs=[pl.BlockSpec((tm, d), lambda i: (i, 0)), _resident((1, d)), _resident((d, ROUTER_LANES)),
                  _resident((1, ROUTER_LANES)), _resident((tm, tm))],
        out_specs=[pl.BlockSpec((SUBLANES, tm), lambda i: (0, i)),
                   pl.BlockSpec((1, ROUTER_LANES), lambda i: (0, 0))],
        scratch_shapes=[pltpu.VMEM((1, ROUTER_LANES), F32)],
        compiler_params=_params("arbitrary"),
        name="moe_router",
    )(h2d, g.reshape(1, d), wr, br, tri)


ITEM_VALID, ITEM_FIRST_OF_TILE, ITEM_NEW_A, ITEM_NEW_B, ITEM_MORE_A, ITEM_MORE_B = 1, 2, 4, 8, 16, 32


def _prefix_sum(v):
    ids = jnp.arange(v.shape[0], dtype=jnp.int32)
    return jnp.sum(jnp.where(ids[None, :] <= ids[:, None], v[None, :], 0), axis=1)


def _sorted_layout(route, counts, n_tiles, n_items):
    cls = route[META_CLASS].astype(jnp.int32)
    rank = route[META_RANK].astype(jnp.int32)
    cnt = counts[0, :N_CLASSES].astype(jnp.int32)
    ids = jnp.arange(N_CLASSES, dtype=jnp.int32)
    end = _prefix_sum(cnt)
    start = end - cnt
    pos = jnp.sum(jnp.where(cls[None, :] == ids[:, None], start[:, None], 0), axis=0) + rank

    first = start // EXPERT_TILE
    last = jnp.where(cnt > 0, (end - 1) // EXPERT_TILE, first - 1)
    items = last - first + 1
    item_end = _prefix_sum(items)
    k = jnp.arange(n_items, dtype=jnp.int32)
    item_cls = jnp.sum((item_end[None, :] <= k[:, None]).astype(jnp.int32), axis=1)
    valid = item_cls < N_CLASSES
    item_cls = jnp.where(valid, item_cls, jnp.max(jnp.where(cnt > 0, ids, 0)))
    sel = item_cls[:, None] == ids[None, :]
    pick = lambda v: jnp.sum(jnp.where(sel, v[None, :], 0), axis=1)
    tile = jnp.where(valid, pick(first) + k - pick(item_end - items), n_tiles - 1)
    lo = jnp.where(valid, jnp.clip(pick(start) - tile * EXPERT_TILE, 0, EXPERT_TILE), 0)
    hi = jnp.where(valid, jnp.clip(pick(end) - tile * EXPERT_TILE, 0, EXPERT_TILE), 0)

    grp, pair = item_cls // len(PAIRS), item_cls % len(PAIRS)
    ea = grp * EXPERTS_PER_GROUP + sum(jnp.where(pair == n, a, 0) for n, (a, _) in enumerate(PAIRS))
    eb = grp * EXPERTS_PER_GROUP + sum(jnp.where(pair == n, b, 0) for n, (_, b) in enumerate(PAIRS))
    prev = lambda v: jnp.concatenate([jnp.full((1,), -1, jnp.int32), v[:-1]])
    new_a, new_b = ea != prev(ea), eb != prev(eb)

    def upcoming(e, is_new):
        later = is_new[None, :] & (k[None, :] > k[:, None])
        nxt = jnp.min(jnp.where(later, k[None, :], n_items), axis=1)
        return jnp.sum(jnp.where(k[None, :] == nxt[:, None], e[None, :], 0), axis=1), nxt < n_items

    next_a, more_a = upcoming(ea, new_a)
    next_b, more_b = upcoming(eb, new_b)
    flags = (jnp.where(valid, ITEM_VALID, 0) + jnp.where(valid & (tile != prev(tile)), ITEM_FIRST_OF_TILE, 0)
             + jnp.where(new_a, ITEM_NEW_A, 0) + jnp.where(new_b, ITEM_NEW_B, 0)
             + jnp.where(more_a, ITEM_MORE_A, 0) + jnp.where(more_b, ITEM_MORE_B, 0))
    return pos, (tile, ea, eb, next_a, next_b, lo, hi, flags)


DISPATCH_CHUNK = 2048
ROW_DMA_UNROLL = 8


def _row_copy(src, dst, i, j, sem):
    return pltpu.make_async_copy(src.at[pl.ds(i, 1)], dst.at[pl.ds(j, 1)], sem)


DISPATCH_ROWS = 256


def _dispatch_kernel(pos_ref, h_ref, g_ref, route_ref, hxs_ref, rows, sems, *, d):
    s = pl.program_id(0)
    n = pl.num_programs(0) - 1
    prev = (s + 1) % 2

    @pl.when(s > 0)
    def _():
        base = (s - 1) * DISPATCH_CHUNK

        def issue(r, carry):
            _row_copy(rows.at[prev], hxs_ref, r, pos_ref[base + r], sems.at[prev]).start()
            return carry

        lax.fori_loop(0, DISPATCH_CHUNK, issue, 0, unroll=ROW_DMA_UNROLL)

    @pl.when(s < n)
    def _():
        def build(i, carry):
            r0 = pl.multiple_of(i * DISPATCH_ROWS, DISPATCH_ROWS)
            rec = route_ref[:, pl.ds(r0, DISPATCH_ROWS)]
            pad = jnp.zeros((LANES - SUBLANES, DISPATCH_ROWS), F32)
            rows[s % 2, pl.ds(r0, DISPATCH_ROWS), 0, :d] = _rms(h_ref[pl.ds(r0, DISPATCH_ROWS), :], g_ref[...])
            rows[s % 2, pl.ds(r0, DISPATCH_ROWS), 0, d:] = jnp.concatenate([rec, pad], axis=0).T
            return carry

        lax.fori_loop(0, DISPATCH_CHUNK // DISPATCH_ROWS, build, 0)

    @pl.when(s > 0)
    def _():
        pltpu.make_async_copy(rows.at[prev], hxs_ref.at[pl.ds(0, DISPATCH_CHUNK)], sems.at[prev]).wait()


def _dispatch(h2d, g, route, pos):
    t, d = h2d.shape
    w = d + ROUTER_LANES
    assert t % DISPATCH_CHUNK == 0 and DISPATCH_CHUNK % DISPATCH_ROWS == 0
    n = t // DISPATCH_CHUNK
    return pl.pallas_call(
        functools.partial(_dispatch_kernel, d=d),
        out_shape=jax.ShapeDtypeStruct((t, 1, w), F32),
        grid_spec=pltpu.PrefetchScalarGridSpec(
            num_scalar_prefetch=1,
            grid=(n + 1,),
            in_specs=[pl.BlockSpec((DISPATCH_CHUNK, d), lambda s, pos: (jnp.minimum(s, n - 1), 0)),
                      pl.BlockSpec((1, d), lambda s, pos: (0, 0)),
                      pl.BlockSpec((SUBLANES, DISPATCH_CHUNK), lambda s, pos: (0, jnp.minimum(s, n - 1)))],
            out_specs=pl.BlockSpec(memory_space=pl.ANY),
            scratch_shapes=[pltpu.VMEM((2, DISPATCH_CHUNK, 1, w), F32), pltpu.SemaphoreType.DMA((2,))],
        ),
        compiler_params=_params("arbitrary"),
        name="moe_dispatch",
    )(pos, h2d, g.reshape(1, d), route)


def _experts_kernel(tile_ref, ea_ref, eb_ref, na_ref, nb_ref, lo_ref, hi_ref, flag_ref, x_ref, wg_hbm, wu_hbm,
                    wd_hbm, y_ref, w1, w2, xs, stage_g, stage_u, stage_d, sems, *, d, hid, layer):
    del tile_ref
    k = pl.program_id(0)
    flags = flag_ref[k]

    def refresh(slot, now_ref, next_ref, new_bit, more_bit):
        def copies(e):
            return [pltpu.make_async_copy(src.at[layer, e], dst.at[slot], sems.at[slot, n])
                    for n, (src, dst) in enumerate(((wg_hbm, stage_g), (wu_hbm, stage_u), (wd_hbm, stage_d)))]

        @pl.when((flags & new_bit) != 0)
        def _():
            @pl.when(k == 0)
            def _():
                for c in copies(now_ref[k]):
                    c.start()

            for c in copies(now_ref[k]):
                c.wait()
            w1[:, 2 * slot * hid:(2 * slot + 1) * hid] = stage_g[slot].astype(BF16)
            w1[:, (2 * slot + 1) * hid:(2 * slot + 2) * hid] = stage_u[slot].astype(BF16)
            w2[slot * hid:(slot + 1) * hid, :] = stage_d[slot].astype(BF16)

            @pl.when((flags & more_bit) != 0)
            def _():
                for c in copies(next_ref[k]):
                    c.start()

    refresh(0, ea_ref, na_ref, ITEM_NEW_A, ITEM_MORE_A)
    refresh(1, eb_ref, nb_ref, ITEM_NEW_B, ITEM_MORE_B)

    @pl.when((flags & ITEM_VALID) != 0)
    def _():
        row = lax.broadcasted_iota(jnp.int32, (EXPERT_TILE, 1), 0)
        inside = (row >= lo_ref[k]) & (row < hi_ref[k])
        xs[...] = x_ref[:, 0, :]
        wa = jnp.where(inside, xs[:, d + META_WA:d + META_WA + 1], 0.0)
        wb = jnp.where(inside, xs[:, d + META_WB:d + META_WB + 1], 0.0)
        gu = jnp.dot(xs[:, :d].astype(BF16), w1[...], preferred_element_type=F32)
        ga, gb = gu[:, 0:hid], gu[:, 2 * hid:3 * hid]
        ha = ga * _sigmoid(ga) * gu[:, hid:2 * hid] * wa
        hb = gb * _sigmoid(gb) * gu[:, 3 * hid:] * wb
        y = jnp.dot(jnp.concatenate([ha, hb], axis=1).astype(BF16), w2[...], preferred_element_type=F32)

        @pl.when((flags & ITEM_FIRST_OF_TILE) != 0)
        def _():
            y_ref[...] = y

        @pl.when((flags & ITEM_FIRST_OF_TILE) == 0)
        def _():
            y_ref[...] += y


def _experts(hxs, items, w_gate, w_up, w_down, layer, *, d):
    t, _, w = hxs.shape
    hid = w_gate.shape[-1]
    n_items = items[0].shape[0]
    hbm = pl.BlockSpec(memory_space=pl.ANY)
    return pl.pallas_call(
        functools.partial(_experts_kernel, d=d, hid=hid, layer=layer),
        out_shape=jax.ShapeDtypeStruct((t, d), F32),
        grid_spec=pltpu.PrefetchScalarGridSpec(
            num_scalar_prefetch=len(items),
            grid=(n_items,),
            in_specs=[pl.BlockSpec((EXPERT_TILE, 1, w), lambda k, tile, *_: (tile[k], 0, 0)), hbm, hbm, hbm],
            out_specs=pl.BlockSpec((EXPERT_TILE, d), lambda k, tile, *_: (tile[k], 0)),
            scratch_shapes=[pltpu.VMEM((d, 4 * hid), BF16), pltpu.VMEM((2 * hid, d), BF16),
                            pltpu.VMEM((EXPERT_TILE, w), F32),
                            pltpu.VMEM((2, d, hid), F32), pltpu.VMEM((2, d, hid), F32), pltpu.VMEM((2, hid, d), F32),
                            pltpu.SemaphoreType.DMA((2, 3))],
        ),
        compiler_params=_params("arbitrary"),
        name="moe_experts",
    )(*items, hxs, w_gate, w_up, w_down)


def _combine_ple_kernel(pos_ref, h_ref, p_ref, g_ref, wg_ref, wp_ref, ys_ref, o_ref, ybuf, sems, *, tm):
    i = pl.program_id(0)
    n = pl.num_programs(0)
    slot = i % 2

    def wait(s):
        pltpu.make_async_copy(ys_ref.at[pl.ds(0, tm)], ybuf.at[s], sems.at[s]).wait()

    @pl.when(i == 0)
    def _():
        def issue(r, carry):
            _row_copy(ys_ref, ybuf.at[0], pos_ref[r], r, sems.at[0]).start()
            return carry
        lax.fori_loop(0, tm, issue, 0, unroll=ROW_DMA_UNROLL)

    nxt = jnp.minimum(i + 1, n - 1) * tm
    for r in range(tm):
        _row_copy(ys_ref, ybuf.at[1 - slot], pos_ref[nxt + r], r, sems.at[1 - slot]).start()

    wait(slot)
    h = h_ref[...] + ybuf[slot]
    gate = _sigmoid(jnp.dot(_rms(h, g_ref[...]).astype(BF16), wg_ref[...], preferred_element_type=F32))
    proj = jnp.dot(p_ref[...].astype(BF16), wp_ref[...], preferred_element_type=F32)
    o_ref[...] = h + gate * proj

    @pl.when(i == n - 1)
    def _():
        wait(1 - slot)


def _combine_ple(h2d, ys, pos, p3d, layer, g, w_gate, w_proj, *, tm=512):
    t, d = h2d.shape
    pd = p3d.shape[-1]
    tile = pl.BlockSpec((tm, d), lambda i, pos: (i, 0))
    res = lambda shape: pl.BlockSpec(shape, lambda i, pos: (0,) * len(shape), pipeline_mode=pl.Buffered(1))
    return pl.pallas_call(
        functools.partial(_combine_ple_kernel, tm=tm),
        out_shape=jax.ShapeDtypeStruct((t, d), F32),
        grid_spec=pltpu.PrefetchScalarGridSpec(
            num_scalar_prefetch=1,
            grid=(t // tm,),
            in_specs=[tile, pl.BlockSpec((None, tm, pd), lambda i, pos: (layer, i, 0)), res((1, d)), res((d, d)),
                      res((pd, d)), pl.BlockSpec(memory_space=pl.ANY)],
            out_specs=tile,
            scratch_shapes=[pltpu.VMEM((2, tm, d), F32), pltpu.SemaphoreType.DMA((2,))],
        ),
        compiler_params=_params("arbitrary"),
        name="moe_combine_ple",
    )(pos, h2d, p3d, g.reshape(1, d), w_gate.astype(BF16), w_proj.astype(BF16), ys)


def _moe_ple(h2d, p3d, layer, g_ffn, w_group, b_group, w_router, b_router, w_gate, w_up, w_down, g_ple,
             ple_gate, ple_proj):
    t, d = h2d.shape
    assert t % EXPERT_TILE == 0
    n_tiles = t // EXPERT_TILE
    n_items = n_tiles + N_CLASSES - 1
    route, counts = _router(h2d, g_ffn, w_group, b_group, w_router, b_router)
    pos, items = _sorted_layout(route, counts, n_tiles, n_items)
    hxs = _dispatch(h2d, g_ffn, route, pos)
    ys = _experts(hxs, items, w_gate, w_up, w_down, layer, d=d)
    return _combine_ple(h2d, ys, pos, p3d, layer, g_ple, ple_gate, ple_proj)


def kernel(x, p, g_mix, g_ffn, g_ple, conv_w_pw1, conv_b_pw1, conv_w_dw, conv_b_dw, conv_ln_g, conv_ln_b, conv_w_pw2, conv_b_pw2, attn_w_qkv, attn_q_gain, attn_k_gain, attn_w_o, moe_w_group, moe_b_group, moe_w_router, moe_b_router, moe_w_gate, moe_w_up, moe_w_down, ple_w_gate, ple_w_proj):
    b, s, d = x.shape
    depth = g_mix.shape[0]
    t = b * s
    h = x
    for i in range(depth):
        if i % 2 == 0:
            c = i // 2
            h = _conv_mixer(h.reshape(b, s, d), g_mix[i], conv_w_pw1[c], conv_b_pw1[c], conv_w_dw[c],
                            conv_b_dw[c], conv_ln_g[c], conv_ln_b[c], conv_w_pw2[c], conv_b_pw2[c])
            h = h.reshape(t, d)
        else:
            a = i // 2
            h = h.reshape(t, d)
            q, k, v = _qkv(h, g_mix[i], attn_w_qkv[a], attn_q_gain[a], attn_k_gain[a])
            o = _moba(q.reshape(b, s, d), k.reshape(b, s, d), v.reshape(b, s, d), n_heads=N_HEADS)
            h = _matmul_residual(h, o.reshape(t, d), attn_w_o[a])
        h = _moe_ple(h, p.reshape(depth, t, -1), i, g_ffn[i], moe_w_group[i], moe_b_group[i], moe_w_router[i],
                     moe_b_router[i], moe_w_gate, moe_w_up, moe_w_down, g_ple[i], ple_w_gate[i], ple_w_proj[i])
    return h.reshape(b, s, d)
```

```python
import functools

import jax
import jax.numpy as jnp
import numpy as np
from jax import lax
from jax.experimental import pallas as pl
from jax.experimental.pallas import tpu as pltpu

F32 = jnp.float32
BF16 = jnp.bfloat16

EPS = 1e-6
NEG_INF = -1e30

N_HEADS = 8
MOBA_BLOCK = 256
MOBA_TOPK = 3
N_GROUPS = 4
EXPERTS_PER_GROUP = 4
N_EXPERTS = N_GROUPS * EXPERTS_PER_GROUP

V7X_VMEM_BYTES = 64 * 1024 * 1024
LANES = 128
SUBLANES = 8
VMEM_LIMIT = V7X_VMEM_BYTES * 7 // 8

ROUTER_LANES = LANES
EXPERT_LANE0 = N_GROUPS

CONV_HALO = 32
CONV_ROWS = 64
CONV_COLS = 128
CONV_SHIFT = 4
CONV_PANEL = 256


def _params(*semantics):
    return pltpu.CompilerParams(dimension_semantics=semantics, vmem_limit_bytes=VMEM_LIMIT)


def _resident(shape):
    nd = len(shape)
    return pl.BlockSpec(shape, lambda *_: (0,) * nd, pipeline_mode=pl.Buffered(1))


def _rms(x, g):
    return x * lax.rsqrt(jnp.mean(x * x, axis=-1, keepdims=True) + EPS) * g


def _sigmoid(x):
    return 1.0 / (1.0 + jnp.exp(-x))


def _conv_mixer_kernel(x_ref, g_ref, w1_ref, b1_ref, wdw_ref, bdw_ref, lng_ref, lnb_ref,
                       w2_ref, b2_ref, o_ref, xnbuf, ubuf, ubuf4, cbuf, *, ts, d, kw):
    s = pl.program_id(1)
    x = x_ref[0]
    xnbuf[...] = _rms(x, g_ref[...]).astype(BF16)

    @pl.when(s == 0)
    def _():
        ubuf[0:CONV_HALO, :] = jnp.zeros((CONV_HALO, d), F32)
        ubuf4[0:CONV_HALO, :] = jnp.zeros((CONV_HALO, d), F32)
        ubuf4[ts + CONV_HALO - SUBLANES:ts + CONV_HALO, :] = jnp.zeros((SUBLANES, d), F32)

    def glu_panel(c0):
        a = jnp.dot(xnbuf[...], w1_ref[:, c0:c0 + CONV_PANEL], preferred_element_type=F32)
        a = a + b1_ref[:, c0:c0 + CONV_PANEL]
        g = jnp.dot(xnbuf[...], w1_ref[:, d + c0:d + c0 + CONV_PANEL], preferred_element_type=F32)
        g = g + b1_ref[:, d + c0:d + c0 + CONV_PANEL]
        u = a * _sigmoid(g)
        ubuf[CONV_HALO:CONV_HALO + ts, c0:c0 + CONV_PANEL] = u
        ubuf4[CONV_HALO - CONV_SHIFT:CONV_HALO - CONV_SHIFT + ts, c0:c0 + CONV_PANEL] = u

    first = CONV_HALO - (kw - 1)

    def conv_panel(c0):
        for cc in range(c0, c0 + CONV_PANEL, CONV_COLS):
            cols = slice(cc, cc + CONV_COLS)
            for r0 in range(0, ts, CONV_ROWS):
                acc = None
                for r in range(CONV_SHIFT):
                    rows = CONV_ROWS + (SUBLANES if r else 0)
                    part = None
                    for k in range(kw):
                        j = (first + k) % SUBLANES
                        if j % CONV_SHIFT == r:
                            src = ubuf if j < CONV_SHIFT else ubuf4
                            a0 = r0 + (first + k) - j
                            term = src[a0:a0 + rows, cols] * wdw_ref[k:k + 1, cols]
                            part = term if part is None else part + term
                    if part is not None:
                        part = part[r:r + CONV_ROWS]
                        acc = part if acc is None else acc + part
                cbuf[r0:r0 + CONV_ROWS, cols] = acc
        ubuf[0:CONV_HALO, c0:c0 + CONV_PANEL] = ubuf[ts:ts + CONV_HALO, c0:c0 + CONV_PANEL]
        ubuf4[0:CONV_HALO, c0:c0 + CONV_PANEL] = ubuf4[ts:ts + CONV_HALO, c0:c0 + CONV_PANEL]

    glu_panel(0)
    for c0 in range(0, d, CONV_PANEL):
        if c0 + CONV_PANEL < d:
            glu_panel(c0 + CONV_PANEL)
        conv_panel(c0)

    c = cbuf[...] + bdw_ref[...]
    mu = jnp.mean(c, axis=-1, keepdims=True)
    xc = c - mu
    y = xc * lax.rsqrt(jnp.mean(xc * xc, axis=-1, keepdims=True) + EPS) * lng_ref[...] + lnb_ref[...]
    y = y * _sigmoid(y)
    o_ref[0] = x + jnp.dot(y.astype(BF16), w2_ref[...], preferred_element_type=F32) + b2_ref[...]


def _conv_mixer(h, g, w1, b1, wdw, bdw, lng, lnb, w2, b2, *, ts=512):
    b, s, d = h.shape
    kw = wdw.shape[0]
    assert s % ts == 0 and ts % CONV_ROWS == 0 and ts >= CONV_HALO >= kw - 1
    assert CONV_HALO % SUBLANES == 0 and d % CONV_PANEL == 0 and CONV_PANEL % CONV_COLS == 0
    assert 2 * CONV_SHIFT == SUBLANES
    row = lambda a: a.reshape(1, -1)
    kern = functools.partial(_conv_mixer_kernel, ts=ts, d=d, kw=kw)
    return pl.pallas_call(
        kern,
        out_shape=jax.ShapeDtypeStruct((b, s, d), F32),
        grid=(b, s // ts),
        in_specs=[
            pl.BlockSpec((1, ts, d), lambda i, j: (i, j, 0)),
            _resident((1, d)), _resident((d, 2 * d)), _resident((1, 2 * d)),
            _resident((kw, d)), _resident((1, d)), _resident((1, d)), _resident((1, d)),
            _resident((d, d)), _resident((1, d)),
        ],
        out_specs=pl.BlockSpec((1, ts, d), lambda i, j: (i, j, 0)),
        scratch_shapes=[pltpu.VMEM((ts, d), BF16), pltpu.VMEM((ts + CONV_HALO, d), F32),
                        pltpu.VMEM((ts + CONV_HALO, d), F32), pltpu.VMEM((ts, d), F32)],
        compiler_params=_params("arbitrary", "arbitrary"),
        name="conv_mixer",
    )(h, row(g), w1.astype(BF16), row(b1), wdw, row(bdw), row(lng), row(lnb), w2.astype(BF16), row(b2))


def _qkv_kernel(x_ref, g_ref, w_ref, qg_ref, kg_ref, q_ref, k_ref, v_ref, *, d, hd):
    xn = _rms(x_ref[...], g_ref[...]).astype(BF16)
    qkv = jnp.dot(xn, w_ref[...], preferred_element_type=F32)
    for h in range(d // hd):
        cols = slice(h * hd, (h + 1) * hd)
        q_ref[:, cols] = _rms(qkv[:, h * hd:(h + 1) * hd], qg_ref[...]).astype(BF16)
        k_ref[:, cols] = _rms(qkv[:, d + h * hd:d + (h + 1) * hd], kg_ref[...]).astype(BF16)
    v_ref[...] = qkv[:, 2 * d:].astype(BF16)


def _qkv(h2d, g, w, qg, kg, *, tm=512):
    t, d = h2d.shape
    hd = qg.shape[0]
    kern = functools.partial(_qkv_kernel, d=d, hd=hd)
    tile = pl.BlockSpec((tm, d), lambda i: (i, 0))
    return pl.pallas_call(
        kern,
        out_shape=[jax.ShapeDtypeStruct((t, d), BF16)] * 3,
        grid=(t // tm,),
        in_specs=[tile, _resident((1, d)), _resident((d, 3 * d)), _resident((1, hd)), _resident((1, hd))],
        out_specs=[tile, tile, tile],
        compiler_params=_params("arbitrary"),
        name="attn_qkv",
    )(h2d, g.reshape(1, d), w.astype(BF16), qg.reshape(1, hd), kg.reshape(1, hd))


def _moba_kernel(slope_ref, q_ref, k_ref, v_ref, kaux_ref, vaux_ref, o_ref, kaug, vaug, *bufs,
                 nb, blk, hd, topk):
    scale = hd ** -0.5
    negz = NEG_INF / scale
    slope = slope_ref[0][:, :1]
    kaug[:, :hd] = k_ref[0]
    kaug[:, hd:] = kaux_ref[...]
    vaug[:, :hd] = v_ref[0]
    vaug[:, hd:] = vaux_ref[...]

    km = jnp.concatenate([jnp.mean(k_ref[0, n * blk:(n + 1) * blk, :].astype(F32), axis=0, keepdims=True)
                          for n in range(nb)], axis=0)
    hi = km.astype(BF16).astype(F32)
    mid = (km - hi).astype(BF16).astype(F32)
    lo = (km - hi - mid).astype(BF16).astype(F32)
    pad = [jnp.zeros((-3 * nb % (2 * SUBLANES), hd), F32)] if 3 * nb % (2 * SUBLANES) else []
    km3 = jnp.concatenate([hi, mid, lo] + pad, axis=0).astype(BF16)

    r = lax.broadcasted_iota(jnp.int32, (blk, blk), 0)
    c = lax.broadcasted_iota(jnp.int32, (blk, blk), 1)
    causal = r >= c
    keypos = lax.broadcasted_iota(jnp.int32, (1, blk), 1).astype(F32)
    nt = (((1,), (1,)), ((), ()))

    def scores(j):
        zbuf = bufs[j]
        q = q_ref[0, j * blk:(j + 1) * blk, :]
        select = j > topk
        if select:
            g3 = lax.dot_general(km3, q, nt, preferred_element_type=F32)
            gate = g3[0:nb] + g3[nb:2 * nb] + g3[2 * nb:3 * nb]
            g = [gate[m:m + 1, :] for m in range(j)]
            rows = []
            for n in range(j):
                beaten = jnp.zeros((1, blk), F32)
                for m in range(j):
                    if m != n:
                        ahead = (g[m] > g[n]) | (g[m] == g[n]) if m < n else (g[m] > g[n])
                        beaten = beaten + jnp.where(ahead, 1.0, 0.0)
                rows.append(jnp.where(beaten < topk, 0.0, negz))
            bias_t = jnp.concatenate(rows + [jnp.zeros((LANES - j, blk), F32)], axis=0)
            q = jnp.concatenate([q, bias_t.T.astype(BF16)], axis=1)
        mrun = None
        for n in range(j + 1):
            keys = kaug[n * blk:(n + 1) * blk, :] if select else k_ref[0, n * blk:(n + 1) * blk, :]
            z = lax.dot_general(q, keys, nt, preferred_element_type=F32)
            u = z * scale + slope * (keypos + float(n * blk))
            if n == j:
                u = jnp.where(causal, u, NEG_INF)
            zbuf[:, n * blk:(n + 1) * blk] = u
            for half in range(blk // LANES):
                f = u[:, half * LANES:(half + 1) * LANES]
                mrun = f if mrun is None else jnp.maximum(mrun, f)
        return jnp.broadcast_to(jnp.max(mrun, axis=-1, keepdims=True), (blk, LANES))

    def attend(j, mb):
        zbuf, pbuf = bufs[j], bufs[nb + j]
        for n in range(j + 1):
            for half in range(blk // LANES):
                cols = slice(n * blk + half * LANES, n * blk + (half + 1) * LANES)
                pbuf[:, cols] = jnp.exp(zbuf[:, cols] - mb).astype(BF16)
        acc = jnp.dot(pbuf[:, :(j + 1) * blk], vaug[:(j + 1) * blk, :], preferred_element_type=F32)
        o_ref[0, j * blk:(j + 1) * blk, :] = (acc[:, :hd] / acc[:, hd:hd + 1]).astype(BF16)

    order = list(range(nb - 1, -1, -1))
    mb_next = scores(order[0])
    for i, j in enumerate(order):
        mb = mb_next
        if i + 1 < nb:
            mb_next = scores(order[i + 1])
        attend(j, mb)


def _moba(q, k, v, *, n_heads):
    b, s, d = q.shape
    hd = d // n_heads
    blk = MOBA_BLOCK
    assert s % blk == 0 and hd == LANES and blk % LANES == 0
    nb = s // blk
    assert nb <= LANES
    topk = max(1, min(MOBA_TOPK, nb - 1))
    slopes = jnp.exp2(-8.0 * jnp.arange(1, n_heads + 1, dtype=F32) / n_heads)
    slopes = jnp.broadcast_to(slopes[:, None, None], (n_heads, 1, LANES))
    kaux = np.zeros((s, LANES), np.float32)
    kaux[np.arange(s), np.arange(s) // blk] = 1.0
    vaux = np.zeros((s, LANES), np.float32)
    vaux[:, 0] = 1.0
    kern = functools.partial(_moba_kernel, nb=nb, blk=blk, hd=hd, topk=topk)
    head = pl.BlockSpec((1, s, hd), lambda i, h: (i, 0, h))
    return pl.pallas_call(
        kern,
        out_shape=jax.ShapeDtypeStruct((b, s, d), BF16),
        grid=(b, n_heads),
        in_specs=[pl.BlockSpec((1, 1, LANES), lambda i, h: (h, 0, 0)), head, head, head,
                  _resident((s, LANES)), _resident((s, LANES))],
        out_specs=head,
        scratch_shapes=([pltpu.VMEM((s, hd + LANES), BF16), pltpu.VMEM((s, hd + LANES), BF16)]
                        + [pltpu.VMEM((blk, (j + 1) * blk), F32) for j in range(nb)]
                        + [pltpu.VMEM((blk, (j + 1) * blk), BF16) for j in range(nb)]),
        compiler_params=_params("arbitrary", "arbitrary"),
        name="moba_attn",
    )(slopes, q, k, v, jnp.asarray(kaux, BF16), jnp.asarray(vaux, BF16))


def _matmul_residual_kernel(h_ref, a_ref, w_ref, o_ref):
    o_ref[...] = h_ref[...] + jnp.dot(a_ref[...], w_ref[...], preferred_element_type=F32)


def _matmul_residual(h2d, a, w, *, tm=512):
    t, d = h2d.shape
    tile = pl.BlockSpec((tm, d), lambda i: (i, 0))
    return pl.pallas_call(
        _matmul_residual_kernel,
        out_shape=jax.ShapeDtypeStruct((t, d), F32),
        grid=(t // tm,),
        in_specs=[tile, pl.BlockSpec((tm, a.shape[1]), lambda i: (i, 0)), _resident(w.shape)],
        out_specs=tile,
        compiler_params=_params("arbitrary"),
        name="attn_out_proj",
    )(h2d, a, w.astype(BF16))


def _pair_order(n):
    todo = [(a, b) for a in range(n) for b in range(a + 1, n)]
    order = [todo.pop(0)]
    while todo:
        a, b = order[-1]
        nxt = next((p for p in todo if p[0] == a or p[1] == b), todo[0])
        todo.remove(nxt)
        order.append(nxt)
    return order


PAIRS = _pair_order(EXPERTS_PER_GROUP)
N_CLASSES = N_GROUPS * len(PAIRS)
META_CLASS, META_RANK, META_WA, META_WB = 0, 1, 2, 3
EXPERT_TILE = 256


def _router_kernel(h_ref, g_ref, wr_ref, br_ref, tri_ref, route_ref, cnt_ref, running):
    @pl.when(pl.program_id(0) == 0)
    def _():
        running[...] = jnp.zeros_like(running)

    xn = _rms(h_ref[...], g_ref[...])
    wr = wr_ref[...]
    xh, wh = xn.astype(BF16), wr.astype(BF16)
    xl, wl = (xn - xh.astype(F32)).astype(BF16), (wr - wh.astype(F32)).astype(BF16)
    logits = (jnp.dot(xh, wh, preferred_element_type=F32) + jnp.dot(xl, wh, preferred_element_type=F32)
              + jnp.dot(xh, wl, preferred_element_type=F32)) + br_ref[...]
    lane = lax.broadcasted_iota(jnp.int32, logits.shape, 1).astype(F32)

    def first_max(vals):
        mx = jnp.max(vals, axis=-1, keepdims=True)
        return mx, jnp.min(jnp.where(vals == mx, lane, float(ROUTER_LANES)), axis=-1, keepdims=True)

    is_g = lane < N_GROUPS
    gmax, gidx = first_max(jnp.where(is_g, logits, -jnp.inf))
    zg = jnp.sum(jnp.where(is_g, jnp.exp(logits - gmax), 0.0), axis=-1, keepdims=True)
    g_w = 1.0 / zg
    lo = EXPERT_LANE0 + gidx * EXPERTS_PER_GROUP
    in_grp = (lane >= lo) & (lane < lo + EXPERTS_PER_GROUP)
    el = jnp.where(in_grp, logits, -jnp.inf)
    m1, i1 = first_max(el)
    m2, i2 = first_max(jnp.where(lane == i1, -jnp.inf, el))
    p2 = jnp.exp(m2 - m1)
    w1 = g_w / (1.0 + p2)
    w2 = g_w * p2 / (1.0 + p2)
    e1, e2 = i1 - lo, i2 - lo
    ea, eb = jnp.minimum(e1, e2), jnp.maximum(e1, e2)
    wa, wb = jnp.where(e1 < e2, w1, w2), jnp.where(e1 < e2, w2, w1)
    pair = sum(jnp.where((ea == a) & (eb == b), float(n), 0.0) for n, (a, b) in enumerate(PAIRS))
    cls = gidx * float(len(PAIRS)) + pair
    onehot = jnp.where(lane == cls, 1.0, 0.0)
    before = jnp.dot(tri_ref[...], onehot.astype(BF16), preferred_element_type=F32) + running[...]
    rank = jnp.sum(onehot * before, axis=-1, keepdims=True)
    running[...] += jnp.sum(onehot, axis=0, keepdims=True)
    cnt_ref[...] = running[...]

    meta = (jnp.where(lane == META_CLASS, cls, 0.0) + jnp.where(lane == META_RANK, rank, 0.0)
            + jnp.where(lane == META_WA, wa, 0.0) + jnp.where(lane == META_WB, wb, 0.0))
    route_ref[...] = meta.T[:SUBLANES, :]


def _router(h2d, g, w_group, b_group, w_router, b_router, *, tm=512):
    t, d = h2d.shape
    ng, _, ne = w_router.shape
    assert (ng, ne) == (N_GROUPS, EXPERTS_PER_GROUP) and N_CLASSES <= ROUTER_LANES
    wr = jnp.concatenate([w_group, jnp.transpose(w_router, (1, 0, 2)).reshape(d, ng * ne)], axis=1)
    br = jnp.concatenate([b_group, b_router.reshape(ng * ne)])
    pad = ROUTER_LANES - wr.shape[1]
    wr = jnp.pad(wr, ((0, 0), (0, pad)))
    br = jnp.pad(br, (0, pad)).reshape(1, ROUTER_LANES)
    tri = jnp.asarray(np.tril(np.ones((tm, tm), np.float32), -1), BF16)
    return pl.pallas_call(
        _router_kernel,
        out_shape=[jax.ShapeDtypeStruct((SUBLANES, t), F32), jax.ShapeDtypeStruct((1, ROUTER_LANES), F32)],
        grid=(t // tm,),
        in_specs=[pl.BlockSpec((tm, d), lambda i: (i, 0)), _resident((1, d)), _resident((d, ROUTER_LANES)),
                  _resident((1, ROUTER_LANES)), _resident((tm, tm))],
        out_specs=[pl.BlockSpec((SUBLANES, tm), lambda i: (0, i)),
                   pl.BlockSpec((1, ROUTER_LANES), lambda i: (0, 0))],
        scratch_shapes=[pltpu.VMEM((1, ROUTER_LANES), F32)],
        compiler_params=_params("arbitrary"),
        name="moe_router",
    )(h2d, g.reshape(1, d), wr, br, tri)


ITEM_VALID, ITEM_FIRST_OF_TILE, ITEM_NEW_A, ITEM_NEW_B, ITEM_MORE_A, ITEM_MORE_B = 1, 2, 4, 8, 16, 32


def _prefix_sum(v):
    ids = jnp.arange(v.shape[0], dtype=jnp.int32)
    return jnp.sum(jnp.where(ids[None, :] <= ids[:, None], v[None, :], 0), axis=1)


def _sorted_layout(route, counts, n_tiles, n_items):
    cls = route[META_CLASS].astype(jnp.int32)
    rank = route[META_RANK].astype(jnp.int32)
    cnt = counts[0, :N_CLASSES].astype(jnp.int32)
    ids = jnp.arange(N_CLASSES, dtype=jnp.int32)
    end = _prefix_sum(cnt)
    start = end - cnt
    pos = jnp.sum(jnp.where(cls[None, :] == ids[:, None], start[:, None], 0), axis=0) + rank

    first = start // EXPERT_TILE
    last = jnp.where(cnt > 0, (end - 1) // EXPERT_TILE, first - 1)
    items = last - first + 1
    item_end = _prefix_sum(items)
    k = jnp.arange(n_items, dtype=jnp.int32)
    item_cls = jnp.sum((item_end[None, :] <= k[:, None]).astype(jnp.int32), axis=1)
    valid = item_cls < N_CLASSES
    item_cls = jnp.where(valid, item_cls, jnp.max(jnp.where(cnt > 0, ids, 0)))
    sel = item_cls[:, None] == ids[None, :]
    pick = lambda v: jnp.sum(jnp.where(sel, v[None, :], 0), axis=1)
    tile = jnp.where(valid, pick(first) + k - pick(item_end - items), n_tiles - 1)
    lo = jnp.where(valid, jnp.clip(pick(start) - tile * EXPERT_TILE, 0, EXPERT_TILE), 0)
    hi = jnp.where(valid, jnp.clip(pick(end) - tile * EXPERT_TILE, 0, EXPERT_TILE), 0)

    grp, pair = item_cls // len(PAIRS), item_cls % len(PAIRS)
    ea = grp * EXPERTS_PER_GROUP + sum(jnp.where(pair == n, a, 0) for n, (a, _) in enumerate(PAIRS))
    eb = grp * EXPERTS_PER_GROUP + sum(jnp.where(pair == n, b, 0) for n, (_, b) in enumerate(PAIRS))
    prev = lambda v: jnp.concatenate([jnp.full((1,), -1, jnp.int32), v[:-1]])
    new_a, new_b = ea != prev(ea), eb != prev(eb)

    def upcoming(e, is_new):
        later = is_new[None, :] & (k[None, :] > k[:, None])
        nxt = jnp.min(jnp.where(later, k[None, :], n_items), axis=1)
        return jnp.sum(jnp.where(k[None, :] == nxt[:, None], e[None, :], 0), axis=1), nxt < n_items

    next_a, more_a = upcoming(ea, new_a)
    next_b, more_b = upcoming(eb, new_b)
    flags = (jnp.where(valid, ITEM_VALID, 0) + jnp.where(valid & (tile != prev(tile)), ITEM_FIRST_OF_TILE, 0)
             + jnp.where(new_a, ITEM_NEW_A, 0) + jnp.where(new_b, ITEM_NEW_B, 0)
             + jnp.where(more_a, ITEM_MORE_A, 0) + jnp.where(more_b, ITEM_MORE_B, 0))
    return pos, (tile, ea, eb, next_a, next_b, lo, hi, flags)


DISPATCH_CHUNK = 2048
ROW_DMA_UNROLL = 8


def _row_copy(src, dst, i, j, sem):
    return pltpu.make_async_copy(src.at[pl.ds(i, 1)], dst.at[pl.ds(j, 1)], sem)


DISPATCH_ROWS = 256


def _dispatch_kernel(pos_ref, h_ref, g_ref, route_ref, hxs_ref, rows, sems, *, d):
    s = pl.program_id(0)
    n = pl.num_programs(0) - 1
    prev = (s + 1) % 2

    @pl.when(s > 0)
    def _():
        base = (s - 1) * DISPATCH_CHUNK

        def issue(r, carry):
            _row_copy(rows.at[prev], hxs_ref, r, pos_ref[base + r], sems.at[prev]).start()
            return carry

        lax.fori_loop(0, DISPATCH_CHUNK, issue, 0, unroll=ROW_DMA_UNROLL)

    @pl.when(s < n)
    def _():
        def build(i, carry):
            r0 = pl.multiple_of(i * DISPATCH_ROWS, DISPATCH_ROWS)
            rec = route_ref[:, pl.ds(r0, DISPATCH_ROWS)]
            pad = jnp.zeros((LANES - SUBLANES, DISPATCH_ROWS), F32)
            rows[s % 2, pl.ds(r0, DISPATCH_ROWS), 0, :d] = _rms(h_ref[pl.ds(r0, DISPATCH_ROWS), :], g_ref[...])
            rows[s % 2, pl.ds(r0, DISPATCH_ROWS), 0, d:] = jnp.concatenate([rec, pad], axis=0).T
            return carry

        lax.fori_loop(0, DISPATCH_CHUNK // DISPATCH_ROWS, build, 0)

    @pl.when(s > 0)
    def _():
        pltpu.make_async_copy(rows.at[prev], hxs_ref.at[pl.ds(0, DISPATCH_CHUNK)], sems.at[prev]).wait()


def _dispatch(h2d, g, route, pos):
    t, d = h2d.shape
    w = d + ROUTER_LANES
    assert t % DISPATCH_CHUNK == 0 and DISPATCH_CHUNK % DISPATCH_ROWS == 0
    n = t // DISPATCH_CHUNK
    return pl.pallas_call(
        functools.partial(_dispatch_kernel, d=d),
        out_shape=jax.ShapeDtypeStruct((t, 1, w), F32),
        grid_spec=pltpu.PrefetchScalarGridSpec(
            num_scalar_prefetch=1,
            grid=(n + 1,),
            in_specs=[pl.BlockSpec((DISPATCH_CHUNK, d), lambda s, pos: (jnp.minimum(s, n - 1), 0)),
                      pl.BlockSpec((1, d), lambda s, pos: (0, 0)),
                      pl.BlockSpec((SUBLANES, DISPATCH_CHUNK), lambda s, pos: (0, jnp.minimum(s, n - 1)))],
            out_specs=pl.BlockSpec(memory_space=pl.ANY),
            scratch_shapes=[pltpu.VMEM((2, DISPATCH_CHUNK, 1, w), F32), pltpu.SemaphoreType.DMA((2,))],
        ),
        compiler_params=_params("arbitrary"),
        name="moe_dispatch",
    )(pos, h2d, g.reshape(1, d), route)


def _experts_kernel(tile_ref, ea_ref, eb_ref, na_ref, nb_ref, lo_ref, hi_ref, flag_ref, x_ref, wg_hbm, wu_hbm,
                    wd_hbm, y_ref, w1, w2, xs, stage_g, stage_u, stage_d, sems, *, d, hid, layer):
    del tile_ref
    k = pl.program_id(0)
    flags = flag_ref[k]

    def refresh(slot, now_ref, next_ref, new_bit, more_bit):
        def copies(e):
            return [pltpu.make_async_copy(src.at[layer, e], dst.at[slot], sems.at[slot, n])
                    for n, (src, dst) in enumerate(((wg_hbm, stage_g), (wu_hbm, stage_u), (wd_hbm, stage_d)))]

        @pl.when((flags & new_bit) != 0)
        def _():
            @pl.when(k == 0)
            def _():
                for c in copies(now_ref[k]):
                    c.start()

            for c in copies(now_ref[k]):
                c.wait()
            w1[:, 2 * slot * hid:(2 * slot + 1) * hid] = stage_g[slot].astype(BF16)
            w1[:, (2 * slot + 1) * hid:(2 * slot + 2) * hid] = stage_u[slot].astype(BF16)
            w2[slot * hid:(slot + 1) * hid, :] = stage_d[slot].astype(BF16)

            @pl.when((flags & more_bit) != 0)
            def _():
                for c in copies(next_ref[k]):
                    c.start()

    refresh(0, ea_ref, na_ref, ITEM_NEW_A, ITEM_MORE_A)
    refresh(1, eb_ref, nb_ref, ITEM_NEW_B, ITEM_MORE_B)

    @pl.when((flags & ITEM_VALID) != 0)
    def _():
        row = lax.broadcasted_iota(jnp.int32, (EXPERT_TILE, 1), 0)
        inside = (row >= lo_ref[k]) & (row < hi_ref[k])
        xs[...] = x_ref[:, 0, :]
        wa = jnp.where(inside, xs[:, d + META_WA:d + META_WA + 1], 0.0)
        wb = jnp.where(inside, xs[:, d + META_WB:d + META_WB + 1], 0.0)
        gu = jnp.dot(xs[:, :d].astype(BF16), w1[...], preferred_element_type=F32)
        ga, gb = gu[:, 0:hid], gu[:, 2 * hid:3 * hid]
        ha = ga * _sigmoid(ga) * gu[:, hid:2 * hid] * wa
        hb = gb * _sigmoid(gb) * gu[:, 3 * hid:] * wb
        y = jnp.dot(jnp.concatenate([ha, hb], axis=1).astype(BF16), w2[...], preferred_element_type=F32)

        @pl.when((flags & ITEM_FIRST_OF_TILE) != 0)
        def _():
            y_ref[...] = y

        @pl.when((flags & ITEM_FIRST_OF_TILE) == 0)
        def _():
            y_ref[...] += y


def _experts(hxs, items, w_gate, w_up, w_down, layer, *, d):
    t, _, w = hxs.shape
    hid = w_gate.shape[-1]
    n_items = items[0].shape[0]
    hbm = pl.BlockSpec(memory_space=pl.ANY)
    return pl.pallas_call(
        functools.partial(_experts_kernel, d=d, hid=hid, layer=layer),
        out_shape=jax.ShapeDtypeStruct((t, d), F32),
        grid_spec=pltpu.PrefetchScalarGridSpec(
            num_scalar_prefetch=len(items),
            grid=(n_items,),
            in_specs=[pl.BlockSpec((EXPERT_TILE, 1, w), lambda k, tile, *_: (tile[k], 0, 0)), hbm, hbm, hbm],
            out_specs=pl.BlockSpec((EXPERT_TILE, d), lambda k, tile, *_: (tile[k], 0)),
            scratch_shapes=[pltpu.VMEM((d, 4 * hid), BF16), pltpu.VMEM((2 * hid, d), BF16),
                            pltpu.VMEM((EXPERT_TILE, w), F32),
                            pltpu.VMEM((2, d, hid), F32), pltpu.VMEM((2, d, hid), F32), pltpu.VMEM((2, hid, d), F32),
                            pltpu.SemaphoreType.DMA((2, 3))],
        ),
        compiler_params=_params("arbitrary"),
        name="moe_experts",
    )(*items, hxs, w_gate, w_up, w_down)


def _combine_ple_kernel(pos_ref, h_ref, p_ref, g_ref, wg_ref, wp_ref, ys_ref, o_ref, ybuf, sems, *, tm):
    i = pl.program_id(0)
    n = pl.num_programs(0)
    slot = i % 2

    def wait(s):
        pltpu.make_async_copy(ys_ref.at[pl.ds(0, tm)], ybuf.at[s], sems.at[s]).wait()

    @pl.when(i == 0)
    def _():
        def issue(r, carry):
            _row_copy(ys_ref, ybuf.at[0], pos_ref[r], r, sems.at[0]).start()
            return carry
        lax.fori_loop(0, tm, issue, 0, unroll=ROW_DMA_UNROLL)

    nxt = jnp.minimum(i + 1, n - 1) * tm
    for r in range(tm):
        _row_copy(ys_ref, ybuf.at[1 - slot], pos_ref[nxt + r], r, sems.at[1 - slot]).start()

    wait(slot)
    h = h_ref[...] + ybuf[slot]
    gate = _sigmoid(jnp.dot(_rms(h, g_ref[...]).astype(BF16), wg_ref[...], preferred_element_type=F32))
    proj = jnp.dot(p_ref[...].astype(BF16), wp_ref[...], preferred_element_type=F32)
    o_ref[...] = h + gate * proj

    @pl.when(i == n - 1)
    def _():
        wait(1 - slot)


def _combine_ple(h2d, ys, pos, p3d, layer, g, w_gate, w_proj, *, tm=512):
    t, d = h2d.shape
    pd = p3d.shape[-1]
    tile = pl.BlockSpec((tm, d), lambda i, pos: (i, 0))
    res = lambda shape: pl.BlockSpec(shape, lambda i, pos: (0,) * len(shape), pipeline_mode=pl.Buffered(1))
    return pl.pallas_call(
        functools.partial(_combine_ple_kernel, tm=tm),
        out_shape=jax.ShapeDtypeStruct((t, d), F32),
        grid_spec=pltpu.PrefetchScalarGridSpec(
            num_scalar_prefetch=1,
            grid=(t // tm,),
            in_specs=[tile, pl.BlockSpec((None, tm, pd), lambda i, pos: (layer, i, 0)), res((1, d)), res((d, d)),
                      res((pd, d)), pl.BlockSpec(memory_space=pl.ANY)],
            out_specs=tile,
            scratch_shapes=[pltpu.VMEM((2, tm, d), F32), pltpu.SemaphoreType.DMA((2,))],
        ),
        compiler_params=_params("arbitrary"),
        name="moe_combine_ple",
    )(pos, h2d, p3d, g.reshape(1, d), w_gate.astype(BF16), w_proj.astype(BF16), ys)


def _moe_ple(h2d, p3d, layer, g_ffn, w_group, b_group, w_router, b_router, w_gate, w_up, w_down, g_ple,
             ple_gate, ple_proj):
    t, d = h2d.shape
    assert t % EXPERT_TILE == 0
    n_tiles = t // EXPERT_TILE
    n_items = n_tiles + N_CLASSES - 1
    route, counts = _router(h2d, g_ffn, w_group, b_group, w_router, b_router)
    pos, items = _sorted_layout(route, counts, n_tiles, n_items)
    hxs = _dispatch(h2d, g_ffn, route, pos)
    ys = _experts(hxs, items, w_gate, w_up, w_down, layer, d=d)
    return _combine_ple(h2d, ys, pos, p3d, layer, g_ple, ple_gate, ple_proj)


def kernel(x, p, g_mix, g_ffn, g_ple, conv_w_pw1, conv_b_pw1, conv_w_dw, conv_b_dw, conv_ln_g, conv_ln_b, conv_w_pw2, conv_b_pw2, attn_w_qkv, attn_q_gain, attn_k_gain, attn_w_o, moe_w_group, moe_b_group, moe_w_router, moe_b_router, moe_w_gate, moe_w_up, moe_w_down, ple_w_gate, ple_w_proj):
    b, s, d = x.shape
    depth = g_mix.shape[0]
    t = b * s
    h = x
    for i in range(depth):
        if i % 2 == 0:
            c = i // 2
            h = _conv_mixer(h.reshape(b, s, d), g_mix[i], conv_w_pw1[c], conv_b_pw1[c], conv_w_dw[c],
                            conv_b_dw[c], conv_ln_g[c], conv_ln_b[c], conv_w_pw2[c], conv_b_pw2[c])
            h = h.reshape(t, d)
        else:
            a = i // 2
            h = h.reshape(t, d)
            q, k, v = _qkv(h, g_mix[i], attn_w_qkv[a], attn_q_gain[a], attn_k_gain[a])
            o = _moba(q.reshape(b, s, d), k.reshape(b, s, d), v.reshape(b, s, d), n_heads=N_HEADS)
            h = _matmul_residual(h, o.reshape(t, d), attn_w_o[a])
        h = _moe_ple(h, p.reshape(depth, t, -1), i, g_ffn[i], moe_w_group[i], moe_b_group[i], moe_w_router[i],
                     moe_b_router[i], moe_w_gate, moe_w_up, moe_w_down, g_ple[i], ple_w_gate[i], ple_w_proj[i])
    return h.reshape(b, s, d)
```

```python
import functools

import jax
import jax.numpy as jnp
import numpy as np
from jax import lax
from jax.experimental import pallas as pl
from jax.experimental.pallas import tpu as pltpu

F32 = jnp.float32
BF16 = jnp.bfloat16

EPS = 1e-6
NEG_INF = -1e30
LOG2E = 1.4426950408889634

N_HEADS = 8
MOBA_BLOCK = 256
MOBA_TOPK = 3
N_GROUPS = 4
EXPERTS_PER_GROUP = 4
N_EXPERTS = N_GROUPS * EXPERTS_PER_GROUP

V7X_VMEM_BYTES = 64 * 1024 * 1024
LANES = 128
SUBLANES = 8
VMEM_LIMIT = V7X_VMEM_BYTES * 7 // 8

ROUTER_LANES = LANES
EXPERT_LANE0 = N_GROUPS

CONV_HALO = 32
CONV_ROWS = 64
CONV_COLS = 128
CONV_SHIFT = 4
CONV_PANEL = 256


def _params(*semantics):
    return pltpu.CompilerParams(dimension_semantics=semantics, vmem_limit_bytes=VMEM_LIMIT)


def _resident(shape):
    nd = len(shape)
    return pl.BlockSpec(shape, lambda *_: (0,) * nd, pipeline_mode=pl.Buffered(1))


def _rms(x, g):
    return x * lax.rsqrt(jnp.mean(x * x, axis=-1, keepdims=True) + EPS) * g


def _sigmoid(x):
    return 1.0 / (1.0 + jnp.exp(-x))


def _conv_mixer_kernel(x_ref, g_ref, w1_ref, b1_ref, wdw_ref, bdw_ref, lng_ref, lnb_ref,
                       w2_ref, b2_ref, o_ref, xnbuf, ubuf, ubuf4, cbuf, *, ts, d, kw):
    s = pl.program_id(1)
    x = x_ref[0]
    xnbuf[...] = _rms(x, g_ref[...]).astype(BF16)

    @pl.when(s == 0)
    def _():
        ubuf[0:CONV_HALO, :] = jnp.zeros((CONV_HALO, d), F32)
        ubuf4[0:CONV_HALO, :] = jnp.zeros((CONV_HALO, d), F32)
        ubuf4[ts + CONV_HALO - SUBLANES:ts + CONV_HALO, :] = jnp.zeros((SUBLANES, d), F32)

    def glu_panel(c0):
        a = jnp.dot(xnbuf[...], w1_ref[:, c0:c0 + CONV_PANEL], preferred_element_type=F32)
        a = a + b1_ref[:, c0:c0 + CONV_PANEL]
        g = jnp.dot(xnbuf[...], w1_ref[:, d + c0:d + c0 + CONV_PANEL], preferred_element_type=F32)
        g = g + b1_ref[:, d + c0:d + c0 + CONV_PANEL]
        u = a * _sigmoid(g)
        ubuf[CONV_HALO:CONV_HALO + ts, c0:c0 + CONV_PANEL] = u
        ubuf4[CONV_HALO - CONV_SHIFT:CONV_HALO - CONV_SHIFT + ts, c0:c0 + CONV_PANEL] = u

    first = CONV_HALO - (kw - 1)

    def conv_panel(c0):
        for cc in range(c0, c0 + CONV_PANEL, CONV_COLS):
            cols = slice(cc, cc + CONV_COLS)
            for r0 in range(0, ts, CONV_ROWS):
                acc = None
                for r in range(CONV_SHIFT):
                    rows = CONV_ROWS + (SUBLANES if r else 0)
                    part = None
                    for k in range(kw):
                        j = (first + k) % SUBLANES
                        if j % CONV_SHIFT == r:
                            src = ubuf if j < CONV_SHIFT else ubuf4
                            a0 = r0 + (first + k) - j
                            term = src[a0:a0 + rows, cols] * wdw_ref[k:k + 1, cols]
                            part = term if part is None else part + term
                    if part is not None:
                        part = part[r:r + CONV_ROWS]
                        acc = part if acc is None else acc + part
                cbuf[r0:r0 + CONV_ROWS, cols] = acc
        ubuf[0:CONV_HALO, c0:c0 + CONV_PANEL] = ubuf[ts:ts + CONV_HALO, c0:c0 + CONV_PANEL]
        ubuf4[0:CONV_HALO, c0:c0 + CONV_PANEL] = ubuf4[ts:ts + CONV_HALO, c0:c0 + CONV_PANEL]

    glu_panel(0)
    for c0 in range(0, d, CONV_PANEL):
        if c0 + CONV_PANEL < d:
            glu_panel(c0 + CONV_PANEL)
        conv_panel(c0)

    c = cbuf[...] + bdw_ref[...]
    mu = jnp.mean(c, axis=-1, keepdims=True)
    xc = c - mu
    y = xc * lax.rsqrt(jnp.mean(xc * xc, axis=-1, keepdims=True) + EPS) * lng_ref[...] + lnb_ref[...]
    y = y * _sigmoid(y)
    o_ref[0] = x + jnp.dot(y.astype(BF16), w2_ref[...], preferred_element_type=F32) + b2_ref[...]


def _conv_mixer(h, g, w1, b1, wdw, bdw, lng, lnb, w2, b2, *, ts=512):
    b, s, d = h.shape
    kw = wdw.shape[0]
    assert s % ts == 0 and ts % CONV_ROWS == 0 and ts >= CONV_HALO >= kw - 1
    assert CONV_HALO % SUBLANES == 0 and d % CONV_PANEL == 0 and CONV_PANEL % CONV_COLS == 0
    assert 2 * CONV_SHIFT == SUBLANES
    row = lambda a: a.reshape(1, -1)
    kern = functools.partial(_conv_mixer_kernel, ts=ts, d=d, kw=kw)
    return pl.pallas_call(
        kern,
        out_shape=jax.ShapeDtypeStruct((b, s, d), F32),
        grid=(b, s // ts),
        in_specs=[
            pl.BlockSpec((1, ts, d), lambda i, j: (i, j, 0)),
            _resident((1, d)), _resident((d, 2 * d)), _resident((1, 2 * d)),
            _resident((kw, d)), _resident((1, d)), _resident((1, d)), _resident((1, d)),
            _resident((d, d)), _resident((1, d)),
        ],
        out_specs=pl.BlockSpec((1, ts, d), lambda i, j: (i, j, 0)),
        scratch_shapes=[pltpu.VMEM((ts, d), BF16), pltpu.VMEM((ts + CONV_HALO, d), F32),
                        pltpu.VMEM((ts + CONV_HALO, d), F32), pltpu.VMEM((ts, d), F32)],
        compiler_params=_params("arbitrary", "arbitrary"),
        name="conv_mixer",
    )(h, row(g), w1.astype(BF16), row(b1), wdw, row(bdw), row(lng), row(lnb), w2.astype(BF16), row(b2))


def _qkv_kernel(x_ref, g_ref, w_ref, qg_ref, kg_ref, q_ref, k_ref, v_ref, *, d, hd):
    xn = _rms(x_ref[...], g_ref[...]).astype(BF16)
    qkv = jnp.dot(xn, w_ref[...], preferred_element_type=F32)
    for h in range(d // hd):
        cols = slice(h * hd, (h + 1) * hd)
        q_ref[:, cols] = _rms(qkv[:, h * hd:(h + 1) * hd], qg_ref[...]).astype(BF16)
        k_ref[:, cols] = _rms(qkv[:, d + h * hd:d + (h + 1) * hd], kg_ref[...]).astype(BF16)
    v_ref[...] = qkv[:, 2 * d:].astype(BF16)


def _qkv(h2d, g, w, qg, kg, *, tm=512):
    t, d = h2d.shape
    hd = qg.shape[0]
    kern = functools.partial(_qkv_kernel, d=d, hd=hd)
    tile = pl.BlockSpec((tm, d), lambda i: (i, 0))
    return pl.pallas_call(
        kern,
        out_shape=[jax.ShapeDtypeStruct((t, d), BF16)] * 3,
        grid=(t // tm,),
        in_specs=[tile, _resident((1, d)), _resident((d, 3 * d)), _resident((1, hd)), _resident((1, hd))],
        out_specs=[tile, tile, tile],
        compiler_params=_params("arbitrary"),
        name="attn_qkv",
    )(h2d, g.reshape(1, d), w.astype(BF16), qg.reshape(1, hd), kg.reshape(1, hd))


def _moba_kernel(slope_ref, q_ref, k_ref, v_ref, kaux_ref, vaux_ref, o_ref, kaug, vaug, *bufs,
                 nb, blk, hd, topk):
    scale = hd ** -0.5
    negz = NEG_INF / scale
    slope = slope_ref[0][:, :1]
    kaug[:, :hd] = k_ref[0]
    kaug[:, hd:] = kaux_ref[...]
    vaug[:, :hd] = v_ref[0]
    vaug[:, hd:] = vaux_ref[...]

    km = jnp.concatenate([jnp.mean(k_ref[0, n * blk:(n + 1) * blk, :].astype(F32), axis=0, keepdims=True)
                          for n in range(nb)], axis=0)
    hi = km.astype(BF16).astype(F32)
    mid = (km - hi).astype(BF16).astype(F32)
    lo = (km - hi - mid).astype(BF16).astype(F32)
    pad = [jnp.zeros((-3 * nb % (2 * SUBLANES), hd), F32)] if 3 * nb % (2 * SUBLANES) else []
    km3 = jnp.concatenate([hi, mid, lo] + pad, axis=0).astype(BF16)

    r = lax.broadcasted_iota(jnp.int32, (blk, blk), 0)
    c = lax.broadcasted_iota(jnp.int32, (blk, blk), 1)
    causal = r >= c
    keypos = lax.broadcasted_iota(jnp.int32, (1, blk), 1).astype(F32)
    nt = (((1,), (1,)), ((), ()))

    def scores(j):
        zbuf = bufs[j]
        q = q_ref[0, j * blk:(j + 1) * blk, :]
        select = j > topk
        if select:
            g3 = lax.dot_general(km3, q, nt, preferred_element_type=F32)
            gate = g3[0:nb] + g3[nb:2 * nb] + g3[2 * nb:3 * nb]
            g = [gate[m:m + 1, :] for m in range(j)]
            rows = []
            for n in range(j):
                beaten = jnp.zeros((1, blk), F32)
                for m in range(j):
                    if m != n:
                        ahead = (g[m] > g[n]) | (g[m] == g[n]) if m < n else (g[m] > g[n])
                        beaten = beaten + jnp.where(ahead, 1.0, 0.0)
                rows.append(jnp.where(beaten < topk, 0.0, negz))
            bias_t = jnp.concatenate(rows + [jnp.zeros((LANES - j, blk), F32)], axis=0)
            q = jnp.concatenate([q, bias_t.T.astype(BF16)], axis=1)
        mrun = None
        for n in range(j + 1):
            keys = kaug[n * blk:(n + 1) * blk, :] if select else k_ref[0, n * blk:(n + 1) * blk, :]
            z = lax.dot_general(q, keys, nt, preferred_element_type=F32)
            u = z + (slope / scale) * (keypos + float(n * blk))
            if n == j:
                u = jnp.where(causal, u, negz)
            zbuf[:, n * blk:(n + 1) * blk] = u
            for half in range(blk // LANES):
                f = u[:, half * LANES:(half + 1) * LANES]
                mrun = f if mrun is None else jnp.maximum(mrun, f)
        return jnp.broadcast_to(jnp.max(mrun, axis=-1, keepdims=True), (blk, LANES))

    def attend(j, mb):
        zbuf, pbuf = bufs[j], bufs[nb + j]
        for n in range(j + 1):
            for half in range(blk // LANES):
                cols = slice(n * blk + half * LANES, n * blk + (half + 1) * LANES)
                pbuf[:, cols] = jnp.exp2((zbuf[:, cols] - mb) * (scale * LOG2E)).astype(BF16)
        acc = jnp.dot(pbuf[:, :(j + 1) * blk], vaug[:(j + 1) * blk, :], preferred_element_type=F32)
        o_ref[0, j * blk:(j + 1) * blk, :] = (acc[:, :hd] / acc[:, hd:hd + 1]).astype(BF16)

    order = list(range(nb - 1, -1, -1))
    mb_next = scores(order[0])
    for i, j in enumerate(order):
        mb = mb_next
        if i + 1 < nb:
            mb_next = scores(order[i + 1])
        attend(j, mb)


def _moba(q, k, v, *, n_heads):
    b, s, d = q.shape
    hd = d // n_heads
    blk = MOBA_BLOCK
    assert s % blk == 0 and hd == LANES and blk % LANES == 0
    nb = s // blk
    assert nb <= LANES
    topk = max(1, min(MOBA_TOPK, nb - 1))
    slopes = jnp.exp2(-8.0 * jnp.arange(1, n_heads + 1, dtype=F32) / n_heads)
    slopes = jnp.broadcast_to(slopes[:, None, None], (n_heads, 1, LANES))
    kaux = np.zeros((s, LANES), np.float32)
    kaux[np.arange(s), np.arange(s) // blk] = 1.0
    vaux = np.zeros((s, LANES), np.float32)
    vaux[:, 0] = 1.0
    kern = functools.partial(_moba_kernel, nb=nb, blk=blk, hd=hd, topk=topk)
    head = pl.BlockSpec((1, s, hd), lambda i, h: (i, 0, h))
    return pl.pallas_call(
        kern,
        out_shape=jax.ShapeDtypeStruct((b, s, d), BF16),
        grid=(b, n_heads),
        in_specs=[pl.BlockSpec((1, 1, LANES), lambda i, h: (h, 0, 0)), head, head, head,
                  _resident((s, LANES)), _resident((s, LANES))],
        out_specs=head,
        scratch_shapes=([pltpu.VMEM((s, hd + LANES), BF16), pltpu.VMEM((s, hd + LANES), BF16)]
                        + [pltpu.VMEM((blk, (j + 1) * blk), F32) for j in range(nb)]
                        + [pltpu.VMEM((blk, (j + 1) * blk), BF16) for j in range(nb)]),
        compiler_params=_params("arbitrary", "arbitrary"),
        name="moba_attn",
    )(slopes, q, k, v, jnp.asarray(kaux, BF16), jnp.asarray(vaux, BF16))


def _pair_order(n):
    todo = [(a, b) for a in range(n) for b in range(a + 1, n)]
    order = [todo.pop(0)]
    while todo:
        a, b = order[-1]
        nxt = next((p for p in todo if p[0] == a or p[1] == b), todo[0])
        todo.remove(nxt)
        order.append(nxt)
    return order


PAIRS = _pair_order(EXPERTS_PER_GROUP)
N_CLASSES = N_GROUPS * len(PAIRS)
META_CLASS, META_RANK, META_WA, META_WB = 0, 1, 2, 3
EXPERT_TILE = 256


def _router_kernel(h_ref, g_ref, wr_ref, br_ref, tri_ref, route_ref, cnt_ref, running):
    _route_tile(h_ref[...], g_ref, wr_ref, br_ref, tri_ref, route_ref, cnt_ref, running)


def _out_proj_router_kernel(h_ref, a_ref, w_ref, g_ref, wr_ref, br_ref, tri_ref, o_ref, route_ref, cnt_ref,
                            running):
    h = h_ref[...] + jnp.dot(a_ref[...], w_ref[...], preferred_element_type=F32)
    o_ref[...] = h
    _route_tile(h, g_ref, wr_ref, br_ref, tri_ref, route_ref, cnt_ref, running)


def _route_tile(h, g_ref, wr_ref, br_ref, tri_ref, route_ref, cnt_ref, running):
    @pl.when(pl.program_id(0) == 0)
    def _():
        running[...] = jnp.zeros_like(running)

    xn = _rms(h, g_ref[...])
    wr = wr_ref[...]
    xh, wh = xn.astype(BF16), wr.astype(BF16)
    xl, wl = (xn - xh.astype(F32)).astype(BF16), (wr - wh.astype(F32)).astype(BF16)
    logits = (jnp.dot(xh, wh, preferred_element_type=F32) + jnp.dot(xl, wh, preferred_element_type=F32)
              + jnp.dot(xh, wl, preferred_element_type=F32)) + br_ref[...]
    lane = lax.broadcasted_iota(jnp.int32, logits.shape, 1).astype(F32)

    def first_max(vals):
        mx = jnp.max(vals, axis=-1, keepdims=True)
        return mx, jnp.min(jnp.where(vals == mx, lane, float(ROUTER_LANES)), axis=-1, keepdims=True)

    is_g = lane < N_GROUPS
    gmax, gidx = first_max(jnp.where(is_g, logits, -jnp.inf))
    zg = jnp.sum(jnp.where(is_g, jnp.exp(logits - gmax), 0.0), axis=-1, keepdims=True)
    g_w = 1.0 / zg
    lo = EXPERT_LANE0 + gidx * EXPERTS_PER_GROUP
    in_grp = (lane >= lo) & (lane < lo + EXPERTS_PER_GROUP)
    el = jnp.where(in_grp, logits, -jnp.inf)
    m1, i1 = first_max(el)
    m2, i2 = first_max(jnp.where(lane == i1, -jnp.inf, el))
    p2 = jnp.exp(m2 - m1)
    w1 = g_w / (1.0 + p2)
    w2 = g_w * p2 / (1.0 + p2)
    e1, e2 = i1 - lo, i2 - lo
    ea, eb = jnp.minimum(e1, e2), jnp.maximum(e1, e2)
    wa, wb = jnp.where(e1 < e2, w1, w2), jnp.where(e1 < e2, w2, w1)
    pair = sum(jnp.where((ea == a) & (eb == b), float(n), 0.0) for n, (a, b) in enumerate(PAIRS))
    cls = gidx * float(len(PAIRS)) + pair
    onehot = jnp.where(lane == cls, 1.0, 0.0)
    before = jnp.dot(tri_ref[...], onehot.astype(BF16), preferred_element_type=F32) + running[...]
    rank = jnp.sum(onehot * before, axis=-1, keepdims=True)
    running[...] += jnp.sum(onehot, axis=0, keepdims=True)
    cnt_ref[...] = running[...]

    meta = (jnp.where(lane == META_CLASS, cls, 0.0) + jnp.where(lane == META_RANK, rank, 0.0)
            + jnp.where(lane == META_WA, wa, 0.0) + jnp.where(lane == META_WB, wb, 0.0))
    route_ref[...] = meta.T[:SUBLANES, :]


def _router(h2d, g, w_group, b_group, w_router, b_router, mix=None, *, tm=512):
    t, d = h2d.shape
    ng, _, ne = w_router.shape
    assert (ng, ne) == (N_GROUPS, EXPERTS_PER_GROUP) and N_CLASSES <= ROUTER_LANES
    wr = jnp.concatenate([w_group, jnp.transpose(w_router, (1, 0, 2)).reshape(d, ng * ne)], axis=1)
    br = jnp.concatenate([b_group, b_router.reshape(ng * ne)])
    pad = ROUTER_LANES - wr.shape[1]
    wr = jnp.pad(wr, ((0, 0), (0, pad)))
    br = jnp.pad(br, (0, pad)).reshape(1, ROUTER_LANES)
    tri = jnp.asarray(np.tril(np.ones((tm, tm), np.float32), -1), BF16)
    tile = pl.BlockSpec((tm, d), lambda i: (i, 0))
    route_shapes = [jax.ShapeDtypeStruct((SUBLANES, t), F32), jax.ShapeDtypeStruct((1, ROUTER_LANES), F32)]
    route_specs = [pl.BlockSpec((SUBLANES, tm), lambda i: (0, i)), pl.BlockSpec((1, ROUTER_LANES), lambda i: (0, 0))]
    route_in = [_resident((1, d)), _resident((d, ROUTER_LANES)), _resident((1, ROUTER_LANES)), _resident((tm, tm))]
    common = dict(grid=(t // tm,), scratch_shapes=[pltpu.VMEM((1, ROUTER_LANES), F32)],
                  compiler_params=_params("arbitrary"))
    if mix is None:
        return pl.pallas_call(
            _router_kernel, out_shape=route_shapes, in_specs=[tile] + route_in, out_specs=route_specs,
            name="moe_router", **common,
        )(h2d, g.reshape(1, d), wr, br, tri)
    a, w = mix
    return pl.pallas_call(
        _out_proj_router_kernel,
        out_shape=[jax.ShapeDtypeStruct((t, d), F32)] + route_shapes,
        in_specs=[tile, pl.BlockSpec((tm, a.shape[1]), lambda i: (i, 0)), _resident(w.shape)] + route_in,
        out_specs=[tile] + route_specs,
        name="attn_out_proj_router", **common,
    )(h2d, a, w.astype(BF16), g.reshape(1, d), wr, br, tri)


ITEM_VALID, ITEM_FIRST_OF_TILE, ITEM_NEW_A, ITEM_NEW_B, ITEM_MORE_A, ITEM_MORE_B = 1, 2, 4, 8, 16, 32


def _prefix_sum(v):
    ids = jnp.arange(v.shape[0], dtype=jnp.int32)
    return jnp.sum(jnp.where(ids[None, :] <= ids[:, None], v[None, :], 0), axis=1)


def _sorted_layout(route, counts, n_tiles, n_items):
    cls = route[META_CLASS].astype(jnp.int32)
    rank = route[META_RANK].astype(jnp.int32)
    cnt = counts[0, :N_CLASSES].astype(jnp.int32)
    ids = jnp.arange(N_CLASSES, dtype=jnp.int32)
    end = _prefix_sum(cnt)
    start = end - cnt
    pos = jnp.sum(jnp.where(cls[None, :] == ids[:, None], start[:, None], 0), axis=0) + rank

    first = start // EXPERT_TILE
    last = jnp.where(cnt > 0, (end - 1) // EXPERT_TILE, first - 1)
    items = last - first + 1
    item_end = _prefix_sum(items)
    k = jnp.arange(n_items, dtype=jnp.int32)
    item_cls = jnp.sum((item_end[None, :] <= k[:, None]).astype(jnp.int32), axis=1)
    valid = item_cls < N_CLASSES
    item_cls = jnp.where(valid, item_cls, jnp.max(jnp.where(cnt > 0, ids, 0)))
    sel = item_cls[:, None] == ids[None, :]
    pick = lambda v: jnp.sum(jnp.where(sel, v[None, :], 0), axis=1)
    tile = jnp.where(valid, pick(first) + k - pick(item_end - items), n_tiles - 1)
    lo = jnp.where(valid, jnp.clip(pick(start) - tile * EXPERT_TILE, 0, EXPERT_TILE), 0)
    hi = jnp.where(valid, jnp.clip(pick(end) - tile * EXPERT_TILE, 0, EXPERT_TILE), 0)

    grp, pair = item_cls // len(PAIRS), item_cls % len(PAIRS)
    ea = grp * EXPERTS_PER_GROUP + sum(jnp.where(pair == n, a, 0) for n, (a, _) in enumerate(PAIRS))
    eb = grp * EXPERTS_PER_GROUP + sum(jnp.where(pair == n, b, 0) for n, (_, b) in enumerate(PAIRS))
    prev = lambda v: jnp.concatenate([jnp.full((1,), -1, jnp.int32), v[:-1]])
    new_a, new_b = ea != prev(ea), eb != prev(eb)

    def upcoming(e, is_new):
        later = is_new[None, :] & (k[None, :] > k[:, None])
        nxt = jnp.min(jnp.where(later, k[None, :], n_items), axis=1)
        return jnp.sum(jnp.where(k[None, :] == nxt[:, None], e[None, :], 0), axis=1), nxt < n_items

    next_a, more_a = upcoming(ea, new_a)
    next_b, more_b = upcoming(eb, new_b)
    flags = (jnp.where(valid, ITEM_VALID, 0) + jnp.where(valid & (tile != prev(tile)), ITEM_FIRST_OF_TILE, 0)
             + jnp.where(new_a, ITEM_NEW_A, 0) + jnp.where(new_b, ITEM_NEW_B, 0)
             + jnp.where(more_a, ITEM_MORE_A, 0) + jnp.where(more_b, ITEM_MORE_B, 0))
    return pos, (tile, ea, eb, next_a, next_b, lo, hi, flags)


DISPATCH_CHUNK = 2048
ROW_DMA_UNROLL = 8


def _row_copy(src, dst, i, j, sem):
    return pltpu.make_async_copy(src.at[pl.ds(i, 1)], dst.at[pl.ds(j, 1)], sem)


DISPATCH_ROWS = 256


def _dispatch_kernel(pos_ref, h_ref, g_ref, route_ref, hxs_ref, rows, sems, *, d):
    s = pl.program_id(0)
    n = pl.num_programs(0) - 1
    prev = (s + 1) % 2

    @pl.when(s > 0)
    def _():
        base = (s - 1) * DISPATCH_CHUNK

        def issue(r, carry):
            _row_copy(rows.at[prev], hxs_ref, r, pos_ref[base + r], sems.at[prev]).start()
            return carry

        lax.fori_loop(0, DISPATCH_CHUNK, issue, 0, unroll=ROW_DMA_UNROLL)

    @pl.when(s < n)
    def _():
        def build(i, carry):
            r0 = pl.multiple_of(i * DISPATCH_ROWS, DISPATCH_ROWS)
            rec = route_ref[:, pl.ds(r0, DISPATCH_ROWS)]
            pad = jnp.zeros((LANES - SUBLANES, DISPATCH_ROWS), F32)
            rows[s % 2, pl.ds(r0, DISPATCH_ROWS), 0, :d] = _rms(h_ref[pl.ds(r0, DISPATCH_ROWS), :], g_ref[...])
            rows[s % 2, pl.ds(r0, DISPATCH_ROWS), 0, d:] = jnp.concatenate([rec, pad], axis=0).T
            return carry

        lax.fori_loop(0, DISPATCH_CHUNK // DISPATCH_ROWS, build, 0)

    @pl.when(s > 0)
    def _():
        pltpu.make_async_copy(rows.at[prev], hxs_ref.at[pl.ds(0, DISPATCH_CHUNK)], sems.at[prev]).wait()


def _dispatch(h2d, g, route, pos):
    t, d = h2d.shape
    w = d + ROUTER_LANES
    assert t % DISPATCH_CHUNK == 0 and DISPATCH_CHUNK % DISPATCH_ROWS == 0
    n = t // DISPATCH_CHUNK
    return pl.pallas_call(
        functools.partial(_dispatch_kernel, d=d),
        out_shape=jax.ShapeDtypeStruct((t, 1, w), F32),
        grid_spec=pltpu.PrefetchScalarGridSpec(
            num_scalar_prefetch=1,
            grid=(n + 1,),
            in_specs=[pl.BlockSpec((DISPATCH_CHUNK, d), lambda s, pos: (jnp.minimum(s, n - 1), 0)),
                      pl.BlockSpec((1, d), lambda s, pos: (0, 0)),
                      pl.BlockSpec((SUBLANES, DISPATCH_CHUNK), lambda s, pos: (0, jnp.minimum(s, n - 1)))],
            out_specs=pl.BlockSpec(memory_space=pl.ANY),
            scratch_shapes=[pltpu.VMEM((2, DISPATCH_CHUNK, 1, w), F32), pltpu.SemaphoreType.DMA((2,))],
        ),
        compiler_params=_params("arbitrary"),
        name="moe_dispatch",
    )(pos, h2d, g.reshape(1, d), route)


def _experts_kernel(tile_ref, ea_ref, eb_ref, na_ref, nb_ref, lo_ref, hi_ref, flag_ref, x_ref, wg_hbm, wu_hbm,
                    wd_hbm, y_ref, w1, w2, xs, stage_g, stage_u, stage_d, sems, *, d, hid, layer):
    del tile_ref
    k = pl.program_id(0)
    flags = flag_ref[k]

    def refresh(slot, now_ref, next_ref, new_bit, more_bit):
        def copies(e):
            return [pltpu.make_async_copy(src.at[layer, e], dst.at[slot], sems.at[slot, n])
                    for n, (src, dst) in enumerate(((wg_hbm, stage_g), (wu_hbm, stage_u), (wd_hbm, stage_d)))]

        @pl.when((flags & new_bit) != 0)
        def _():
            @pl.when(k == 0)
            def _():
                for c in copies(now_ref[k]):
                    c.start()

            for c in copies(now_ref[k]):
                c.wait()
            w1[:, 2 * slot * hid:(2 * slot + 1) * hid] = stage_g[slot].astype(BF16)
            w1[:, (2 * slot + 1) * hid:(2 * slot + 2) * hid] = stage_u[slot].astype(BF16)
            w2[slot * hid:(slot + 1) * hid, :] = stage_d[slot].astype(BF16)

            @pl.when((flags & more_bit) != 0)
            def _():
                for c in copies(next_ref[k]):
                    c.start()

    refresh(0, ea_ref, na_ref, ITEM_NEW_A, ITEM_MORE_A)
    refresh(1, eb_ref, nb_ref, ITEM_NEW_B, ITEM_MORE_B)

    @pl.when((flags & ITEM_VALID) != 0)
    def _():
        row = lax.broadcasted_iota(jnp.int32, (EXPERT_TILE, 1), 0)
        inside = (row >= lo_ref[k]) & (row < hi_ref[k])
        xs[...] = x_ref[:, 0, :]
        wa = jnp.where(inside, xs[:, d + META_WA:d + META_WA + 1], 0.0)
        wb = jnp.where(inside, xs[:, d + META_WB:d + META_WB + 1], 0.0)
        gu = jnp.dot(xs[:, :d].astype(BF16), w1[...], preferred_element_type=F32)
        ga, gb = gu[:, 0:hid], gu[:, 2 * hid:3 * hid]
        ha = ga * _sigmoid(ga) * gu[:, hid:2 * hid] * wa
        hb = gb * _sigmoid(gb) * gu[:, 3 * hid:] * wb
        y = jnp.dot(jnp.concatenate([ha, hb], axis=1).astype(BF16), w2[...], preferred_element_type=F32)

        @pl.when((flags & ITEM_FIRST_OF_TILE) != 0)
        def _():
            y_ref[...] = y

        @pl.when((flags & ITEM_FIRST_OF_TILE) == 0)
        def _():
            y_ref[...] += y


def _experts(hxs, items, w_gate, w_up, w_down, layer, *, d):
    t, _, w = hxs.shape
    hid = w_gate.shape[-1]
    n_items = items[0].shape[0]
    hbm = pl.BlockSpec(memory_space=pl.ANY)
    return pl.pallas_call(
        functools.partial(_experts_kernel, d=d, hid=hid, layer=layer),
        out_shape=jax.ShapeDtypeStruct((t, d), F32),
        grid_spec=pltpu.PrefetchScalarGridSpec(
            num_scalar_prefetch=len(items),
            grid=(n_items,),
            in_specs=[pl.BlockSpec((EXPERT_TILE, 1, w), lambda k, tile, *_: (tile[k], 0, 0)), hbm, hbm, hbm],
            out_specs=pl.BlockSpec((EXPERT_TILE, d), lambda k, tile, *_: (tile[k], 0)),
            scratch_shapes=[pltpu.VMEM((d, 4 * hid), BF16), pltpu.VMEM((2 * hid, d), BF16),
                            pltpu.VMEM((EXPERT_TILE, w), F32),
                            pltpu.VMEM((2, d, hid), F32), pltpu.VMEM((2, d, hid), F32), pltpu.VMEM((2, hid, d), F32),
                            pltpu.SemaphoreType.DMA((2, 3))],
        ),
        compiler_params=_params("arbitrary"),
        name="moe_experts",
    )(*items, hxs, w_gate, w_up, w_down)


def _combine_ple_kernel(pos_ref, h_ref, p_ref, g_ref, wg_ref, wp_ref, ys_ref, o_ref, ybuf, sems, *, tm):
    i = pl.program_id(0)
    n = pl.num_programs(0)
    slot = i % 2

    def wait(s):
        pltpu.make_async_copy(ys_ref.at[pl.ds(0, tm)], ybuf.at[s], sems.at[s]).wait()

    @pl.when(i == 0)
    def _():
        def issue(r, carry):
            _row_copy(ys_ref, ybuf.at[0], pos_ref[r], r, sems.at[0]).start()
            return carry
        lax.fori_loop(0, tm, issue, 0, unroll=ROW_DMA_UNROLL)

    nxt = jnp.minimum(i + 1, n - 1) * tm
    for r in range(tm):
        _row_copy(ys_ref, ybuf.at[1 - slot], pos_ref[nxt + r], r, sems.at[1 - slot]).start()

    wait(slot)
    h = h_ref[...] + ybuf[slot]
    gate = _sigmoid(jnp.dot(_rms(h, g_ref[...]).astype(BF16), wg_ref[...], preferred_element_type=F32))
    proj = jnp.dot(p_ref[...].astype(BF16), wp_ref[...], preferred_element_type=F32)
    o_ref[...] = h + gate * proj

    @pl.when(i == n - 1)
    def _():
        wait(1 - slot)


def _combine_ple(h2d, ys, pos, p3d, layer, g, w_gate, w_proj, *, tm=512):
    t, d = h2d.shape
    pd = p3d.shape[-1]
    tile = pl.BlockSpec((tm, d), lambda i, pos: (i, 0))
    res = lambda shape: pl.BlockSpec(shape, lambda i, pos: (0,) * len(shape), pipeline_mode=pl.Buffered(1))
    return pl.pallas_call(
        functools.partial(_combine_ple_kernel, tm=tm),
        out_shape=jax.ShapeDtypeStruct((t, d), F32),
        grid_spec=pltpu.PrefetchScalarGridSpec(
            num_scalar_prefetch=1,
            grid=(t // tm,),
            in_specs=[tile, pl.BlockSpec((None, tm, pd), lambda i, pos: (layer, i, 0)), res((1, d)), res((d, d)),
                      res((pd, d)), pl.BlockSpec(memory_space=pl.ANY)],
            out_specs=tile,
            scratch_shapes=[pltpu.VMEM((2, tm, d), F32), pltpu.SemaphoreType.DMA((2,))],
        ),
        compiler_params=_params("arbitrary"),
        name="moe_combine_ple",
    )(pos, h2d, p3d, g.reshape(1, d), w_gate.astype(BF16), w_proj.astype(BF16), ys)


def _moe_ple(h2d, mix, p3d, layer, g_ffn, w_group, b_group, w_router, b_router, w_gate, w_up, w_down, g_ple,
             ple_gate, ple_proj):
    t, d = h2d.shape
    assert t % EXPERT_TILE == 0
    n_tiles = t // EXPERT_TILE
    n_items = n_tiles + N_CLASSES - 1
    if mix is None:
        route, counts = _router(h2d, g_ffn, w_group, b_group, w_router, b_router)
    else:
        h2d, route, counts = _router(h2d, g_ffn, w_group, b_group, w_router, b_router, mix)
    pos, items = _sorted_layout(route, counts, n_tiles, n_items)
    hxs = _dispatch(h2d, g_ffn, route, pos)
    ys = _experts(hxs, items, w_gate, w_up, w_down, layer, d=d)
    return _combine_ple(h2d, ys, pos, p3d, layer, g_ple, ple_gate, ple_proj)


def kernel(x, p, g_mix, g_ffn, g_ple, conv_w_pw1, conv_b_pw1, conv_w_dw, conv_b_dw, conv_ln_g, conv_ln_b, conv_w_pw2, conv_b_pw2, attn_w_qkv, attn_q_gain, attn_k_gain, attn_w_o, moe_w_group, moe_b_group, moe_w_router, moe_b_router, moe_w_gate, moe_w_up, moe_w_down, ple_w_gate, ple_w_proj):
    b, s, d = x.shape
    depth = g_mix.shape[0]
    t = b * s
    h = x
    for i in range(depth):
        if i % 2 == 0:
            c = i // 2
            h = _conv_mixer(h.reshape(b, s, d), g_mix[i], conv_w_pw1[c], conv_b_pw1[c], conv_w_dw[c],
                            conv_b_dw[c], conv_ln_g[c], conv_ln_b[c], conv_w_pw2[c], conv_b_pw2[c])
            h = h.reshape(t, d)
            mix = None
        else:
            a = i // 2
            h = h.reshape(t, d)
            q, k, v = _qkv(h, g_mix[i], attn_w_qkv[a], attn_q_gain[a], attn_k_gain[a])
            o = _moba(q.reshape(b, s, d), k.reshape(b, s, d), v.reshape(b, s, d), n_heads=N_HEADS)
            mix = (o.reshape(t, d), attn_w_o[a])
        h = _moe_ple(h, mix, p.reshape(depth, t, -1), i, g_ffn[i], moe_w_group[i], moe_b_group[i],
                     moe_w_router[i], moe_b_router[i], moe_w_gate, moe_w_up, moe_w_down, g_ple[i], ple_w_gate[i],
                     ple_w_proj[i])
    return h.reshape(b, s, d)
```

```python
import functools

import jax
import jax.numpy as jnp
import numpy as np
from jax import lax
from jax.experimental import pallas as pl
from jax.experimental.pallas import tpu as pltpu

F32 = jnp.float32
BF16 = jnp.bfloat16

EPS = 1e-6
NEG_INF = -1e30
LOG2E = 1.4426950408889634

N_HEADS = 8
MOBA_BLOCK = 256
MOBA_TOPK = 3
N_GROUPS = 4
EXPERTS_PER_GROUP = 4
N_EXPERTS = N_GROUPS * EXPERTS_PER_GROUP

V7X_VMEM_BYTES = 64 * 1024 * 1024
LANES = 128
SUBLANES = 8
VMEM_LIMIT = V7X_VMEM_BYTES * 7 // 8

ROUTER_LANES = LANES
EXPERT_LANE0 = N_GROUPS

CONV_HALO = 32
CONV_ROWS = 64
CONV_COLS = 128
CONV_SHIFT = 4
CONV_PANEL = 256


def _params(*semantics):
    return pltpu.CompilerParams(dimension_semantics=semantics, vmem_limit_bytes=VMEM_LIMIT)


def _resident(shape):
    nd = len(shape)
    return pl.BlockSpec(shape, lambda *_: (0,) * nd, pipeline_mode=pl.Buffered(1))


def _rms(x, g):
    return x * lax.rsqrt(jnp.mean(x * x, axis=-1, keepdims=True) + EPS) * g


def _sigmoid(x):
    return 1.0 / (1.0 + jnp.exp(-x))


def _conv_mixer_kernel(x_ref, g_ref, w1_ref, b1_ref, wdw_ref, bdw_ref, lng_ref, lnb_ref,
                       w2_ref, b2_ref, o_ref, xnbuf, ubuf, ubuf4, cbuf, *, ts, d, kw):
    s = pl.program_id(1)
    x = x_ref[0]
    xnbuf[...] = _rms(x, g_ref[...]).astype(BF16)

    @pl.when(s == 0)
    def _():
        ubuf[0:CONV_HALO, :] = jnp.zeros((CONV_HALO, d), F32)
        ubuf4[0:CONV_HALO, :] = jnp.zeros((CONV_HALO, d), F32)
        ubuf4[ts + CONV_HALO - SUBLANES:ts + CONV_HALO, :] = jnp.zeros((SUBLANES, d), F32)

    def glu_panel(c0):
        a = jnp.dot(xnbuf[...], w1_ref[:, c0:c0 + CONV_PANEL], preferred_element_type=F32)
        a = a + b1_ref[:, c0:c0 + CONV_PANEL]
        g = jnp.dot(xnbuf[...], w1_ref[:, d + c0:d + c0 + CONV_PANEL], preferred_element_type=F32)
        g = g + b1_ref[:, d + c0:d + c0 + CONV_PANEL]
        u = a * _sigmoid(g)
        ubuf[CONV_HALO:CONV_HALO + ts, c0:c0 + CONV_PANEL] = u
        ubuf4[CONV_HALO - CONV_SHIFT:CONV_HALO - CONV_SHIFT + ts, c0:c0 + CONV_PANEL] = u

    first = CONV_HALO - (kw - 1)

    def conv_panel(c0):
        for cc in range(c0, c0 + CONV_PANEL, CONV_COLS):
            cols = slice(cc, cc + CONV_COLS)
            for r0 in range(0, ts, CONV_ROWS):
                acc = None
                for r in range(CONV_SHIFT):
                    rows = CONV_ROWS + (SUBLANES if r else 0)
                    part = None
                    for k in range(kw):
                        j = (first + k) % SUBLANES
                        if j % CONV_SHIFT == r:
                            src = ubuf if j < CONV_SHIFT else ubuf4
                            a0 = r0 + (first + k) - j
                            term = src[a0:a0 + rows, cols] * wdw_ref[k:k + 1, cols]
                            part = term if part is None else part + term
                    if part is not None:
                        part = part[r:r + CONV_ROWS]
                        acc = part if acc is None else acc + part
                cbuf[r0:r0 + CONV_ROWS, cols] = acc
        ubuf[0:CONV_HALO, c0:c0 + CONV_PANEL] = ubuf[ts:ts + CONV_HALO, c0:c0 + CONV_PANEL]
        ubuf4[0:CONV_HALO, c0:c0 + CONV_PANEL] = ubuf4[ts:ts + CONV_HALO, c0:c0 + CONV_PANEL]

    glu_panel(0)
    for c0 in range(0, d, CONV_PANEL):
        if c0 + CONV_PANEL < d:
            glu_panel(c0 + CONV_PANEL)
        conv_panel(c0)

    c = cbuf[...] + bdw_ref[...]
    mu = jnp.mean(c, axis=-1, keepdims=True)
    xc = c - mu
    y = xc * lax.rsqrt(jnp.mean(xc * xc, axis=-1, keepdims=True) + EPS) * lng_ref[...] + lnb_ref[...]
    y = y * _sigmoid(y)
    o_ref[0] = x + jnp.dot(y.astype(BF16), w2_ref[...], preferred_element_type=F32) + b2_ref[...]


def _conv_mixer(h, g, w1, b1, wdw, bdw, lng, lnb, w2, b2, *, ts=512):
    b, s, d = h.shape
    kw = wdw.shape[0]
    assert s % ts == 0 and ts % CONV_ROWS == 0 and ts >= CONV_HALO >= kw - 1
    assert CONV_HALO % SUBLANES == 0 and d % CONV_PANEL == 0 and CONV_PANEL % CONV_COLS == 0
    assert 2 * CONV_SHIFT == SUBLANES
    row = lambda a: a.reshape(1, -1)
    kern = functools.partial(_conv_mixer_kernel, ts=ts, d=d, kw=kw)
    return pl.pallas_call(
        kern,
        out_shape=jax.ShapeDtypeStruct((b, s, d), F32),
        grid=(b, s // ts),
        in_specs=[
            pl.BlockSpec((1, ts, d), lambda i, j: (i, j, 0)),
            _resident((1, d)), _resident((d, 2 * d)), _resident((1, 2 * d)),
            _resident((kw, d)), _resident((1, d)), _resident((1, d)), _resident((1, d)),
            _resident((d, d)), _resident((1, d)),
        ],
        out_specs=pl.BlockSpec((1, ts, d), lambda i, j: (i, j, 0)),
        scratch_shapes=[pltpu.VMEM((ts, d), BF16), pltpu.VMEM((ts + CONV_HALO, d), F32),
                        pltpu.VMEM((ts + CONV_HALO, d), F32), pltpu.VMEM((ts, d), F32)],
        compiler_params=_params("arbitrary", "arbitrary"),
        name="conv_mixer",
    )(h, row(g), w1.astype(BF16), row(b1), wdw, row(bdw), row(lng), row(lnb), w2.astype(BF16), row(b2))


def _qkv_kernel(x_ref, g_ref, w_ref, qg_ref, kg_ref, q_ref, k_ref, v_ref, *, d, hd):
    xn = _rms(x_ref[...], g_ref[...]).astype(BF16)
    qkv = jnp.dot(xn, w_ref[...], preferred_element_type=F32)
    for h in range(d // hd):
        cols = slice(h * hd, (h + 1) * hd)
        q_ref[:, cols] = _rms(qkv[:, h * hd:(h + 1) * hd], qg_ref[...]).astype(BF16)
        k_ref[:, cols] = _rms(qkv[:, d + h * hd:d + (h + 1) * hd], kg_ref[...]).astype(BF16)
    v_ref[...] = qkv[:, 2 * d:].astype(BF16)


def _qkv(h2d, g, w, qg, kg, *, tm=512):
    t, d = h2d.shape
    hd = qg.shape[0]
    kern = functools.partial(_qkv_kernel, d=d, hd=hd)
    tile = pl.BlockSpec((tm, d), lambda i: (i, 0))
    return pl.pallas_call(
        kern,
        out_shape=[jax.ShapeDtypeStruct((t, d), BF16)] * 3,
        grid=(t // tm,),
        in_specs=[tile, _resident((1, d)), _resident((d, 3 * d)), _resident((1, hd)), _resident((1, hd))],
        out_specs=[tile, tile, tile],
        compiler_params=_params("arbitrary"),
        name="attn_qkv",
    )(h2d, g.reshape(1, d), w.astype(BF16), qg.reshape(1, hd), kg.reshape(1, hd))


def _moba_kernel(slope_ref, q_ref, k_ref, v_ref, kaux_ref, vaux_ref, o_ref, kaug, vaug, *bufs,
                 nb, blk, hd, topk):
    scale = hd ** -0.5
    negz = NEG_INF / scale
    slope = slope_ref[0][:, :1]
    kaug[:, :hd] = k_ref[0]
    kaug[:, hd:] = kaux_ref[...]
    vaug[:, :hd] = v_ref[0]
    vaug[:, hd:] = vaux_ref[...]

    km = jnp.concatenate([jnp.mean(k_ref[0, n * blk:(n + 1) * blk, :].astype(F32), axis=0, keepdims=True)
                          for n in range(nb)], axis=0)
    hi = km.astype(BF16).astype(F32)
    mid = (km - hi).astype(BF16).astype(F32)
    lo = (km - hi - mid).astype(BF16).astype(F32)
    pad = [jnp.zeros((-3 * nb % (2 * SUBLANES), hd), F32)] if 3 * nb % (2 * SUBLANES) else []
    km3 = jnp.concatenate([hi, mid, lo] + pad, axis=0).astype(BF16)

    r = lax.broadcasted_iota(jnp.int32, (blk, blk), 0)
    c = lax.broadcasted_iota(jnp.int32, (blk, blk), 1)
    causal = r >= c
    keypos = lax.broadcasted_iota(jnp.int32, (1, blk), 1).astype(F32)
    nt = (((1,), (1,)), ((), ()))

    def scores(j):
        zbuf = bufs[j]
        q = q_ref[0, j * blk:(j + 1) * blk, :]
        select = j > topk
        if select:
            g3 = lax.dot_general(km3, q, nt, preferred_element_type=F32)
            gate = g3[0:nb] + g3[nb:2 * nb] + g3[2 * nb:3 * nb]
            g = [gate[m:m + 1, :] for m in range(j)]
            rows = []
            for n in range(j):
                beaten = jnp.zeros((1, blk), F32)
                for m in range(j):
                    if m != n:
                        ahead = (g[m] > g[n]) | (g[m] == g[n]) if m < n else (g[m] > g[n])
                        beaten = beaten + jnp.where(ahead, 1.0, 0.0)
                rows.append(jnp.where(beaten < topk, 0.0, negz))
            bias_t = jnp.concatenate(rows + [jnp.zeros((LANES - j, blk), F32)], axis=0)
            q = jnp.concatenate([q, bias_t.T.astype(BF16)], axis=1)
        mrun = None
        for n in range(j + 1):
            keys = kaug[n * blk:(n + 1) * blk, :] if select else k_ref[0, n * blk:(n + 1) * blk, :]
            z = lax.dot_general(q, keys, nt, preferred_element_type=F32)
            u = z + (slope / scale) * (keypos + float(n * blk))
            if n == j:
                u = jnp.where(causal, u, negz)
            zbuf[:, n * blk:(n + 1) * blk] = u
            for half in range(blk // LANES):
                f = u[:, half * LANES:(half + 1) * LANES]
                mrun = f if mrun is None else jnp.maximum(mrun, f)
        return jnp.broadcast_to(jnp.max(mrun, axis=-1, keepdims=True), (blk, LANES))

    def attend(j, mb):
        zbuf, pbuf = bufs[j], bufs[nb + j]
        for n in range(j + 1):
            for half in range(blk // LANES):
                cols = slice(n * blk + half * LANES, n * blk + (half + 1) * LANES)
                pbuf[:, cols] = jnp.exp2((zbuf[:, cols] - mb) * (scale * LOG2E)).astype(BF16)
        acc = jnp.dot(pbuf[:, :(j + 1) * blk], vaug[:(j + 1) * blk, :], preferred_element_type=F32)
        o_ref[0, j * blk:(j + 1) * blk, :] = (acc[:, :hd] / acc[:, hd:hd + 1]).astype(BF16)

    order = list(range(nb - 1, -1, -1))
    mb_next = scores(order[0])
    for i, j in enumerate(order):
        mb = mb_next
        if i + 1 < nb:
            mb_next = scores(order[i + 1])
        attend(j, mb)


def _moba(q, k, v, *, n_heads):
    b, s, d = q.shape
    hd = d // n_heads
    blk = MOBA_BLOCK
    assert s % blk == 0 and hd == LANES and blk % LANES == 0
    nb = s // blk
    assert nb <= LANES
    topk = max(1, min(MOBA_TOPK, nb - 1))
    slopes = jnp.exp2(-8.0 * jnp.arange(1, n_heads + 1, dtype=F32) / n_heads)
    slopes = jnp.broadcast_to(slopes[:, None, None], (n_heads, 1, LANES))
    kaux = np.zeros((s, LANES), np.float32)
    kaux[np.arange(s), np.arange(s) // blk] = 1.0
    vaux = np.zeros((s, LANES), np.float32)
    vaux[:, 0] = 1.0
    kern = functools.partial(_moba_kernel, nb=nb, blk=blk, hd=hd, topk=topk)
    head = pl.BlockSpec((1, s, hd), lambda i, h: (i, 0, h))
    return pl.pallas_call(
        kern,
        out_shape=jax.ShapeDtypeStruct((b, s, d), BF16),
        grid=(b, n_heads),
        in_specs=[pl.BlockSpec((1, 1, LANES), lambda i, h: (h, 0, 0)), head, head, head,
                  _resident((s, LANES)), _resident((s, LANES))],
        out_specs=head,
        scratch_shapes=([pltpu.VMEM((s, hd + LANES), BF16), pltpu.VMEM((s, hd + LANES), BF16)]
                        + [pltpu.VMEM((blk, (j + 1) * blk), F32) for j in range(nb)]
                        + [pltpu.VMEM((blk, (j + 1) * blk), BF16) for j in range(nb)]),
        compiler_params=_params("arbitrary", "arbitrary"),
        name="moba_attn",
    )(slopes, q, k, v, jnp.asarray(kaux, BF16), jnp.asarray(vaux, BF16))


def _pair_order(n):
    todo = [(a, b) for a in range(n) for b in range(a + 1, n)]
    order = [todo.pop(0)]
    while todo:
        a, b = order[-1]
        nxt = next((p for p in todo if p[0] == a or p[1] == b), todo[0])
        todo.remove(nxt)
        order.append(nxt)
    return order


PAIRS = _pair_order(EXPERTS_PER_GROUP)
N_CLASSES = N_GROUPS * len(PAIRS)
META_CLASS, META_RANK, META_WA, META_WB = 0, 1, 2, 3
EXPERT_TILE = 256


def _router_kernel(h_ref, g_ref, wr_ref, br_ref, tri_ref, route_ref, cnt_ref, running):
    _route_tile(h_ref[...], g_ref, wr_ref, br_ref, tri_ref, route_ref, cnt_ref, running)


def _out_proj_router_kernel(h_ref, a_ref, w_ref, g_ref, wr_ref, br_ref, tri_ref, o_ref, route_ref, cnt_ref,
                            running):
    h = h_ref[...] + jnp.dot(a_ref[...], w_ref[...], preferred_element_type=F32)
    o_ref[...] = h
    _route_tile(h, g_ref, wr_ref, br_ref, tri_ref, route_ref, cnt_ref, running)


def _route_tile(h, g_ref, wr_ref, br_ref, tri_ref, route_ref, cnt_ref, running):
    @pl.when(pl.program_id(0) == 0)
    def _():
        running[...] = jnp.zeros_like(running)

    xn = _rms(h, g_ref[...])
    wr = wr_ref[...]
    xh, wh = xn.astype(BF16), wr.astype(BF16)
    xl, wl = (xn - xh.astype(F32)).astype(BF16), (wr - wh.astype(F32)).astype(BF16)
    logits = (jnp.dot(xh, wh, preferred_element_type=F32) + jnp.dot(xl, wh, preferred_element_type=F32)
              + jnp.dot(xh, wl, preferred_element_type=F32)) + br_ref[...]
    lt = logits.T
    tokens = lt.shape[1]

    def first_max(rows):
        mx = functools.reduce(jnp.maximum, rows)
        idx = jnp.full_like(mx, float(len(rows) - 1))
        for i in range(len(rows) - 2, -1, -1):
            idx = jnp.where(rows[i] == mx, float(i), idx)
        return mx, idx

    gl = [lt[i:i + 1, :] for i in range(N_GROUPS)]
    gmax, gidx = first_max(gl)
    g_w = 1.0 / sum(jnp.exp(x - gmax) for x in gl)
    el = []
    for e in range(EXPERTS_PER_GROUP):
        x = lt[EXPERT_LANE0 + e:EXPERT_LANE0 + e + 1, :]
        for grp in range(1, N_GROUPS):
            lane = EXPERT_LANE0 + grp * EXPERTS_PER_GROUP + e
            x = jnp.where(gidx == float(grp), lt[lane:lane + 1, :], x)
        el.append(x)
    m1, e1 = first_max(el)
    m2, e2 = first_max([jnp.where(e1 == float(e), -jnp.inf, x) for e, x in enumerate(el)])
    p2 = jnp.exp(m2 - m1)
    w1 = g_w / (1.0 + p2)
    w2 = g_w * p2 / (1.0 + p2)
    ea, eb = jnp.minimum(e1, e2), jnp.maximum(e1, e2)
    wa, wb = jnp.where(e1 < e2, w1, w2), jnp.where(e1 < e2, w2, w1)
    pair = sum(jnp.where((ea == a) & (eb == b), float(n), 0.0) for n, (a, b) in enumerate(PAIRS))
    cls = gidx * float(len(PAIRS)) + pair
    class_row = lax.broadcasted_iota(jnp.int32, (ROUTER_LANES, tokens), 0).astype(F32)
    onehot = jnp.where(class_row == cls, 1.0, 0.0)
    before = jnp.dot(onehot.astype(BF16), tri_ref[...], preferred_element_type=F32) + running[...]
    rank = jnp.sum(onehot * before, axis=0, keepdims=True)
    running[...] += jnp.sum(onehot, axis=1, keepdims=True)
    cnt_ref[...] = running[...]

    records = {META_CLASS: cls, META_RANK: rank, META_WA: wa, META_WB: wb}
    route_ref[...] = jnp.concatenate([records.get(i, jnp.zeros_like(cls)) for i in range(SUBLANES)], axis=0)


def _router(h2d, g, w_group, b_group, w_router, b_router, mix=None, *, tm=512):
    t, d = h2d.shape
    ng, _, ne = w_router.shape
    assert (ng, ne) == (N_GROUPS, EXPERTS_PER_GROUP) and N_CLASSES <= ROUTER_LANES
    wr = jnp.concatenate([w_group, jnp.transpose(w_router, (1, 0, 2)).reshape(d, ng * ne)], axis=1)
    br = jnp.concatenate([b_group, b_router.reshape(ng * ne)])
    pad = ROUTER_LANES - wr.shape[1]
    wr = jnp.pad(wr, ((0, 0), (0, pad)))
    br = jnp.pad(br, (0, pad)).reshape(1, ROUTER_LANES)
    tri = jnp.asarray(np.triu(np.ones((tm, tm), np.float32), 1), BF16)
    tile = pl.BlockSpec((tm, d), lambda i: (i, 0))
    route_shapes = [jax.ShapeDtypeStruct((SUBLANES, t), F32), jax.ShapeDtypeStruct((ROUTER_LANES, 1), F32)]
    route_specs = [pl.BlockSpec((SUBLANES, tm), lambda i: (0, i)), pl.BlockSpec((ROUTER_LANES, 1), lambda i: (0, 0))]
    route_in = [_resident((1, d)), _resident((d, ROUTER_LANES)), _resident((1, ROUTER_LANES)), _resident((tm, tm))]
    common = dict(grid=(t // tm,), scratch_shapes=[pltpu.VMEM((ROUTER_LANES, 1), F32)],
                  compiler_params=_params("arbitrary"))
    if mix is None:
        return pl.pallas_call(
            _router_kernel, out_shape=route_shapes, in_specs=[tile] + route_in, out_specs=route_specs,
            name="moe_router", **common,
        )(h2d, g.reshape(1, d), wr, br, tri)
    a, w = mix
    return pl.pallas_call(
        _out_proj_router_kernel,
        out_shape=[jax.ShapeDtypeStruct((t, d), F32)] + route_shapes,
        in_specs=[tile, pl.BlockSpec((tm, a.shape[1]), lambda i: (i, 0)), _resident(w.shape)] + route_in,
        out_specs=[tile] + route_specs,
        name="attn_out_proj_router", **common,
    )(h2d, a, w.astype(BF16), g.reshape(1, d), wr, br, tri)


ITEM_VALID, ITEM_FIRST_OF_TILE, ITEM_NEW_A, ITEM_NEW_B, ITEM_MORE_A, ITEM_MORE_B = 1, 2, 4, 8, 16, 32


def _prefix_sum(v):
    ids = jnp.arange(v.shape[0], dtype=jnp.int32)
    return jnp.sum(jnp.where(ids[None, :] <= ids[:, None], v[None, :], 0), axis=1)


def _sorted_layout(route, counts, n_tiles, n_items):
    cls = route[META_CLASS].astype(jnp.int32)
    rank = route[META_RANK].astype(jnp.int32)
    cnt = counts[:N_CLASSES, 0].astype(jnp.int32)
    ids = jnp.arange(N_CLASSES, dtype=jnp.int32)
    end = _prefix_sum(cnt)
    start = end - cnt
    pos = jnp.sum(jnp.where(cls[None, :] == ids[:, None], start[:, None], 0), axis=0) + rank

    first = start // EXPERT_TILE
    last = jnp.where(cnt > 0, (end - 1) // EXPERT_TILE, first - 1)
    items = last - first + 1
    item_end = _prefix_sum(items)
    k = jnp.arange(n_items, dtype=jnp.int32)
    item_cls = jnp.sum((item_end[None, :] <= k[:, None]).astype(jnp.int32), axis=1)
    valid = item_cls < N_CLASSES
    item_cls = jnp.where(valid, item_cls, jnp.max(jnp.where(cnt > 0, ids, 0)))
    sel = item_cls[:, None] == ids[None, :]
    pick = lambda v: jnp.sum(jnp.where(sel, v[None, :], 0), axis=1)
    tile = jnp.where(valid, pick(first) + k - pick(item_end - items), n_tiles - 1)
    lo = jnp.where(valid, jnp.clip(pick(start) - tile * EXPERT_TILE, 0, EXPERT_TILE), 0)
    hi = jnp.where(valid, jnp.clip(pick(end) - tile * EXPERT_TILE, 0, EXPERT_TILE), 0)

    grp, pair = item_cls // len(PAIRS), item_cls % len(PAIRS)
    ea = grp * EXPERTS_PER_GROUP + sum(jnp.where(pair == n, a, 0) for n, (a, _) in enumerate(PAIRS))
    eb = grp * EXPERTS_PER_GROUP + sum(jnp.where(pair == n, b, 0) for n, (_, b) in enumerate(PAIRS))
    prev = lambda v: jnp.concatenate([jnp.full((1,), -1, jnp.int32), v[:-1]])
    new_a, new_b = ea != prev(ea), eb != prev(eb)

    def upcoming(e, is_new):
        later = is_new[None, :] & (k[None, :] > k[:, None])
        nxt = jnp.min(jnp.where(later, k[None, :], n_items), axis=1)
        return jnp.sum(jnp.where(k[None, :] == nxt[:, None], e[None, :], 0), axis=1), nxt < n_items

    next_a, more_a = upcoming(ea, new_a)
    next_b, more_b = upcoming(eb, new_b)
    flags = (jnp.where(valid, ITEM_VALID, 0) + jnp.where(valid & (tile != prev(tile)), ITEM_FIRST_OF_TILE, 0)
             + jnp.where(new_a, ITEM_NEW_A, 0) + jnp.where(new_b, ITEM_NEW_B, 0)
             + jnp.where(more_a, ITEM_MORE_A, 0) + jnp.where(more_b, ITEM_MORE_B, 0))
    return pos, (tile, ea, eb, next_a, next_b, lo, hi, flags)


DISPATCH_CHUNK = 2048
ROW_DMA_UNROLL = 8


def _row_copy(src, dst, i, j, sem):
    return pltpu.make_async_copy(src.at[pl.ds(i, 1)], dst.at[pl.ds(j, 1)], sem)


DISPATCH_ROWS = 256


def _dispatch_kernel(pos_ref, h_ref, g_ref, route_ref, hxs_ref, rows, sems, *, d):
    s = pl.program_id(0)
    n = pl.num_programs(0) - 1
    prev = (s + 1) % 2

    @pl.when(s > 0)
    def _():
        base = (s - 1) * DISPATCH_CHUNK

        def issue(r, carry):
            _row_copy(rows.at[prev], hxs_ref, r, pos_ref[base + r], sems.at[prev]).start()
            return carry

        lax.fori_loop(0, DISPATCH_CHUNK, issue, 0, unroll=ROW_DMA_UNROLL)

    @pl.when(s < n)
    def _():
        def build(i, carry):
            r0 = pl.multiple_of(i * DISPATCH_ROWS, DISPATCH_ROWS)
            rec = route_ref[:, pl.ds(r0, DISPATCH_ROWS)]
            pad = jnp.zeros((LANES - SUBLANES, DISPATCH_ROWS), F32)
            rows[s % 2, pl.ds(r0, DISPATCH_ROWS), 0, :d] = _rms(h_ref[pl.ds(r0, DISPATCH_ROWS), :], g_ref[...])
            rows[s % 2, pl.ds(r0, DISPATCH_ROWS), 0, d:] = jnp.concatenate([rec, pad], axis=0).T
            return carry

        lax.fori_loop(0, DISPATCH_CHUNK // DISPATCH_ROWS, build, 0)

    @pl.when(s > 0)
    def _():
        pltpu.make_async_copy(rows.at[prev], hxs_ref.at[pl.ds(0, DISPATCH_CHUNK)], sems.at[prev]).wait()


def _dispatch(h2d, g, route, pos):
    t, d = h2d.shape
    w = d + ROUTER_LANES
    assert t % DISPATCH_CHUNK == 0 and DISPATCH_CHUNK % DISPATCH_ROWS == 0
    n = t // DISPATCH_CHUNK
    return pl.pallas_call(
        functools.partial(_dispatch_kernel, d=d),
        out_shape=jax.ShapeDtypeStruct((t, 1, w), F32),
        grid_spec=pltpu.PrefetchScalarGridSpec(
            num_scalar_prefetch=1,
            grid=(n + 1,),
            in_specs=[pl.BlockSpec((DISPATCH_CHUNK, d), lambda s, pos: (jnp.minimum(s, n - 1), 0)),
                      pl.BlockSpec((1, d), lambda s, pos: (0, 0)),
                      pl.BlockSpec((SUBLANES, DISPATCH_CHUNK), lambda s, pos: (0, jnp.minimum(s, n - 1)))],
            out_specs=pl.BlockSpec(memory_space=pl.ANY),
            scratch_shapes=[pltpu.VMEM((2, DISPATCH_CHUNK, 1, w), F32), pltpu.SemaphoreType.DMA((2,))],
        ),
        compiler_params=_params("arbitrary"),
        name="moe_dispatch",
    )(pos, h2d, g.reshape(1, d), route)


def _experts_kernel(tile_ref, ea_ref, eb_ref, na_ref, nb_ref, lo_ref, hi_ref, flag_ref, x_ref, wg_hbm, wu_hbm,
                    wd_hbm, y_ref, w1, w2, xs, stage_g, stage_u, stage_d, sems, *, d, hid, layer):
    del tile_ref
    k = pl.program_id(0)
    flags = flag_ref[k]

    def refresh(slot, now_ref, next_ref, new_bit, more_bit):
        def copies(e):
            return [pltpu.make_async_copy(src.at[layer, e], dst.at[slot], sems.at[slot, n])
                    for n, (src, dst) in enumerate(((wg_hbm, stage_g), (wu_hbm, stage_u), (wd_hbm, stage_d)))]

        @pl.when((flags & new_bit) != 0)
        def _():
            @pl.when(k == 0)
            def _():
                for c in copies(now_ref[k]):
                    c.start()

            for c in copies(now_ref[k]):
                c.wait()
            w1[:, 2 * slot * hid:(2 * slot + 1) * hid] = stage_g[slot].astype(BF16)
            w1[:, (2 * slot + 1) * hid:(2 * slot + 2) * hid] = stage_u[slot].astype(BF16)
            w2[slot * hid:(slot + 1) * hid, :] = stage_d[slot].astype(BF16)

            @pl.when((flags & more_bit) != 0)
            def _():
                for c in copies(next_ref[k]):
                    c.start()

    refresh(0, ea_ref, na_ref, ITEM_NEW_A, ITEM_MORE_A)
    refresh(1, eb_ref, nb_ref, ITEM_NEW_B, ITEM_MORE_B)

    @pl.when((flags & ITEM_VALID) != 0)
    def _():
        row = lax.broadcasted_iota(jnp.int32, (EXPERT_TILE, 1), 0)
        inside = (row >= lo_ref[k]) & (row < hi_ref[k])
        xs[...] = x_ref[:, 0, :]
        wa = jnp.where(inside, xs[:, d + META_WA:d + META_WA + 1], 0.0)
        wb = jnp.where(inside, xs[:, d + META_WB:d + META_WB + 1], 0.0)
        gu = jnp.dot(xs[:, :d].astype(BF16), w1[...], preferred_element_type=F32)
        ga, gb = gu[:, 0:hid], gu[:, 2 * hid:3 * hid]
        ha = ga * _sigmoid(ga) * gu[:, hid:2 * hid] * wa
        hb = gb * _sigmoid(gb) * gu[:, 3 * hid:] * wb
        y = jnp.dot(jnp.concatenate([ha, hb], axis=1).astype(BF16), w2[...], preferred_element_type=F32)

        @pl.when((flags & ITEM_FIRST_OF_TILE) != 0)
        def _():
            y_ref[...] = y

        @pl.when((flags & ITEM_FIRST_OF_TILE) == 0)
        def _():
            y_ref[...] += y


def _experts(hxs, items, w_gate, w_up, w_down, layer, *, d):
    t, _, w = hxs.shape
    hid = w_gate.shape[-1]
    n_items = items[0].shape[0]
    hbm = pl.BlockSpec(memory_space=pl.ANY)
    return pl.pallas_call(
        functools.partial(_experts_kernel, d=d, hid=hid, layer=layer),
        out_shape=jax.ShapeDtypeStruct((t, d), F32),
        grid_spec=pltpu.PrefetchScalarGridSpec(
            num_scalar_prefetch=len(items),
            grid=(n_items,),
            in_specs=[pl.BlockSpec((EXPERT_TILE, 1, w), lambda k, tile, *_: (tile[k], 0, 0)), hbm, hbm, hbm],
            out_specs=pl.BlockSpec((EXPERT_TILE, d), lambda k, tile, *_: (tile[k], 0)),
            scratch_shapes=[pltpu.VMEM((d, 4 * hid), BF16), pltpu.VMEM((2 * hid, d), BF16),
                            pltpu.VMEM((EXPERT_TILE, w), F32),
                            pltpu.VMEM((2, d, hid), F32), pltpu.VMEM((2, d, hid), F32), pltpu.VMEM((2, hid, d), F32),
                            pltpu.SemaphoreType.DMA((2, 3))],
        ),
        compiler_params=_params("arbitrary"),
        name="moe_experts",
    )(*items, hxs, w_gate, w_up, w_down)


def _combine_ple_kernel(pos_ref, h_ref, p_ref, g_ref, wg_ref, wp_ref, ys_ref, o_ref, ybuf, sems, *, tm):
    i = pl.program_id(0)
    n = pl.num_programs(0)
    slot = i % 2

    def wait(s):
        pltpu.make_async_copy(ys_ref.at[pl.ds(0, tm)], ybuf.at[s], sems.at[s]).wait()

    @pl.when(i == 0)
    def _():
        def issue(r, carry):
            _row_copy(ys_ref, ybuf.at[0], pos_ref[r], r, sems.at[0]).start()
            return carry
        lax.fori_loop(0, tm, issue, 0, unroll=ROW_DMA_UNROLL)

    nxt = jnp.minimum(i + 1, n - 1) * tm
    for r in range(tm):
        _row_copy(ys_ref, ybuf.at[1 - slot], pos_ref[nxt + r], r, sems.at[1 - slot]).start()

    wait(slot)
    h = h_ref[...] + ybuf[slot]
    gate = _sigmoid(jnp.dot(_rms(h, g_ref[...]).astype(BF16), wg_ref[...], preferred_element_type=F32))
    proj = jnp.dot(p_ref[...].astype(BF16), wp_ref[...], preferred_element_type=F32)
    o_ref[...] = h + gate * proj

    @pl.when(i == n - 1)
    def _():
        wait(1 - slot)


def _combine_ple(h2d, ys, pos, p3d, layer, g, w_gate, w_proj, *, tm=512):
    t, d = h2d.shape
    pd = p3d.shape[-1]
    tile = pl.BlockSpec((tm, d), lambda i, pos: (i, 0))
    res = lambda shape: pl.BlockSpec(shape, lambda i, pos: (0,) * len(shape), pipeline_mode=pl.Buffered(1))
    return pl.pallas_call(
        functools.partial(_combine_ple_kernel, tm=tm),
        out_shape=jax.ShapeDtypeStruct((t, d), F32),
        grid_spec=pltpu.PrefetchScalarGridSpec(
            num_scalar_prefetch=1,
            grid=(t // tm,),
            in_specs=[tile, pl.BlockSpec((None, tm, pd), lambda i, pos: (layer, i, 0)), res((1, d)), res((d, d)),
                      res((pd, d)), pl.BlockSpec(memory_space=pl.ANY)],
            out_specs=tile,
            scratch_shapes=[pltpu.VMEM((2, tm, d), F32), pltpu.SemaphoreType.DMA((2,))],
        ),
        compiler_params=_params("arbitrary"),
        name="moe_combine_ple",
    )(pos, h2d, p3d, g.reshape(1, d), w_gate.astype(BF16), w_proj.astype(BF16), ys)


def _moe_ple(h2d, mix, p3d, layer, g_ffn, w_group, b_group, w_router, b_router, w_gate, w_up, w_down, g_ple,
             ple_gate, ple_proj):
    t, d = h2d.shape
    assert t % EXPERT_TILE == 0
    n_tiles = t // EXPERT_TILE
    n_items = n_tiles + N_CLASSES - 1
    if mix is None:
        route, counts = _router(h2d, g_ffn, w_group, b_group, w_router, b_router)
    else:
        h2d, route, counts = _router(h2d, g_ffn, w_group, b_group, w_router, b_router, mix)
    pos, items = _sorted_layout(route, counts, n_tiles, n_items)
    hxs = _dispatch(h2d, g_ffn, route, pos)
    ys = _experts(hxs, items, w_gate, w_up, w_down, layer, d=d)
    return _combine_ple(h2d, ys, pos, p3d, layer, g_ple, ple_gate, ple_proj)


def kernel(x, p, g_mix, g_ffn, g_ple, conv_w_pw1, conv_b_pw1, conv_w_dw, conv_b_dw, conv_ln_g, conv_ln_b, conv_w_pw2, conv_b_pw2, attn_w_qkv, attn_q_gain, attn_k_gain, attn_w_o, moe_w_group, moe_b_group, moe_w_router, moe_b_router, moe_w_gate, moe_w_up, moe_w_down, ple_w_gate, ple_w_proj):
    b, s, d = x.shape
    depth = g_mix.shape[0]
    t = b * s
    h = x
    for i in range(depth):
        if i % 2 == 0:
            c = i // 2
            h = _conv_mixer(h.reshape(b, s, d), g_mix[i], conv_w_pw1[c], conv_b_pw1[c], conv_w_dw[c],
                            conv_b_dw[c], conv_ln_g[c], conv_ln_b[c], conv_w_pw2[c], conv_b_pw2[c])
            h = h.reshape(t, d)
            mix = None
        else:
            a = i // 2
            h = h.reshape(t, d)
            q, k, v = _qkv(h, g_mix[i], attn_w_qkv[a], attn_q_gain[a], attn_k_gain[a])
            o = _moba(q.reshape(b, s, d), k.reshape(b, s, d), v.reshape(b, s, d), n_heads=N_HEADS)
            mix = (o.reshape(t, d), attn_w_o[a])
        h = _moe_ple(h, mix, p.reshape(depth, t, -1), i, g_ffn[i], moe_w_group[i], moe_b_group[i],
                     moe_w_router[i], moe_b_router[i], moe_w_gate, moe_w_up, moe_w_down, g_ple[i], ple_w_gate[i],
                     ple_w_proj[i])
    return h.reshape(b, s, d)
```

```python
import functools

import jax
import jax.numpy as jnp
import numpy as np
from jax import lax
from jax.experimental import pallas as pl
from jax.experimental.pallas import tpu as pltpu

F32 = jnp.float32
BF16 = jnp.bfloat16

EPS = 1e-6
NEG_INF = -1e30
LOG2E = 1.4426950408889634

N_HEADS = 8
MOBA_BLOCK = 256
MOBA_TOPK = 3
N_GROUPS = 4
EXPERTS_PER_GROUP = 4
N_EXPERTS = N_GROUPS * EXPERTS_PER_GROUP

V7X_VMEM_BYTES = 64 * 1024 * 1024
LANES = 128
SUBLANES = 8
VMEM_LIMIT = V7X_VMEM_BYTES * 7 // 8

ROUTER_LANES = LANES
EXPERT_LANE0 = N_GROUPS

CONV_HALO = 32
CONV_ROWS = 64
CONV_COLS = 128
CONV_SHIFT = 4
CONV_PANEL = 256


def _params(*semantics):
    return pltpu.CompilerParams(dimension_semantics=semantics, vmem_limit_bytes=VMEM_LIMIT)


def _resident(shape):
    nd = len(shape)
    return pl.BlockSpec(shape, lambda *_: (0,) * nd, pipeline_mode=pl.Buffered(1))


def _rms(x, g):
    return x * lax.rsqrt(jnp.mean(x * x, axis=-1, keepdims=True) + EPS) * g


def _sigmoid(x):
    return 1.0 / (1.0 + jnp.exp(-x))


def _conv_mixer_kernel(x_ref, g_ref, w1_ref, b1_ref, wdw_ref, bdw_ref, lng_ref, lnb_ref,
                       w2_ref, b2_ref, o_ref, xnbuf, ubuf, ubuf4, cbuf, *, ts, d, kw):
    s = pl.program_id(1)
    x = x_ref[0]
    xnbuf[...] = _rms(x, g_ref[...]).astype(BF16)

    @pl.when(s == 0)
    def _():
        ubuf[0:CONV_HALO, :] = jnp.zeros((CONV_HALO, d), F32)
        ubuf4[0:CONV_HALO, :] = jnp.zeros((CONV_HALO, d), F32)
        ubuf4[ts + CONV_HALO - SUBLANES:ts + CONV_HALO, :] = jnp.zeros((SUBLANES, d), F32)

    def glu_panel(c0):
        a = jnp.dot(xnbuf[...], w1_ref[:, c0:c0 + CONV_PANEL], preferred_element_type=F32)
        a = a + b1_ref[:, c0:c0 + CONV_PANEL]
        g = jnp.dot(xnbuf[...], w1_ref[:, d + c0:d + c0 + CONV_PANEL], preferred_element_type=F32)
        g = g + b1_ref[:, d + c0:d + c0 + CONV_PANEL]
        u = a * _sigmoid(g)
        ubuf[CONV_HALO:CONV_HALO + ts, c0:c0 + CONV_PANEL] = u
        ubuf4[CONV_HALO - CONV_SHIFT:CONV_HALO - CONV_SHIFT + ts, c0:c0 + CONV_PANEL] = u

    first = CONV_HALO - (kw - 1)

    def conv_panel(c0):
        for cc in range(c0, c0 + CONV_PANEL, CONV_COLS):
            cols = slice(cc, cc + CONV_COLS)
            for r0 in range(0, ts, CONV_ROWS):
                acc = None
                for r in range(CONV_SHIFT):
                    rows = CONV_ROWS + (SUBLANES if r else 0)
                    part = None
                    for k in range(kw):
                        j = (first + k) % SUBLANES
                        if j % CONV_SHIFT == r:
                            src = ubuf if j < CONV_SHIFT else ubuf4
                            a0 = r0 + (first + k) - j
                            term = src[a0:a0 + rows, cols] * wdw_ref[k:k + 1, cols]
                            part = term if part is None else part + term
                    if part is not None:
                        part = part[r:r + CONV_ROWS]
                        acc = part if acc is None else acc + part
                cbuf[r0:r0 + CONV_ROWS, cols] = acc
        ubuf[0:CONV_HALO, c0:c0 + CONV_PANEL] = ubuf[ts:ts + CONV_HALO, c0:c0 + CONV_PANEL]
        ubuf4[0:CONV_HALO, c0:c0 + CONV_PANEL] = ubuf4[ts:ts + CONV_HALO, c0:c0 + CONV_PANEL]

    glu_panel(0)
    for c0 in range(0, d, CONV_PANEL):
        if c0 + CONV_PANEL < d:
            glu_panel(c0 + CONV_PANEL)
        conv_panel(c0)

    c = cbuf[...] + bdw_ref[...]
    mu = jnp.mean(c, axis=-1, keepdims=True)
    xc = c - mu
    y = xc * lax.rsqrt(jnp.mean(xc * xc, axis=-1, keepdims=True) + EPS) * lng_ref[...] + lnb_ref[...]
    y = y * _sigmoid(y)
    o_ref[0] = x + jnp.dot(y.astype(BF16), w2_ref[...], preferred_element_type=F32) + b2_ref[...]


def _conv_mixer(h, g, w1, b1, wdw, bdw, lng, lnb, w2, b2, *, ts=512):
    b, s, d = h.shape
    kw = wdw.shape[0]
    assert s % ts == 0 and ts % CONV_ROWS == 0 and ts >= CONV_HALO >= kw - 1
    assert CONV_HALO % SUBLANES == 0 and d % CONV_PANEL == 0 and CONV_PANEL % CONV_COLS == 0
    assert 2 * CONV_SHIFT == SUBLANES
    row = lambda a: a.reshape(1, -1)
    kern = functools.partial(_conv_mixer_kernel, ts=ts, d=d, kw=kw)
    return pl.pallas_call(
        kern,
        out_shape=jax.ShapeDtypeStruct((b, s, d), F32),
        grid=(b, s // ts),
        in_specs=[
            pl.BlockSpec((1, ts, d), lambda i, j: (i, j, 0)),
            _resident((1, d)), _resident((d, 2 * d)), _resident((1, 2 * d)),
            _resident((kw, d)), _resident((1, d)), _resident((1, d)), _resident((1, d)),
            _resident((d, d)), _resident((1, d)),
        ],
        out_specs=pl.BlockSpec((1, ts, d), lambda i, j: (i, j, 0)),
        scratch_shapes=[pltpu.VMEM((ts, d), BF16), pltpu.VMEM((ts + CONV_HALO, d), F32),
                        pltpu.VMEM((ts + CONV_HALO, d), F32), pltpu.VMEM((ts, d), F32)],
        compiler_params=_params("arbitrary", "arbitrary"),
        name="conv_mixer",
    )(h, row(g), w1.astype(BF16), row(b1), wdw, row(bdw), row(lng), row(lnb), w2.astype(BF16), row(b2))


def _qkv_kernel(x_ref, g_ref, w_ref, qg_ref, kg_ref, q_ref, k_ref, v_ref, *, d, hd):
    xn = _rms(x_ref[...], g_ref[...]).astype(BF16)
    qkv = jnp.dot(xn, w_ref[...], preferred_element_type=F32)
    for h in range(d // hd):
        cols = slice(h * hd, (h + 1) * hd)
        q_ref[:, cols] = _rms(qkv[:, h * hd:(h + 1) * hd], qg_ref[...]).astype(BF16)
        k_ref[:, cols] = _rms(qkv[:, d + h * hd:d + (h + 1) * hd], kg_ref[...]).astype(BF16)
    v_ref[...] = qkv[:, 2 * d:].astype(BF16)


def _qkv(h2d, g, w, qg, kg, *, tm=512):
    t, d = h2d.shape
    hd = qg.shape[0]
    kern = functools.partial(_qkv_kernel, d=d, hd=hd)
    tile = pl.BlockSpec((tm, d), lambda i: (i, 0))
    return pl.pallas_call(
        kern,
        out_shape=[jax.ShapeDtypeStruct((t, d), BF16)] * 3,
        grid=(t // tm,),
        in_specs=[tile, _resident((1, d)), _resident((d, 3 * d)), _resident((1, hd)), _resident((1, hd))],
        out_specs=[tile, tile, tile],
        compiler_params=_params("arbitrary"),
        name="attn_qkv",
    )(h2d, g.reshape(1, d), w.astype(BF16), qg.reshape(1, hd), kg.reshape(1, hd))


def _moba_kernel(slope_ref, q_ref, k_ref, v_ref, kaux_ref, vaux_ref, o_ref, kaug, vaug, kt, *bufs,
                 nb, blk, hd, topk):
    scale = hd ** -0.5
    negz = NEG_INF / scale
    slope = slope_ref[0][:, :1]
    kaug[:, :hd] = k_ref[0]
    kaug[:, hd:] = kaux_ref[...]
    for n in range(nb):
        kt[:, n * blk:(n + 1) * blk] = kaug[n * blk:(n + 1) * blk, :].T
    vaug[:, :hd] = v_ref[0]
    vaug[:, hd:] = vaux_ref[...]

    km = jnp.concatenate([jnp.mean(k_ref[0, n * blk:(n + 1) * blk, :].astype(F32), axis=0, keepdims=True)
                          for n in range(nb)], axis=0)
    hi = km.astype(BF16).astype(F32)
    mid = (km - hi).astype(BF16).astype(F32)
    lo = (km - hi - mid).astype(BF16).astype(F32)
    pad = [jnp.zeros((-3 * nb % (2 * SUBLANES), hd), F32)] if 3 * nb % (2 * SUBLANES) else []
    km3 = jnp.concatenate([hi, mid, lo] + pad, axis=0).astype(BF16)

    r = lax.broadcasted_iota(jnp.int32, (blk, blk), 0)
    c = lax.broadcasted_iota(jnp.int32, (blk, blk), 1)
    causal = r >= c
    keypos = lax.broadcasted_iota(jnp.int32, (1, blk), 1).astype(F32)
    nt = (((1,), (1,)), ((), ()))

    def scores(j):
        zbuf = bufs[j]
        q = q_ref[0, j * blk:(j + 1) * blk, :]
        select = j > topk
        if select:
            g3 = lax.dot_general(km3, q, nt, preferred_element_type=F32)
            gate = g3[0:nb] + g3[nb:2 * nb] + g3[2 * nb:3 * nb]
            g = [gate[m:m + 1, :] for m in range(j)]
            rows = []
            for n in range(j):
                beaten = jnp.zeros((1, blk), F32)
                for m in range(j):
                    if m != n:
                        ahead = (g[m] > g[n]) | (g[m] == g[n]) if m < n else (g[m] > g[n])
                        beaten = beaten + jnp.where(ahead, 1.0, 0.0)
                rows.append(jnp.where(beaten < topk, 0.0, negz))
            bias_t = jnp.concatenate(rows + [jnp.zeros((LANES - j, blk), F32)], axis=0)
            q = jnp.concatenate([q, bias_t.T.astype(BF16)], axis=1)
        mrun = None
        for n in range(j + 1):
            keys_t = kt[:, n * blk:(n + 1) * blk] if select else kt[:hd, n * blk:(n + 1) * blk]
            z = jnp.dot(q, keys_t, preferred_element_type=F32)
            u = z + (slope / scale) * (keypos + float(n * blk))
            if n == j:
                u = jnp.where(causal, u, negz)
            zbuf[:, n * blk:(n + 1) * blk] = u
            for half in range(blk // LANES):
                f = u[:, half * LANES:(half + 1) * LANES]
                mrun = f if mrun is None else jnp.maximum(mrun, f)
        return jnp.broadcast_to(jnp.max(mrun, axis=-1, keepdims=True), (blk, LANES))

    def attend(j, mb):
        zbuf, pbuf = bufs[j], bufs[nb + j]
        for n in range(j + 1):
            for half in range(blk // LANES):
                cols = slice(n * blk + half * LANES, n * blk + (half + 1) * LANES)
                pbuf[:, cols] = jnp.exp2((zbuf[:, cols] - mb) * (scale * LOG2E)).astype(BF16)
        acc = jnp.dot(pbuf[:, :(j + 1) * blk], vaug[:(j + 1) * blk, :], preferred_element_type=F32)
        o_ref[0, j * blk:(j + 1) * blk, :] = (acc[:, :hd] / acc[:, hd:hd + 1]).astype(BF16)

    order = list(range(nb - 1, -1, -1))
    mb_next = scores(order[0])
    for i, j in enumerate(order):
        mb = mb_next
        if i + 1 < nb:
            mb_next = scores(order[i + 1])
        attend(j, mb)


def _moba(q, k, v, *, n_heads):
    b, s, d = q.shape
    hd = d // n_heads
    blk = MOBA_BLOCK
    assert s % blk == 0 and hd == LANES and blk % LANES == 0
    nb = s // blk
    assert nb <= LANES
    topk = max(1, min(MOBA_TOPK, nb - 1))
    slopes = jnp.exp2(-8.0 * jnp.arange(1, n_heads + 1, dtype=F32) / n_heads)
    slopes = jnp.broadcast_to(slopes[:, None, None], (n_heads, 1, LANES))
    kaux = np.zeros((s, LANES), np.float32)
    kaux[np.arange(s), np.arange(s) // blk] = 1.0
    vaux = np.zeros((s, LANES), np.float32)
    vaux[:, 0] = 1.0
    kern = functools.partial(_moba_kernel, nb=nb, blk=blk, hd=hd, topk=topk)
    head = pl.BlockSpec((1, s, hd), lambda i, h: (i, 0, h))
    return pl.pallas_call(
        kern,
        out_shape=jax.ShapeDtypeStruct((b, s, d), BF16),
        grid=(b, n_heads),
        in_specs=[pl.BlockSpec((1, 1, LANES), lambda i, h: (h, 0, 0)), head, head, head,
                  _resident((s, LANES)), _resident((s, LANES))],
        out_specs=head,
        scratch_shapes=([pltpu.VMEM((s, hd + LANES), BF16), pltpu.VMEM((s, hd + LANES), BF16),
                         pltpu.VMEM((hd + LANES, s), BF16)]
                        + [pltpu.VMEM((blk, (j + 1) * blk), F32) for j in range(nb)]
                        + [pltpu.VMEM((blk, (j + 1) * blk), BF16) for j in range(nb)]),
        compiler_params=_params("arbitrary", "arbitrary"),
        name="moba_attn",
    )(slopes, q, k, v, jnp.asarray(kaux, BF16), jnp.asarray(vaux, BF16))


def _pair_order(n):
    todo = [(a, b) for a in range(n) for b in range(a + 1, n)]
    order = [todo.pop(0)]
    while todo:
        a, b = order[-1]
        nxt = next((p for p in todo if p[0] == a or p[1] == b), todo[0])
        todo.remove(nxt)
        order.append(nxt)
    return order


PAIRS = _pair_order(EXPERTS_PER_GROUP)
N_CLASSES = N_GROUPS * len(PAIRS)
META_CLASS, META_RANK, META_WA, META_WB = 0, 1, 2, 3
EXPERT_TILE = 256


def _router_kernel(h_ref, g_ref, wr_ref, br_ref, tri_ref, route_ref, cnt_ref, running):
    _route_tile(h_ref[...], g_ref, wr_ref, br_ref, tri_ref, route_ref, cnt_ref, running)


def _out_proj_router_kernel(h_ref, a_ref, w_ref, g_ref, wr_ref, br_ref, tri_ref, o_ref, route_ref, cnt_ref,
                            running):
    h = h_ref[...] + jnp.dot(a_ref[...], w_ref[...], preferred_element_type=F32)
    o_ref[...] = h
    _route_tile(h, g_ref, wr_ref, br_ref, tri_ref, route_ref, cnt_ref, running)


def _route_tile(h, g_ref, wr_ref, br_ref, tri_ref, route_ref, cnt_ref, running):
    @pl.when(pl.program_id(0) == 0)
    def _():
        running[...] = jnp.zeros_like(running)

    xn = _rms(h, g_ref[...])
    wr = wr_ref[...]
    xh, wh = xn.astype(BF16), wr.astype(BF16)
    xl, wl = (xn - xh.astype(F32)).astype(BF16), (wr - wh.astype(F32)).astype(BF16)
    logits = (jnp.dot(xh, wh, preferred_element_type=F32) + jnp.dot(xl, wh, preferred_element_type=F32)
              + jnp.dot(xh, wl, preferred_element_type=F32)) + br_ref[...]
    lt = logits.T
    tokens = lt.shape[1]

    def first_max(rows):
        mx = functools.reduce(jnp.maximum, rows)
        idx = jnp.full_like(mx, float(len(rows) - 1))
        for i in range(len(rows) - 2, -1, -1):
            idx = jnp.where(rows[i] == mx, float(i), idx)
        return mx, idx

    gl = [lt[i:i + 1, :] for i in range(N_GROUPS)]
    gmax, gidx = first_max(gl)
    g_w = 1.0 / sum(jnp.exp(x - gmax) for x in gl)
    el = []
    for e in range(EXPERTS_PER_GROUP):
        x = lt[EXPERT_LANE0 + e:EXPERT_LANE0 + e + 1, :]
        for grp in range(1, N_GROUPS):
            lane = EXPERT_LANE0 + grp * EXPERTS_PER_GROUP + e
            x = jnp.where(gidx == float(grp), lt[lane:lane + 1, :], x)
        el.append(x)
    m1, e1 = first_max(el)
    m2, e2 = first_max([jnp.where(e1 == float(e), -jnp.inf, x) for e, x in enumerate(el)])
    p2 = jnp.exp(m2 - m1)
    w1 = g_w / (1.0 + p2)
    w2 = g_w * p2 / (1.0 + p2)
    ea, eb = jnp.minimum(e1, e2), jnp.maximum(e1, e2)
    wa, wb = jnp.where(e1 < e2, w1, w2), jnp.where(e1 < e2, w2, w1)
    pair = sum(jnp.where((ea == a) & (eb == b), float(n), 0.0) for n, (a, b) in enumerate(PAIRS))
    cls = gidx * float(len(PAIRS)) + pair
    class_row = lax.broadcasted_iota(jnp.int32, (ROUTER_LANES, tokens), 0).astype(F32)
    onehot = jnp.where(class_row == cls, 1.0, 0.0)
    before = jnp.dot(onehot.astype(BF16), tri_ref[...], preferred_element_type=F32) + running[...]
    rank = jnp.sum(onehot * before, axis=0, keepdims=True)
    running[...] += jnp.sum(onehot, axis=1, keepdims=True)
    cnt_ref[...] = running[...]

    records = {META_CLASS: cls, META_RANK: rank, META_WA: wa, META_WB: wb}
    route_ref[...] = jnp.concatenate([records.get(i, jnp.zeros_like(cls)) for i in range(SUBLANES)], axis=0)


def _router(h2d, g, w_group, b_group, w_router, b_router, mix=None, *, tm=512):
    t, d = h2d.shape
    ng, _, ne = w_router.shape
    assert (ng, ne) == (N_GROUPS, EXPERTS_PER_GROUP) and N_CLASSES <= ROUTER_LANES
    wr = jnp.concatenate([w_group, jnp.transpose(w_router, (1, 0, 2)).reshape(d, ng * ne)], axis=1)
    br = jnp.concatenate([b_group, b_router.reshape(ng * ne)])
    pad = ROUTER_LANES - wr.shape[1]
    wr = jnp.pad(wr, ((0, 0), (0, pad)))
    br = jnp.pad(br, (0, pad)).reshape(1, ROUTER_LANES)
    tri = jnp.asarray(np.triu(np.ones((tm, tm), np.float32), 1), BF16)
    tile = pl.BlockSpec((tm, d), lambda i: (i, 0))
    route_shapes = [jax.ShapeDtypeStruct((SUBLANES, t), F32), jax.ShapeDtypeStruct((ROUTER_LANES, 1), F32)]
    route_specs = [pl.BlockSpec((SUBLANES, tm), lambda i: (0, i)), pl.BlockSpec((ROUTER_LANES, 1), lambda i: (0, 0))]
    route_in = [_resident((1, d)), _resident((d, ROUTER_LANES)), _resident((1, ROUTER_LANES)), _resident((tm, tm))]
    common = dict(grid=(t // tm,), scratch_shapes=[pltpu.VMEM((ROUTER_LANES, 1), F32)],
                  compiler_params=_params("arbitrary"))
    if mix is None:
        return pl.pallas_call(
            _router_kernel, out_shape=route_shapes, in_specs=[tile] + route_in, out_specs=route_specs,
            name="moe_router", **common,
        )(h2d, g.reshape(1, d), wr, br, tri)
    a, w = mix
    return pl.pallas_call(
        _out_proj_router_kernel,
        out_shape=[jax.ShapeDtypeStruct((t, d), F32)] + route_shapes,
        in_specs=[tile, pl.BlockSpec((tm, a.shape[1]), lambda i: (i, 0)), _resident(w.shape)] + route_in,
        out_specs=[tile] + route_specs,
        name="attn_out_proj_router", **common,
    )(h2d, a, w.astype(BF16), g.reshape(1, d), wr, br, tri)


ITEM_VALID, ITEM_FIRST_OF_TILE, ITEM_NEW_A, ITEM_NEW_B, ITEM_MORE_A, ITEM_MORE_B = 1, 2, 4, 8, 16, 32


def _prefix_sum(v):
    ids = jnp.arange(v.shape[0], dtype=jnp.int32)
    return jnp.sum(jnp.where(ids[None, :] <= ids[:, None], v[None, :], 0), axis=1)


def _sorted_layout(route, counts, n_tiles, n_items):
    cls = route[META_CLASS].astype(jnp.int32)
    rank = route[META_RANK].astype(jnp.int32)
    cnt = counts[:N_CLASSES, 0].astype(jnp.int32)
    ids = jnp.arange(N_CLASSES, dtype=jnp.int32)
    end = _prefix_sum(cnt)
    start = end - cnt
    pos = jnp.sum(jnp.where(cls[None, :] == ids[:, None], start[:, None], 0), axis=0) + rank

    first = start // EXPERT_TILE
    last = jnp.where(cnt > 0, (end - 1) // EXPERT_TILE, first - 1)
    items = last - first + 1
    item_end = _prefix_sum(items)
    k = jnp.arange(n_items, dtype=jnp.int32)
    item_cls = jnp.sum((item_end[None, :] <= k[:, None]).astype(jnp.int32), axis=1)
    valid = item_cls < N_CLASSES
    item_cls = jnp.where(valid, item_cls, jnp.max(jnp.where(cnt > 0, ids, 0)))
    sel = item_cls[:, None] == ids[None, :]
    pick = lambda v: jnp.sum(jnp.where(sel, v[None, :], 0), axis=1)
    tile = jnp.where(valid, pick(first) + k - pick(item_end - items), n_tiles - 1)
    lo = jnp.where(valid, jnp.clip(pick(start) - tile * EXPERT_TILE, 0, EXPERT_TILE), 0)
    hi = jnp.where(valid, jnp.clip(pick(end) - tile * EXPERT_TILE, 0, EXPERT_TILE), 0)

    grp, pair = item_cls // len(PAIRS), item_cls % len(PAIRS)
    ea = grp * EXPERTS_PER_GROUP + sum(jnp.where(pair == n, a, 0) for n, (a, _) in enumerate(PAIRS))
    eb = grp * EXPERTS_PER_GROUP + sum(jnp.where(pair == n, b, 0) for n, (_, b) in enumerate(PAIRS))
    prev = lambda v: jnp.concatenate([jnp.full((1,), -1, jnp.int32), v[:-1]])
    new_a, new_b = ea != prev(ea), eb != prev(eb)

    def upcoming(e, is_new):
        later = is_new[None, :] & (k[None, :] > k[:, None])
        nxt = jnp.min(jnp.where(later, k[None, :], n_items), axis=1)
        return jnp.sum(jnp.where(k[None, :] == nxt[:, None], e[None, :], 0), axis=1), nxt < n_items

    next_a, more_a = upcoming(ea, new_a)
    next_b, more_b = upcoming(eb, new_b)
    flags = (jnp.where(valid, ITEM_VALID, 0) + jnp.where(valid & (tile != prev(tile)), ITEM_FIRST_OF_TILE, 0)
             + jnp.where(new_a, ITEM_NEW_A, 0) + jnp.where(new_b, ITEM_NEW_B, 0)
             + jnp.where(more_a, ITEM_MORE_A, 0) + jnp.where(more_b, ITEM_MORE_B, 0))
    return pos, (tile, ea, eb, next_a, next_b, lo, hi, flags)


DISPATCH_CHUNK = 2048
ROW_DMA_UNROLL = 8


def _row_copy(src, dst, i, j, sem):
    return pltpu.make_async_copy(src.at[pl.ds(i, 1)], dst.at[pl.ds(j, 1)], sem)


DISPATCH_ROWS = 256


def _dispatch_kernel(pos_ref, h_ref, g_ref, route_ref, hxs_ref, rows, sems, *, d):
    s = pl.program_id(0)
    n = pl.num_programs(0) - 1
    prev = (s + 1) % 2

    @pl.when(s > 0)
    def _():
        base = (s - 1) * DISPATCH_CHUNK

        def issue(r, carry):
            _row_copy(rows.at[prev], hxs_ref, r, pos_ref[base + r], sems.at[prev]).start()
            return carry

        lax.fori_loop(0, DISPATCH_CHUNK, issue, 0, unroll=ROW_DMA_UNROLL)

    @pl.when(s < n)
    def _():
        def build(i, carry):
            r0 = pl.multiple_of(i * DISPATCH_ROWS, DISPATCH_ROWS)
            rec = route_ref[:, pl.ds(r0, DISPATCH_ROWS)]
            pad = jnp.zeros((LANES - SUBLANES, DISPATCH_ROWS), F32)
            rows[s % 2, pl.ds(r0, DISPATCH_ROWS), 0, :d] = _rms(h_ref[pl.ds(r0, DISPATCH_ROWS), :], g_ref[...])
            rows[s % 2, pl.ds(r0, DISPATCH_ROWS), 0, d:] = jnp.concatenate([rec, pad], axis=0).T
            return carry

        lax.fori_loop(0, DISPATCH_CHUNK // DISPATCH_ROWS, build, 0)

    @pl.when(s > 0)
    def _():
        pltpu.make_async_copy(rows.at[prev], hxs_ref.at[pl.ds(0, DISPATCH_CHUNK)], sems.at[prev]).wait()


def _dispatch(h2d, g, route, pos):
    t, d = h2d.shape
    w = d + ROUTER_LANES
    assert t % DISPATCH_CHUNK == 0 and DISPATCH_CHUNK % DISPATCH_ROWS == 0
    n = t // DISPATCH_CHUNK
    return pl.pallas_call(
        functools.partial(_dispatch_kernel, d=d),
        out_shape=jax.ShapeDtypeStruct((t, 1, w), F32),
        grid_spec=pltpu.PrefetchScalarGridSpec(
            num_scalar_prefetch=1,
            grid=(n + 1,),
            in_specs=[pl.BlockSpec((DISPATCH_CHUNK, d), lambda s, pos: (jnp.minimum(s, n - 1), 0)),
                      pl.BlockSpec((1, d), lambda s, pos: (0, 0)),
                      pl.BlockSpec((SUBLANES, DISPATCH_CHUNK), lambda s, pos: (0, jnp.minimum(s, n - 1)))],
            out_specs=pl.BlockSpec(memory_space=pl.ANY),
            scratch_shapes=[pltpu.VMEM((2, DISPATCH_CHUNK, 1, w), F32), pltpu.SemaphoreType.DMA((2,))],
        ),
        compiler_params=_params("arbitrary"),
        name="moe_dispatch",
    )(pos, h2d, g.reshape(1, d), route)


def _experts_kernel(tile_ref, ea_ref, eb_ref, na_ref, nb_ref, lo_ref, hi_ref, flag_ref, x_ref, wg_hbm, wu_hbm,
                    wd_hbm, y_ref, w1, w2, xs, stage_g, stage_u, stage_d, sems, *, d, hid, layer):
    del tile_ref
    k = pl.program_id(0)
    flags = flag_ref[k]

    def refresh(slot, now_ref, next_ref, new_bit, more_bit):
        def copies(e):
            return [pltpu.make_async_copy(src.at[layer, e], dst.at[slot], sems.at[slot, n])
                    for n, (src, dst) in enumerate(((wg_hbm, stage_g), (wu_hbm, stage_u), (wd_hbm, stage_d)))]

        @pl.when((flags & new_bit) != 0)
        def _():
            @pl.when(k == 0)
            def _():
                for c in copies(now_ref[k]):
                    c.start()

            for c in copies(now_ref[k]):
                c.wait()
            w1[:, 2 * slot * hid:(2 * slot + 1) * hid] = stage_g[slot].astype(BF16)
            w1[:, (2 * slot + 1) * hid:(2 * slot + 2) * hid] = stage_u[slot].astype(BF16)
            w2[slot * hid:(slot + 1) * hid, :] = stage_d[slot].astype(BF16)

            @pl.when((flags & more_bit) != 0)
            def _():
                for c in copies(next_ref[k]):
                    c.start()

    refresh(0, ea_ref, na_ref, ITEM_NEW_A, ITEM_MORE_A)
    refresh(1, eb_ref, nb_ref, ITEM_NEW_B, ITEM_MORE_B)

    @pl.when((flags & ITEM_VALID) != 0)
    def _():
        row = lax.broadcasted_iota(jnp.int32, (EXPERT_TILE, 1), 0)
        inside = (row >= lo_ref[k]) & (row < hi_ref[k])
        xs[...] = x_ref[:, 0, :]
        wa = jnp.where(inside, xs[:, d + META_WA:d + META_WA + 1], 0.0)
        wb = jnp.where(inside, xs[:, d + META_WB:d + META_WB + 1], 0.0)
        gu = jnp.dot(xs[:, :d].astype(BF16), w1[...], preferred_element_type=F32)
        ga, gb = gu[:, 0:hid], gu[:, 2 * hid:3 * hid]
        ha = ga * _sigmoid(ga) * gu[:, hid:2 * hid] * wa
        hb = gb * _sigmoid(gb) * gu[:, 3 * hid:] * wb
        y = jnp.dot(jnp.concatenate([ha, hb], axis=1).astype(BF16), w2[...], preferred_element_type=F32)

        @pl.when((flags & ITEM_FIRST_OF_TILE) != 0)
        def _():
            y_ref[...] = y

        @pl.when((flags & ITEM_FIRST_OF_TILE) == 0)
        def _():
            y_ref[...] += y


def _experts(hxs, items, w_gate, w_up, w_down, layer, *, d):
    t, _, w = hxs.shape
    hid = w_gate.shape[-1]
    n_items = items[0].shape[0]
    hbm = pl.BlockSpec(memory_space=pl.ANY)
    return pl.pallas_call(
        functools.partial(_experts_kernel, d=d, hid=hid, layer=layer),
        out_shape=jax.ShapeDtypeStruct((t, d), F32),
        grid_spec=pltpu.PrefetchScalarGridSpec(
            num_scalar_prefetch=len(items),
            grid=(n_items,),
            in_specs=[pl.BlockSpec((EXPERT_TILE, 1, w), lambda k, tile, *_: (tile[k], 0, 0)), hbm, hbm, hbm],
            out_specs=pl.BlockSpec((EXPERT_TILE, d), lambda k, tile, *_: (tile[k], 0)),
            scratch_shapes=[pltpu.VMEM((d, 4 * hid), BF16), pltpu.VMEM((2 * hid, d), BF16),
                            pltpu.VMEM((EXPERT_TILE, w), F32),
                            pltpu.VMEM((2, d, hid), F32), pltpu.VMEM((2, d, hid), F32), pltpu.VMEM((2, hid, d), F32),
                            pltpu.SemaphoreType.DMA((2, 3))],
        ),
        compiler_params=_params("arbitrary"),
        name="moe_experts",
    )(*items, hxs, w_gate, w_up, w_down)


def _combine_ple_kernel(pos_ref, h_ref, p_ref, g_ref, wg_ref, wp_ref, ys_ref, o_ref, ybuf, sems, *, tm):
    i = pl.program_id(0)
    n = pl.num_programs(0)
    slot = i % 2

    def wait(s):
        pltpu.make_async_copy(ys_ref.at[pl.ds(0, tm)], ybuf.at[s], sems.at[s]).wait()

    @pl.when(i == 0)
    def _():
        def issue(r, carry):
            _row_copy(ys_ref, ybuf.at[0], pos_ref[r], r, sems.at[0]).start()
            return carry
        lax.fori_loop(0, tm, issue, 0, unroll=ROW_DMA_UNROLL)

    nxt = jnp.minimum(i + 1, n - 1) * tm
    for r in range(tm):
        _row_copy(ys_ref, ybuf.at[1 - slot], pos_ref[nxt + r], r, sems.at[1 - slot]).start()

    wait(slot)
    h = h_ref[...] + ybuf[slot]
    gate = _sigmoid(jnp.dot(_rms(h, g_ref[...]).astype(BF16), wg_ref[...], preferred_element_type=F32))
    proj = jnp.dot(p_ref[...].astype(BF16), wp_ref[...], preferred_element_type=F32)
    o_ref[...] = h + gate * proj

    @pl.when(i == n - 1)
    def _():
        wait(1 - slot)


def _combine_ple(h2d, ys, pos, p3d, layer, g, w_gate, w_proj, *, tm=512):
    t, d = h2d.shape
    pd = p3d.shape[-1]
    tile = pl.BlockSpec((tm, d), lambda i, pos: (i, 0))
    res = lambda shape: pl.BlockSpec(shape, lambda i, pos: (0,) * len(shape), pipeline_mode=pl.Buffered(1))
    return pl.pallas_call(
        functools.partial(_combine_ple_kernel, tm=tm),
        out_shape=jax.ShapeDtypeStruct((t, d), F32),
        grid_spec=pltpu.PrefetchScalarGridSpec(
            num_scalar_prefetch=1,
            grid=(t // tm,),
            in_specs=[tile, pl.BlockSpec((None, tm, pd), lambda i, pos: (layer, i, 0)), res((1, d)), res((d, d)),
                      res((pd, d)), pl.BlockSpec(memory_space=pl.ANY)],
            out_specs=tile,
            scratch_shapes=[pltpu.VMEM((2, tm, d), F32), pltpu.SemaphoreType.DMA((2,))],
        ),
        compiler_params=_params("arbitrary"),
        name="moe_combine_ple",
    )(pos, h2d, p3d, g.reshape(1, d), w_gate.astype(BF16), w_proj.astype(BF16), ys)


def _moe_ple(h2d, mix, p3d, layer, g_ffn, w_group, b_group, w_router, b_router, w_gate, w_up, w_down, g_ple,
             ple_gate, ple_proj):
    t, d = h2d.shape
    assert t % EXPERT_TILE == 0
    n_tiles = t // EXPERT_TILE
    n_items = n_tiles + N_CLASSES - 1
    if mix is None:
        route, counts = _router(h2d, g_ffn, w_group, b_group, w_router, b_router)
    else:
        h2d, route, counts = _router(h2d, g_ffn, w_group, b_group, w_router, b_router, mix)
    pos, items = _sorted_layout(route, counts, n_tiles, n_items)
    hxs = _dispatch(h2d, g_ffn, route, pos)
    ys = _experts(hxs, items, w_gate, w_up, w_down, layer, d=d)
    return _combine_ple(h2d, ys, pos, p3d, layer, g_ple, ple_gate, ple_proj)


def kernel(x, p, g_mix, g_ffn, g_ple, conv_w_pw1, conv_b_pw1, conv_w_dw, conv_b_dw, conv_ln_g, conv_ln_b, conv_w_pw2, conv_b_pw2, attn_w_qkv, attn_q_gain, attn_k_gain, attn_w_o, moe_w_group, moe_b_group, moe_w_router, moe_b_router, moe_w_gate, moe_w_up, moe_w_down, ple_w_gate, ple_w_proj):
    b, s, d = x.shape
    depth = g_mix.shape[0]
    t = b * s
    h = x
    for i in range(depth):
        if i % 2 == 0:
            c = i // 2
            h = _conv_mixer(h.reshape(b, s, d), g_mix[i], conv_w_pw1[c], conv_b_pw1[c], conv_w_dw[c],
                            conv_b_dw[c], conv_ln_g[c], conv_ln_b[c], conv_w_pw2[c], conv_b_pw2[c])
            h = h.reshape(t, d)
            mix = None
        else:
            a = i // 2
            h = h.reshape(t, d)
            q, k, v = _qkv(h, g_mix[i], attn_w_qkv[a], attn_q_gain[a], attn_k_gain[a])
            o = _moba(q.reshape(b, s, d), k.reshape(b, s, d), v.reshape(b, s, d), n_heads=N_HEADS)
            mix = (o.reshape(t, d), attn_w_o[a])
        h = _moe_ple(h, mix, p.reshape(depth, t, -1), i, g_ffn[i], moe_w_group[i], moe_b_group[i],
                     moe_w_router[i], moe_b_router[i], moe_w_gate, moe_w_up, moe_w_down, g_ple[i], ple_w_gate[i],
                     ple_w_proj[i])
    return h.reshape(b, s, d)
```

```python
import functools

import jax
import jax.numpy as jnp
import numpy as np
from jax import lax
from jax.experimental import pallas as pl
from jax.experimental.pallas import tpu as pltpu

F32 = jnp.float32
BF16 = jnp.bfloat16

EPS = 1e-6
NEG_INF = -1e30
LOG2E = 1.4426950408889634

N_HEADS = 8
MOBA_BLOCK = 256
MOBA_TOPK = 3
N_GROUPS = 4
EXPERTS_PER_GROUP = 4
N_EXPERTS = N_GROUPS * EXPERTS_PER_GROUP

V7X_VMEM_BYTES = 64 * 1024 * 1024
LANES = 128
SUBLANES = 8
VMEM_LIMIT = V7X_VMEM_BYTES * 7 // 8

ROUTER_LANES = LANES
EXPERT_LANE0 = N_GROUPS

CONV_HALO = 32
CONV_ROWS = 64
CONV_COLS = 128
CONV_SHIFT = 4
CONV_PANEL = 256


def _params(*semantics):
    return pltpu.CompilerParams(dimension_semantics=semantics, vmem_limit_bytes=VMEM_LIMIT)


def _resident(shape):
    nd = len(shape)
    return pl.BlockSpec(shape, lambda *_: (0,) * nd, pipeline_mode=pl.Buffered(1))


def _rms(x, g):
    return x * lax.rsqrt(jnp.mean(x * x, axis=-1, keepdims=True) + EPS) * g


def _sigmoid(x):
    return 1.0 / (1.0 + jnp.exp(-x))


def _conv_mixer_kernel(x_ref, g_ref, w1_ref, b1_ref, wdw_ref, bdw_ref, lng_ref, lnb_ref,
                       w2_ref, b2_ref, o_ref, xnbuf, ubuf, ubuf4, cbuf, *, ts, d, kw):
    s = pl.program_id(1)
    x = x_ref[0]
    xnbuf[...] = _rms(x, g_ref[...]).astype(BF16)

    @pl.when(s == 0)
    def _():
        ubuf[0:CONV_HALO, :] = jnp.zeros((CONV_HALO, d), F32)
        ubuf4[0:CONV_HALO, :] = jnp.zeros((CONV_HALO, d), F32)
        ubuf4[ts + CONV_HALO - SUBLANES:ts + CONV_HALO, :] = jnp.zeros((SUBLANES, d), F32)

    def glu_panel(c0):
        a = jnp.dot(xnbuf[...], w1_ref[:, c0:c0 + CONV_PANEL], preferred_element_type=F32)
        a = a + b1_ref[:, c0:c0 + CONV_PANEL]
        g = jnp.dot(xnbuf[...], w1_ref[:, d + c0:d + c0 + CONV_PANEL], preferred_element_type=F32)
        g = g + b1_ref[:, d + c0:d + c0 + CONV_PANEL]
        u = a * _sigmoid(g)
        ubuf[CONV_HALO:CONV_HALO + ts, c0:c0 + CONV_PANEL] = u
        ubuf4[CONV_HALO - CONV_SHIFT:CONV_HALO - CONV_SHIFT + ts, c0:c0 + CONV_PANEL] = u

    first = CONV_HALO - (kw - 1)

    def conv_panel(c0):
        for cc in range(c0, c0 + CONV_PANEL, CONV_COLS):
            cols = slice(cc, cc + CONV_COLS)
            for r0 in range(0, ts, CONV_ROWS):
                acc = None
                for r in range(CONV_SHIFT):
                    rows = CONV_ROWS + (SUBLANES if r else 0)
                    part = None
                    for k in range(kw):
                        j = (first + k) % SUBLANES
                        if j % CONV_SHIFT == r:
                            src = ubuf if j < CONV_SHIFT else ubuf4
                            a0 = r0 + (first + k) - j
                            term = src[a0:a0 + rows, cols] * wdw_ref[k:k + 1, cols]
                            part = term if part is None else part + term
                    if part is not None:
                        part = part[r:r + CONV_ROWS]
                        acc = part if acc is None else acc + part
                cbuf[r0:r0 + CONV_ROWS, cols] = acc
        ubuf[0:CONV_HALO, c0:c0 + CONV_PANEL] = ubuf[ts:ts + CONV_HALO, c0:c0 + CONV_PANEL]
        ubuf4[0:CONV_HALO, c0:c0 + CONV_PANEL] = ubuf4[ts:ts + CONV_HALO, c0:c0 + CONV_PANEL]

    glu_panel(0)
    for c0 in range(0, d, CONV_PANEL):
        if c0 + CONV_PANEL < d:
            glu_panel(c0 + CONV_PANEL)
        conv_panel(c0)

    c = cbuf[...] + bdw_ref[...]
    mu = jnp.mean(c, axis=-1, keepdims=True)
    xc = c - mu
    y = xc * lax.rsqrt(jnp.mean(xc * xc, axis=-1, keepdims=True) + EPS) * lng_ref[...] + lnb_ref[...]
    y = y * _sigmoid(y)
    o_ref[0] = x + jnp.dot(y.astype(BF16), w2_ref[...], preferred_element_type=F32) + b2_ref[...]


def _conv_mixer(h, g, w1, b1, wdw, bdw, lng, lnb, w2, b2, *, ts=512):
    b, s, d = h.shape
    kw = wdw.shape[0]
    assert s % ts == 0 and ts % CONV_ROWS == 0 and ts >= CONV_HALO >= kw - 1
    assert CONV_HALO % SUBLANES == 0 and d % CONV_PANEL == 0 and CONV_PANEL % CONV_COLS == 0
    assert 2 * CONV_SHIFT == SUBLANES
    row = lambda a: a.reshape(1, -1)
    kern = functools.partial(_conv_mixer_kernel, ts=ts, d=d, kw=kw)
    return pl.pallas_call(
        kern,
        out_shape=jax.ShapeDtypeStruct((b, s, d), F32),
        grid=(b, s // ts),
        in_specs=[
            pl.BlockSpec((1, ts, d), lambda i, j: (i, j, 0)),
            _resident((1, d)), _resident((d, 2 * d)), _resident((1, 2 * d)),
            _resident((kw, d)), _resident((1, d)), _resident((1, d)), _resident((1, d)),
            _resident((d, d)), _resident((1, d)),
        ],
        out_specs=pl.BlockSpec((1, ts, d), lambda i, j: (i, j, 0)),
        scratch_shapes=[pltpu.VMEM((ts, d), BF16), pltpu.VMEM((ts + CONV_HALO, d), F32),
                        pltpu.VMEM((ts + CONV_HALO, d), F32), pltpu.VMEM((ts, d), F32)],
        compiler_params=_params("arbitrary", "arbitrary"),
        name="conv_mixer",
    )(h, row(g), w1.astype(BF16), row(b1), wdw, row(bdw), row(lng), row(lnb), w2.astype(BF16), row(b2))


def _qkv_kernel(x_ref, g_ref, w_ref, qg_ref, kg_ref, q_ref, k_ref, v_ref, *, d, hd):
    xn = _rms(x_ref[...], g_ref[...]).astype(BF16)
    qkv = jnp.dot(xn, w_ref[...], preferred_element_type=F32)
    for h in range(d // hd):
        cols = slice(h * hd, (h + 1) * hd)
        q_ref[:, cols] = _rms(qkv[:, h * hd:(h + 1) * hd], qg_ref[...]).astype(BF16)
        k_ref[:, cols] = _rms(qkv[:, d + h * hd:d + (h + 1) * hd], kg_ref[...]).astype(BF16)
    v_ref[...] = qkv[:, 2 * d:].astype(BF16)


def _qkv(h2d, g, w, qg, kg, *, tm=1024):
    t, d = h2d.shape
    hd = qg.shape[0]
    kern = functools.partial(_qkv_kernel, d=d, hd=hd)
    tile = pl.BlockSpec((tm, d), lambda i: (i, 0))
    return pl.pallas_call(
        kern,
        out_shape=[jax.ShapeDtypeStruct((t, d), BF16)] * 3,
        grid=(t // tm,),
        in_specs=[tile, _resident((1, d)), _resident((d, 3 * d)), _resident((1, hd)), _resident((1, hd))],
        out_specs=[tile, tile, tile],
        compiler_params=_params("arbitrary"),
        name="attn_qkv",
    )(h2d, g.reshape(1, d), w.astype(BF16), qg.reshape(1, hd), kg.reshape(1, hd))


def _moba_kernel(slope_ref, q_ref, k_ref, v_ref, kaux_ref, vaux_ref, o_ref, kaug, vaug, *bufs,
                 nb, blk, hd, topk):
    scale = hd ** -0.5
    negz = NEG_INF / scale
    slope = slope_ref[0][:, :1]
    kaug[:, :hd] = k_ref[0]
    kaug[:, hd:] = kaux_ref[...]
    vaug[:, :hd] = v_ref[0]
    vaug[:, hd:] = vaux_ref[...]

    km = jnp.concatenate([jnp.mean(k_ref[0, n * blk:(n + 1) * blk, :].astype(F32), axis=0, keepdims=True)
                          for n in range(nb)], axis=0)
    hi = km.astype(BF16).astype(F32)
    mid = (km - hi).astype(BF16).astype(F32)
    lo = (km - hi - mid).astype(BF16).astype(F32)
    pad = [jnp.zeros((-3 * nb % (2 * SUBLANES), hd), F32)] if 3 * nb % (2 * SUBLANES) else []
    km3 = jnp.concatenate([hi, mid, lo] + pad, axis=0).astype(BF16)

    r = lax.broadcasted_iota(jnp.int32, (blk, blk), 0)
    c = lax.broadcasted_iota(jnp.int32, (blk, blk), 1)
    causal = r >= c
    keypos = lax.broadcasted_iota(jnp.int32, (1, blk), 1).astype(F32)
    nt = (((1,), (1,)), ((), ()))

    def scores(j):
        zbuf = bufs[j]
        q = q_ref[0, j * blk:(j + 1) * blk, :]
        select = j > topk
        if select:
            g3 = lax.dot_general(km3, q, nt, preferred_element_type=F32)
            gate = g3[0:nb] + g3[nb:2 * nb] + g3[2 * nb:3 * nb]
            g = [gate[m:m + 1, :] for m in range(j)]
            rows = []
            for n in range(j):
                beaten = jnp.zeros((1, blk), F32)
                for m in range(j):
                    if m != n:
                        ahead = (g[m] > g[n]) | (g[m] == g[n]) if m < n else (g[m] > g[n])
                        beaten = beaten + jnp.where(ahead, 1.0, 0.0)
                rows.append(jnp.where(beaten < topk, 0.0, negz))
            bias_t = jnp.concatenate(rows + [jnp.zeros((LANES - j, blk), F32)], axis=0)
            q = jnp.concatenate([q, bias_t.T.astype(BF16)], axis=1)
        mrun = None
        for n in range(j + 1):
            keys = kaug[n * blk:(n + 1) * blk, :] if select else k_ref[0, n * blk:(n + 1) * blk, :]
            z = lax.dot_general(q, keys, nt, preferred_element_type=F32)
            u = z + (slope / scale) * (keypos + float(n * blk))
            if n == j:
                u = jnp.where(causal, u, negz)
            zbuf[:, n * blk:(n + 1) * blk] = u
            for half in range(blk // LANES):
                f = u[:, half * LANES:(half + 1) * LANES]
                mrun = f if mrun is None else jnp.maximum(mrun, f)
        return jnp.broadcast_to(jnp.max(mrun, axis=-1, keepdims=True), (blk, LANES))

    def attend(j, mb):
        zbuf, pbuf = bufs[j], bufs[nb + j]
        for n in range(j + 1):
            for half in range(blk // LANES):
                cols = slice(n * blk + half * LANES, n * blk + (half + 1) * LANES)
                pbuf[:, cols] = jnp.exp2((zbuf[:, cols] - mb) * (scale * LOG2E)).astype(BF16)
        acc = jnp.dot(pbuf[:, :(j + 1) * blk], vaug[:(j + 1) * blk, :], preferred_element_type=F32)
        o_ref[0, j * blk:(j + 1) * blk, :] = (acc[:, :hd] / acc[:, hd:hd + 1]).astype(BF16)

    order = list(range(nb - 1, -1, -1))
    mb_next = scores(order[0])
    for i, j in enumerate(order):
        mb = mb_next
        if i + 1 < nb:
            mb_next = scores(order[i + 1])
        attend(j, mb)


def _moba(q, k, v, *, n_heads):
    b, s, d = q.shape
    hd = d // n_heads
    blk = MOBA_BLOCK
    assert s % blk == 0 and hd == LANES and blk % LANES == 0
    nb = s // blk
    assert nb <= LANES
    topk = max(1, min(MOBA_TOPK, nb - 1))
    slopes = jnp.exp2(-8.0 * jnp.arange(1, n_heads + 1, dtype=F32) / n_heads)
    slopes = jnp.broadcast_to(slopes[:, None, None], (n_heads, 1, LANES))
    kaux = np.zeros((s, LANES), np.float32)
    kaux[np.arange(s), np.arange(s) // blk] = 1.0
    vaux = np.zeros((s, LANES), np.float32)
    vaux[:, 0] = 1.0
    kern = functools.partial(_moba_kernel, nb=nb, blk=blk, hd=hd, topk=topk)
    head = pl.BlockSpec((1, s, hd), lambda i, h: (i, 0, h))
    return pl.pallas_call(
        kern,
        out_shape=jax.ShapeDtypeStruct((b, s, d), BF16),
        grid=(b, n_heads),
        in_specs=[pl.BlockSpec((1, 1, LANES), lambda i, h: (h, 0, 0)), head, head, head,
                  _resident((s, LANES)), _resident((s, LANES))],
        out_specs=head,
        scratch_shapes=([pltpu.VMEM((s, hd + LANES), BF16), pltpu.VMEM((s, hd + LANES), BF16)]
                        + [pltpu.VMEM((blk, (j + 1) * blk), F32) for j in range(nb)]
                        + [pltpu.VMEM((blk, (j + 1) * blk), BF16) for j in range(nb)]),
        compiler_params=_params("arbitrary", "arbitrary"),
        name="moba_attn",
    )(slopes, q, k, v, jnp.asarray(kaux, BF16), jnp.asarray(vaux, BF16))


def _pair_order(n):
    todo = [(a, b) for a in range(n) for b in range(a + 1, n)]
    order = [todo.pop(0)]
    while todo:
        a, b = order[-1]
        nxt = next((p for p in todo if p[0] == a or p[1] == b), todo[0])
        todo.remove(nxt)
        order.append(nxt)
    return order


PAIRS = _pair_order(EXPERTS_PER_GROUP)
N_CLASSES = N_GROUPS * len(PAIRS)
META_CLASS, META_RANK, META_WA, META_WB = 0, 1, 2, 3
EXPERT_TILE = 256


def _router_kernel(h_ref, g_ref, wr_ref, br_ref, tri_ref, route_ref, cnt_ref, running):
    _route_tile(h_ref[...], g_ref, wr_ref, br_ref, tri_ref, route_ref, cnt_ref, running)


def _out_proj_router_kernel(h_ref, a_ref, w_ref, g_ref, wr_ref, br_ref, tri_ref, o_ref, route_ref, cnt_ref,
                            running):
    h = h_ref[...] + jnp.dot(a_ref[...], w_ref[...], preferred_element_type=F32)
    o_ref[...] = h
    _route_tile(h, g_ref, wr_ref, br_ref, tri_ref, route_ref, cnt_ref, running)


def _route_tile(h, g_ref, wr_ref, br_ref, tri_ref, route_ref, cnt_ref, running):
    @pl.when(pl.program_id(0) == 0)
    def _():
        running[...] = jnp.zeros_like(running)

    xn = _rms(h, g_ref[...])
    wr = wr_ref[...]
    xh, wh = xn.astype(BF16), wr.astype(BF16)
    xl, wl = (xn - xh.astype(F32)).astype(BF16), (wr - wh.astype(F32)).astype(BF16)
    logits = (jnp.dot(xh, wh, preferred_element_type=F32) + jnp.dot(xl, wh, preferred_element_type=F32)
              + jnp.dot(xh, wl, preferred_element_type=F32)) + br_ref[...]
    lt = logits.T
    tokens = lt.shape[1]

    def first_max(rows):
        mx = functools.reduce(jnp.maximum, rows)
        idx = jnp.full_like(mx, float(len(rows) - 1))
        for i in range(len(rows) - 2, -1, -1):
            idx = jnp.where(rows[i] == mx, float(i), idx)
        return mx, idx

    gl = [lt[i:i + 1, :] for i in range(N_GROUPS)]
    gmax, gidx = first_max(gl)
    g_w = 1.0 / sum(jnp.exp(x - gmax) for x in gl)
    el = []
    for e in range(EXPERTS_PER_GROUP):
        x = lt[EXPERT_LANE0 + e:EXPERT_LANE0 + e + 1, :]
        for grp in range(1, N_GROUPS):
            lane = EXPERT_LANE0 + grp * EXPERTS_PER_GROUP + e
            x = jnp.where(gidx == float(grp), lt[lane:lane + 1, :], x)
        el.append(x)
    m1, e1 = first_max(el)
    m2, e2 = first_max([jnp.where(e1 == float(e), -jnp.inf, x) for e, x in enumerate(el)])
    p2 = jnp.exp(m2 - m1)
    w1 = g_w / (1.0 + p2)
    w2 = g_w * p2 / (1.0 + p2)
    ea, eb = jnp.minimum(e1, e2), jnp.maximum(e1, e2)
    wa, wb = jnp.where(e1 < e2, w1, w2), jnp.where(e1 < e2, w2, w1)
    pair = sum(jnp.where((ea == a) & (eb == b), float(n), 0.0) for n, (a, b) in enumerate(PAIRS))
    cls = gidx * float(len(PAIRS)) + pair
    class_row = lax.broadcasted_iota(jnp.int32, (ROUTER_LANES, tokens), 0).astype(F32)
    onehot = jnp.where(class_row == cls, 1.0, 0.0)
    before = jnp.dot(onehot.astype(BF16), tri_ref[...], preferred_element_type=F32) + running[...]
    rank = jnp.sum(onehot * before, axis=0, keepdims=True)
    running[...] += jnp.sum(onehot, axis=1, keepdims=True)
    cnt_ref[...] = running[...]

    records = {META_CLASS: cls, META_RANK: rank, META_WA: wa, META_WB: wb}
    route_ref[...] = jnp.concatenate([records.get(i, jnp.zeros_like(cls)) for i in range(SUBLANES)], axis=0)


def _router(h2d, g, w_group, b_group, w_router, b_router, mix=None, *, tm=512):
    t, d = h2d.shape
    ng, _, ne = w_router.shape
    assert (ng, ne) == (N_GROUPS, EXPERTS_PER_GROUP) and N_CLASSES <= ROUTER_LANES
    wr = jnp.concatenate([w_group, jnp.transpose(w_router, (1, 0, 2)).reshape(d, ng * ne)], axis=1)
    br = jnp.concatenate([b_group, b_router.reshape(ng * ne)])
    pad = ROUTER_LANES - wr.shape[1]
    wr = jnp.pad(wr, ((0, 0), (0, pad)))
    br = jnp.pad(br, (0, pad)).reshape(1, ROUTER_LANES)
    tri = jnp.asarray(np.triu(np.ones((tm, tm), np.float32), 1), BF16)
    tile = pl.BlockSpec((tm, d), lambda i: (i, 0))
    route_shapes = [jax.ShapeDtypeStruct((SUBLANES, t), F32), jax.ShapeDtypeStruct((ROUTER_LANES, 1), F32)]
    route_specs = [pl.BlockSpec((SUBLANES, tm), lambda i: (0, i)), pl.BlockSpec((ROUTER_LANES, 1), lambda i: (0, 0))]
    route_in = [_resident((1, d)), _resident((d, ROUTER_LANES)), _resident((1, ROUTER_LANES)), _resident((tm, tm))]
    common = dict(grid=(t // tm,), scratch_shapes=[pltpu.VMEM((ROUTER_LANES, 1), F32)],
                  compiler_params=_params("arbitrary"))
    if mix is None:
        return pl.pallas_call(
            _router_kernel, out_shape=route_shapes, in_specs=[tile] + route_in, out_specs=route_specs,
            name="moe_router", **common,
        )(h2d, g.reshape(1, d), wr, br, tri)
    a, w = mix
    return pl.pallas_call(
        _out_proj_router_kernel,
        out_shape=[jax.ShapeDtypeStruct((t, d), F32)] + route_shapes,
        in_specs=[tile, pl.BlockSpec((tm, a.shape[1]), lambda i: (i, 0)), _resident(w.shape)] + route_in,
        out_specs=[tile] + route_specs,
        name="attn_out_proj_router", **common,
    )(h2d, a, w.astype(BF16), g.reshape(1, d), wr, br, tri)


ITEM_VALID, ITEM_FIRST_OF_TILE, ITEM_NEW_A, ITEM_NEW_B, ITEM_MORE_A, ITEM_MORE_B = 1, 2, 4, 8, 16, 32


def _prefix_sum(v):
    ids = jnp.arange(v.shape[0], dtype=jnp.int32)
    return jnp.sum(jnp.where(ids[None, :] <= ids[:, None], v[None, :], 0), axis=1)


def _sorted_layout(route, counts, n_tiles, n_items):
    cls = route[META_CLASS].astype(jnp.int32)
    rank = route[META_RANK].astype(jnp.int32)
    cnt = counts[:N_CLASSES, 0].astype(jnp.int32)
    ids = jnp.arange(N_CLASSES, dtype=jnp.int32)
    end = _prefix_sum(cnt)
    start = end - cnt
    pos = jnp.sum(jnp.where(cls[None, :] == ids[:, None], start[:, None], 0), axis=0) + rank

    first = start // EXPERT_TILE
    last = jnp.where(cnt > 0, (end - 1) // EXPERT_TILE, first - 1)
    items = last - first + 1
    item_end = _prefix_sum(items)
    k = jnp.arange(n_items, dtype=jnp.int32)
    item_cls = jnp.sum((item_end[None, :] <= k[:, None]).astype(jnp.int32), axis=1)
    valid = item_cls < N_CLASSES
    item_cls = jnp.where(valid, item_cls, jnp.max(jnp.where(cnt > 0, ids, 0)))
    sel = item_cls[:, None] == ids[None, :]
    pick = lambda v: jnp.sum(jnp.where(sel, v[None, :], 0), axis=1)
    tile = jnp.where(valid, pick(first) + k - pick(item_end - items), n_tiles - 1)
    lo = jnp.where(valid, jnp.clip(pick(start) - tile * EXPERT_TILE, 0, EXPERT_TILE), 0)
    hi = jnp.where(valid, jnp.clip(pick(end) - tile * EXPERT_TILE, 0, EXPERT_TILE), 0)

    grp, pair = item_cls // len(PAIRS), item_cls % len(PAIRS)
    ea = grp * EXPERTS_PER_GROUP + sum(jnp.where(pair == n, a, 0) for n, (a, _) in enumerate(PAIRS))
    eb = grp * EXPERTS_PER_GROUP + sum(jnp.where(pair == n, b, 0) for n, (_, b) in enumerate(PAIRS))
    prev = lambda v: jnp.concatenate([jnp.full((1,), -1, jnp.int32), v[:-1]])
    new_a, new_b = ea != prev(ea), eb != prev(eb)

    def upcoming(e, is_new):
        later = is_new[None, :] & (k[None, :] > k[:, None])
        nxt = jnp.min(jnp.where(later, k[None, :], n_items), axis=1)
        return jnp.sum(jnp.where(k[None, :] == nxt[:, None], e[None, :], 0), axis=1), nxt < n_items

    next_a, more_a = upcoming(ea, new_a)
    next_b, more_b = upcoming(eb, new_b)
    flags = (jnp.where(valid, ITEM_VALID, 0) + jnp.where(valid & (tile != prev(tile)), ITEM_FIRST_OF_TILE, 0)
             + jnp.where(new_a, ITEM_NEW_A, 0) + jnp.where(new_b, ITEM_NEW_B, 0)
             + jnp.where(more_a, ITEM_MORE_A, 0) + jnp.where(more_b, ITEM_MORE_B, 0))
    return pos, (tile, ea, eb, next_a, next_b, lo, hi, flags)


DISPATCH_CHUNK = 2048
ROW_DMA_UNROLL = 8


def _row_copy(src, dst, i, j, sem):
    return pltpu.make_async_copy(src.at[pl.ds(i, 1)], dst.at[pl.ds(j, 1)], sem)


DISPATCH_ROWS = 256


def _dispatch_kernel(pos_ref, h_ref, g_ref, route_ref, hxs_ref, rows, sems, *, d):
    s = pl.program_id(0)
    n = pl.num_programs(0) - 1
    prev = (s + 1) % 2

    @pl.when(s > 0)
    def _():
        base = (s - 1) * DISPATCH_CHUNK

        def issue(r, carry):
            _row_copy(rows.at[prev], hxs_ref, r, pos_ref[base + r], sems.at[prev]).start()
            return carry

        lax.fori_loop(0, DISPATCH_CHUNK, issue, 0, unroll=ROW_DMA_UNROLL)

    @pl.when(s < n)
    def _():
        def build(i, carry):
            r0 = pl.multiple_of(i * DISPATCH_ROWS, DISPATCH_ROWS)
            rec = route_ref[:, pl.ds(r0, DISPATCH_ROWS)]
            pad = jnp.zeros((LANES - SUBLANES, DISPATCH_ROWS), F32)
            rows[s % 2, pl.ds(r0, DISPATCH_ROWS), 0, :d] = _rms(h_ref[pl.ds(r0, DISPATCH_ROWS), :], g_ref[...])
            rows[s % 2, pl.ds(r0, DISPATCH_ROWS), 0, d:] = jnp.concatenate([rec, pad], axis=0).T
            return carry

        lax.fori_loop(0, DISPATCH_CHUNK // DISPATCH_ROWS, build, 0)

    @pl.when(s > 0)
    def _():
        pltpu.make_async_copy(rows.at[prev], hxs_ref.at[pl.ds(0, DISPATCH_CHUNK)], sems.at[prev]).wait()


def _dispatch(h2d, g, route, pos):
    t, d = h2d.shape
    w = d + ROUTER_LANES
    assert t % DISPATCH_CHUNK == 0 and DISPATCH_CHUNK % DISPATCH_ROWS == 0
    n = t // DISPATCH_CHUNK
    return pl.pallas_call(
        functools.partial(_dispatch_kernel, d=d),
        out_shape=jax.ShapeDtypeStruct((t, 1, w), F32),
        grid_spec=pltpu.PrefetchScalarGridSpec(
            num_scalar_prefetch=1,
            grid=(n + 1,),
            in_specs=[pl.BlockSpec((DISPATCH_CHUNK, d), lambda s, pos: (jnp.minimum(s, n - 1), 0)),
                      pl.BlockSpec((1, d), lambda s, pos: (0, 0)),
                      pl.BlockSpec((SUBLANES, DISPATCH_CHUNK), lambda s, pos: (0, jnp.minimum(s, n - 1)))],
            out_specs=pl.BlockSpec(memory_space=pl.ANY),
            scratch_shapes=[pltpu.VMEM((2, DISPATCH_CHUNK, 1, w), F32), pltpu.SemaphoreType.DMA((2,))],
        ),
        compiler_params=_params("arbitrary"),
        name="moe_dispatch",
    )(pos, h2d, g.reshape(1, d), route)


def _experts_kernel(tile_ref, ea_ref, eb_ref, na_ref, nb_ref, lo_ref, hi_ref, flag_ref, x_ref, wg_hbm, wu_hbm,
                    wd_hbm, y_ref, w1, w2, xs, stage_g, stage_u, stage_d, sems, *, d, hid, layer):
    del tile_ref
    k = pl.program_id(0)
    flags = flag_ref[k]

    def refresh(slot, now_ref, next_ref, new_bit, more_bit):
        def copies(e):
            return [pltpu.make_async_copy(src.at[layer, e], dst.at[slot], sems.at[slot, n])
                    for n, (src, dst) in enumerate(((wg_hbm, stage_g), (wu_hbm, stage_u), (wd_hbm, stage_d)))]

        @pl.when((flags & new_bit) != 0)
        def _():
            @pl.when(k == 0)
            def _():
                for c in copies(now_ref[k]):
                    c.start()

            for c in copies(now_ref[k]):
                c.wait()
            w1[:, 2 * slot * hid:(2 * slot + 1) * hid] = stage_g[slot].astype(BF16)
            w1[:, (2 * slot + 1) * hid:(2 * slot + 2) * hid] = stage_u[slot].astype(BF16)
            w2[slot * hid:(slot + 1) * hid, :] = stage_d[slot].astype(BF16)

            @pl.when((flags & more_bit) != 0)
            def _():
                for c in copies(next_ref[k]):
                    c.start()

    refresh(0, ea_ref, na_ref, ITEM_NEW_A, ITEM_MORE_A)
    refresh(1, eb_ref, nb_ref, ITEM_NEW_B, ITEM_MORE_B)

    @pl.when((flags & ITEM_VALID) != 0)
    def _():
        row = lax.broadcasted_iota(jnp.int32, (EXPERT_TILE, 1), 0)
        inside = (row >= lo_ref[k]) & (row < hi_ref[k])
        xs[...] = x_ref[:, 0, :]
        wa = jnp.where(inside, xs[:, d + META_WA:d + META_WA + 1], 0.0)
        wb = jnp.where(inside, xs[:, d + META_WB:d + META_WB + 1], 0.0)
        gu = jnp.dot(xs[:, :d].astype(BF16), w1[...], preferred_element_type=F32)
        ga, gb = gu[:, 0:hid], gu[:, 2 * hid:3 * hid]
        ha = ga * _sigmoid(ga) * gu[:, hid:2 * hid] * wa
        hb = gb * _sigmoid(gb) * gu[:, 3 * hid:] * wb
        y = jnp.dot(jnp.concatenate([ha, hb], axis=1).astype(BF16), w2[...], preferred_element_type=F32)

        @pl.when((flags & ITEM_FIRST_OF_TILE) != 0)
        def _():
            y_ref[...] = y

        @pl.when((flags & ITEM_FIRST_OF_TILE) == 0)
        def _():
            y_ref[...] += y


def _experts(hxs, items, w_gate, w_up, w_down, layer, *, d):
    t, _, w = hxs.shape
    hid = w_gate.shape[-1]
    n_items = items[0].shape[0]
    hbm = pl.BlockSpec(memory_space=pl.ANY)
    return pl.pallas_call(
        functools.partial(_experts_kernel, d=d, hid=hid, layer=layer),
        out_shape=jax.ShapeDtypeStruct((t, d), F32),
        grid_spec=pltpu.PrefetchScalarGridSpec(
            num_scalar_prefetch=len(items),
            grid=(n_items,),
            in_specs=[pl.BlockSpec((EXPERT_TILE, 1, w), lambda k, tile, *_: (tile[k], 0, 0)), hbm, hbm, hbm],
            out_specs=pl.BlockSpec((EXPERT_TILE, d), lambda k, tile, *_: (tile[k], 0)),
            scratch_shapes=[pltpu.VMEM((d, 4 * hid), BF16), pltpu.VMEM((2 * hid, d), BF16),
                            pltpu.VMEM((EXPERT_TILE, w), F32),
                            pltpu.VMEM((2, d, hid), F32), pltpu.VMEM((2, d, hid), F32), pltpu.VMEM((2, hid, d), F32),
                            pltpu.SemaphoreType.DMA((2, 3))],
        ),
        compiler_params=_params("arbitrary"),
        name="moe_experts",
    )(*items, hxs, w_gate, w_up, w_down)


def _combine_ple_kernel(pos_ref, h_ref, p_ref, g_ref, wg_ref, wp_ref, ys_ref, o_ref, ybuf, sems, *, tm):
    i = pl.program_id(0)
    n = pl.num_programs(0)
    slot = i % 2

    def wait(s):
        pltpu.make_async_copy(ys_ref.at[pl.ds(0, tm)], ybuf.at[s], sems.at[s]).wait()

    @pl.when(i == 0)
    def _():
        def issue(r, carry):
            _row_copy(ys_ref, ybuf.at[0], pos_ref[r], r, sems.at[0]).start()
            return carry
        lax.fori_loop(0, tm, issue, 0, unroll=ROW_DMA_UNROLL)

    nxt = jnp.minimum(i + 1, n - 1) * tm
    for r in range(tm):
        _row_copy(ys_ref, ybuf.at[1 - slot], pos_ref[nxt + r], r, sems.at[1 - slot]).start()

    wait(slot)
    h = h_ref[...] + ybuf[slot]
    gate = _sigmoid(jnp.dot(_rms(h, g_ref[...]).astype(BF16), wg_ref[...], preferred_element_type=F32))
    proj = jnp.dot(p_ref[...].astype(BF16), wp_ref[...], preferred_element_type=F32)
    o_ref[...] = h + gate * proj

    @pl.when(i == n - 1)
    def _():
        wait(1 - slot)


def _combine_ple(h2d, ys, pos, p3d, layer, g, w_gate, w_proj, *, tm=1024):
    t, d = h2d.shape
    pd = p3d.shape[-1]
    tile = pl.BlockSpec((tm, d), lambda i, pos: (i, 0))
    res = lambda shape: pl.BlockSpec(shape, lambda i, pos: (0,) * len(shape), pipeline_mode=pl.Buffered(1))
    return pl.pallas_call(
        functools.partial(_combine_ple_kernel, tm=tm),
        out_shape=jax.ShapeDtypeStruct((t, d), F32),
        grid_spec=pltpu.PrefetchScalarGridSpec(
            num_scalar_prefetch=1,
            grid=(t // tm,),
            in_specs=[tile, pl.BlockSpec((None, tm, pd), lambda i, pos: (layer, i, 0)), res((1, d)), res((d, d)),
                      res((pd, d)), pl.BlockSpec(memory_space=pl.ANY)],
            out_specs=tile,
            scratch_shapes=[pltpu.VMEM((2, tm, d), F32), pltpu.SemaphoreType.DMA((2,))],
        ),
        compiler_params=_params("arbitrary"),
        name="moe_combine_ple",
    )(pos, h2d, p3d, g.reshape(1, d), w_gate.astype(BF16), w_proj.astype(BF16), ys)


def _moe_ple(h2d, mix, p3d, layer, g_ffn, w_group, b_group, w_router, b_router, w_gate, w_up, w_down, g_ple,
             ple_gate, ple_proj):
    t, d = h2d.shape
    assert t % EXPERT_TILE == 0
    n_tiles = t // EXPERT_TILE
    n_items = n_tiles + N_CLASSES - 1
    if mix is None:
        route, counts = _router(h2d, g_ffn, w_group, b_group, w_router, b_router)
    else:
        h2d, route, counts = _router(h2d, g_ffn, w_group, b_group, w_router, b_router, mix)
    pos, items = _sorted_layout(route, counts, n_tiles, n_items)
    hxs = _dispatch(h2d, g_ffn, route, pos)
    ys = _experts(hxs, items, w_gate, w_up, w_down, layer, d=d)
    return _combine_ple(h2d, ys, pos, p3d, layer, g_ple, ple_gate, ple_proj)


def kernel(x, p, g_mix, g_ffn, g_ple, conv_w_pw1, conv_b_pw1, conv_w_dw, conv_b_dw, conv_ln_g, conv_ln_b, conv_w_pw2, conv_b_pw2, attn_w_qkv, attn_q_gain, attn_k_gain, attn_w_o, moe_w_group, moe_b_group, moe_w_router, moe_b_router, moe_w_gate, moe_w_up, moe_w_down, ple_w_gate, ple_w_proj):
    b, s, d = x.shape
    depth = g_mix.shape[0]
    t = b * s
    h = x
    for i in range(depth):
        if i % 2 == 0:
            c = i // 2
            h = _conv_mixer(h.reshape(b, s, d), g_mix[i], conv_w_pw1[c], conv_b_pw1[c], conv_w_dw[c],
                            conv_b_dw[c], conv_ln_g[c], conv_ln_b[c], conv_w_pw2[c], conv_b_pw2[c])
            h = h.reshape(t, d)
            mix = None
        else:
            a = i // 2
            h = h.reshape(t, d)
            q, k, v = _qkv(h, g_mix[i], attn_w_qkv[a], attn_q_gain[a], attn_k_gain[a])
            o = _moba(q.reshape(b, s, d), k.reshape(b, s, d), v.reshape(b, s, d), n_heads=N_HEADS)
            mix = (o.reshape(t, d), attn_w_o[a])
        h = _moe_ple(h, mix, p.reshape(depth, t, -1), i, g_ffn[i], moe_w_group[i], moe_b_group[i],
                     moe_w_router[i], moe_b_router[i], moe_w_gate, moe_w_up, moe_w_down, g_ple[i], ple_w_gate[i],
                     ple_w_proj[i])
    return h.reshape(b, s, d)
```
